```python
import jax, jax.numpy as jnp
from jax import lax
import numpy as np

D_MODEL = 1024
BATCH = 8
SEQ = 2048
DEPTH = 2

N_META = 16
D_MIX = D_MODEL
POOL_WIDTH = D_MIX // 4
POOL_WINDOWS = (2, 4, 8, 16)
POOL_GROUP = POOL_WIDTH // len(POOL_WINDOWS)
CONV_WIDTH = D_MIX // 4
CONV_K = 3
GLA_WIDTH = D_MIX // 2
GLA_HEADS = 4
GLA_DV = GLA_WIDTH // GLA_HEADS
GLA_DK = GLA_DV // 2
GLA_KW = GLA_HEADS * GLA_DK
GLA_GATE_RANK = 16
GLA_TAU = 16.0
GLA_CHUNK = 64
D_FF = 2816
FFN_RESID = 0.5
EPS = 1e-6
PROJ_SIZES = (POOL_WIDTH, CONV_WIDTH, CONV_WIDTH, CONV_WIDTH, GLA_KW, GLA_KW, GLA_WIDTH, GLA_GATE_RANK, GLA_WIDTH)
D_PROJ = sum(PROJ_SIZES)

kernel_name = 'hymba_pool_conv_gla_macaron_block'


def rms_norm(x, g):
    xf = x.astype(jnp.float32)
    y = xf * lax.rsqrt(jnp.mean(xf * xf, axis=-1, keepdims=True) + EPS)
    return (y * g.astype(jnp.float32)).astype(x.dtype)


def swiglu(h, w_gate, w_up, w_down):
    return (jax.nn.silu(h @ w_gate) * (h @ w_up)) @ w_down


def pool_mixer(p, w_group, scale):
    B, L, _ = p.shape
    pg = p.reshape(B, L, len(POOL_WINDOWS), POOL_GROUP).astype(jnp.float32)
    csum = jnp.cumsum(pg, axis=1)
    pos = jnp.arange(1, L + 1, dtype=jnp.float32)
    outs = []
    for i, w in enumerate(POOL_WINDOWS):
        ci = csum[:, :, i]
        prev = jnp.pad(ci, ((0, 0), (w, 0), (0, 0)))[:, :L]
        cnt = jnp.minimum(pos, float(w))[None, :, None]
        outs.append((ci - prev) / cnt - pg[:, :, i])
    m = jnp.stack(outs, axis=2).astype(p.dtype)
    y = jnp.einsum('blgc,gcd->blgd', m, w_group)
    return y.reshape(B, L, POOL_WIDTH) * scale


def conv_mixer(gate_b, gate_c, u, w_conv):
    z = gate_c * u
    y = lax.conv_general_dilated(
        z, w_conv[:, None, :].astype(z.dtype), window_strides=(1,),
        padding=[(CONV_K - 1, 0)], dimension_numbers=('NWC', 'WIO', 'NWC'),
        feature_group_count=CONV_WIDTH)
    return gate_b * y


def gla_mixer(q, k, v, g_low, out_gate, w_g2, b_g2, norm_g):
    B, L, _ = q.shape
    out_dtype = q.dtype
    f32 = jnp.float32
    log_a = jax.nn.log_sigmoid((g_low @ w_g2 + b_g2).astype(f32)) / GLA_TAU
    pad_front = (-N_META) % GLA_CHUNK
    Lp = L + pad_front
    n_chunks = Lp // GLA_CHUNK

    def to_chunks(t, d):
        t = jnp.pad(t.astype(f32), ((0, 0), (pad_front, 0), (0, 0)))
        t = t.reshape(B, n_chunks, GLA_CHUNK, GLA_HEADS, d)
        return t.transpose(1, 0, 3, 2, 4)

    qc = to_chunks(q, GLA_DK) * (GLA_DK ** -0.5)
    kc = to_chunks(k, GLA_DK)
    vc = to_chunks(v, GLA_DV)
    gc = to_chunks(log_a, GLA_DK)
    causal = jnp.tril(jnp.ones((GLA_CHUNK, GLA_CHUNK), dtype=bool))

    def step(S, inp):
        qi, ki, vi, gi = inp
        b = jnp.cumsum(gi, axis=2)
        inter = jnp.einsum('bhtd,bhde->bhte', qi * jnp.exp(b), S)
        diff = b[:, :, :, None, :] - b[:, :, None, :, :]
        decay = jnp.exp(jnp.where(causal[:, :, None], diff, -jnp.inf))
        scores = jnp.einsum('bhtd,bhsd,bhtsd->bhts', qi, ki, decay)
        intra = jnp.einsum('bhts,bhse->bhte', scores, vi)
        b_last = b[:, :, -1:, :]
        S = S * jnp.exp(b_last[:, :, 0, :])[..., None] + jnp.einsum(
            'bhsd,bhse->bhde', ki * jnp.exp(b_last - b), vi)
        return S, inter + intra

    S0 = jnp.zeros((B, GLA_HEADS, GLA_DK, GLA_DV), f32)
    _, o = lax.scan(step, S0, (qc, kc, vc, gc))
    o = o.transpose(1, 0, 3, 2, 4).reshape(B, Lp, GLA_HEADS, GLA_DV)[:, pad_front:]
    o = o * lax.rsqrt(jnp.mean(o * o, axis=-1, keepdims=True) + EPS)
    o = o * norm_g.reshape(GLA_HEADS, GLA_DV).astype(f32)
    o = o.reshape(B, L, GLA_WIDTH) * jax.nn.silu(out_gate.astype(f32))
    return o.astype(out_dtype)


def setup_inputs(seed: int = 0) -> dict:
    key = jax.random.key(seed)
    ks = jax.random.split(key, 22)
    f32 = jnp.float32
    nrm = lambda k, shape, s: jax.random.normal(k, shape, f32) * s
    gain = lambda k, shape: 1.0 + 0.02 * jax.random.normal(k, shape, f32)
    return {
        'x': nrm(ks[0], (BATCH, SEQ, D_MODEL), 1.0),
        'meta': nrm(ks[1], (N_META, D_MODEL), 1.0),
        'ffn1_pre': gain(ks[2], (DEPTH, D_MODEL)),
        'ffn1_post': gain(ks[3], (DEPTH, D_MODEL)),
        'ffn1_wg': nrm(ks[4], (DEPTH, D_MODEL, D_FF), D_MODEL ** -0.5),
        'ffn1_wu': nrm(ks[5], (DEPTH, D_MODEL, D_FF), D_MODEL ** -0.5),
        'ffn1_wd': nrm(ks[6], (DEPTH, D_FF, D_MODEL), D_FF ** -0.5),
        'mix_pre': gain(ks[7], (DEPTH, D_MODEL)),
        'mix_post': gain(ks[8], (DEPTH, D_MODEL)),
        'w_in': nrm(ks[9], (DEPTH, D_MODEL, D_PROJ), D_MODEL ** -0.5),
        'pool_w': nrm(ks[10], (DEPTH, len(POOL_WINDOWS), POOL_GROUP, POOL_GROUP), POOL_GROUP ** -0.5),
        'pool_scale': gain(ks[11], (DEPTH, POOL_WIDTH)),
        'conv_w': nrm(ks[12], (DEPTH, CONV_K, CONV_WIDTH), CONV_K ** -0.5),
        'gla_w2': nrm(ks[13], (DEPTH, GLA_GATE_RANK, GLA_KW), GLA_GATE_RANK ** -0.5),
        'gla_b2': nrm(ks[14], (DEPTH, GLA_KW), 0.01),
        'gla_norm': gain(ks[15], (DEPTH, GLA_WIDTH)),
        'w_out': nrm(ks[16], (DEPTH, D_MIX, D_MODEL), D_MIX ** -0.5),
        'ffn2_pre': gain(ks[17], (DEPTH, D_MODEL)),
        'ffn2_post': gain(ks[18], (DEPTH, D_MODEL)),
        'ffn2_wg': nrm(ks[19], (DEPTH, D_MODEL, D_FF), D_MODEL ** -0.5),
        'ffn2_wu': nrm(ks[20], (DEPTH, D_MODEL, D_FF), D_MODEL ** -0.5),
        'ffn2_wd': nrm(ks[21], (DEPTH, D_FF, D_MODEL), D_FF ** -0.5),
    }


def reference(x, meta, ffn1_pre, ffn1_post, ffn1_wg, ffn1_wu, ffn1_wd,
              mix_pre, mix_post, w_in, pool_w, pool_scale, conv_w,
              gla_w2, gla_b2, gla_norm, w_out,
              ffn2_pre, ffn2_post, ffn2_wg, ffn2_wu, ffn2_wd):
    B = x.shape[0]
    h = jnp.concatenate(
        [jnp.broadcast_to(meta.astype(x.dtype)[None], (B, N_META, D_MODEL)), x], axis=1)
    split_idx = np.cumsum(PROJ_SIZES)[:-1].tolist()
    for l in range(DEPTH):
        f = swiglu(rms_norm(h, ffn1_pre[l]), ffn1_wg[l], ffn1_wu[l], ffn1_wd[l])
        h = h + FFN_RESID * rms_norm(f, ffn1_post[l])
        a = rms_norm(h, mix_pre[l])
        proj = a @ w_in[l]
        p_pool, c_b, c_c, c_u, g_q, g_k, g_v, g_low, g_out = jnp.split(proj, split_idx, axis=-1)
        y_pool = pool_mixer(p_pool, pool_w[l], pool_scale[l])
        y_conv = conv_mixer(c_b, c_c, c_u, conv_w[l])
        y_gla = gla_mixer(g_q, g_k, g_v, g_low, g_out, gla_w2[l], gla_b2[l], gla_norm[l])
        mix = jnp.concatenate([y_pool, y_conv, y_gla], axis=-1) @ w_out[l]
        h = h + rms_norm(mix, mix_post[l])
        f = swiglu(rms_norm(h, ffn2_pre[l]), ffn2_wg[l], ffn2_wu[l], ffn2_wd[l])
        h = h + FFN_RESID * rms_norm(f, ffn2_post[l])
    return h[:, N_META:]
```

```python
import functools

import jax
import jax.numpy as jnp
import numpy as np
from jax import lax
from jax.experimental import pallas as pl
from jax.experimental.pallas import tpu as pltpu

F32 = jnp.float32
BF16 = jnp.bfloat16

D_MODEL = 1024
D_FF = 2816
N_META = 16
EPS = 1e-6
FFN_RESID = 0.5

POOL_WIDTH = 256
POOL_WINDOWS = (2, 4, 8, 16)
POOL_GROUP = 64
CONV_WIDTH = 256
CONV_K = 3
GLA_HEADS = 4
GLA_DK = 64
GLA_DV = 128
GLA_KW = GLA_HEADS * GLA_DK
GLA_WIDTH = GLA_HEADS * GLA_DV
GLA_GATE_RANK = 16
GLA_TAU = 16.0
CHUNK = 64

FRONT_PAD = 48
HEAD_ROWS = FRONT_PAD + N_META

OFF_POOL, OFF_CB, OFF_CC, OFF_CU, OFF_Q, OFF_K = 0, 256, 512, 768, 1024, 1280
OFF_V, OFF_OG, OFF_GL = 1536, 2048, 2560
GL_PAD = 128
D_PROJ_PACKED = OFF_GL + GL_PAD

V7X_VMEM_BYTES = 64 * 1024 * 1024
VMEM_LIMIT = 56 * 1024 * 1024

FFN_TILE = 512
FF_CHUNK = 256
MIX_TILE = 704
POOL_HIST = 16
CONV_HIST = 8


def _rms(x, g):
    return x * lax.rsqrt(jnp.mean(x * x, axis=-1, keepdims=True) + EPS) * g


def _ffn_kernel(h_ref, pre_ref, post_ref, wg_ref, wu_ref, wd_ref, o_ref, z_ref):
    h = h_ref[...]
    a = _rms(h, pre_ref[...]).astype(BF16)
    for j in range(D_FF // FF_CHUNK):
        sl = slice(j * FF_CHUNK, (j + 1) * FF_CHUNK)
        g = jnp.dot(a, wg_ref[:, sl], preferred_element_type=F32)
        u = jnp.dot(a, wu_ref[:, sl], preferred_element_type=F32)
        z_ref[:, sl] = (g * jax.nn.sigmoid(g) * u).astype(BF16)
    f = jnp.dot(z_ref[...], wd_ref[...], preferred_element_type=F32)
    o_ref[...] = h + FFN_RESID * _rms(f, post_ref[...])


def _resident(shape):
    return pl.BlockSpec(shape, lambda *_: (0,) * len(shape), pipeline_mode=pl.Buffered(1))


def _ffn(h, pre, post, wg, wu, wd):
    n = h.shape[0]
    return pl.pallas_call(
        _ffn_kernel,
        grid=(n // FFN_TILE,),
        in_specs=[
            pl.BlockSpec((FFN_TILE, D_MODEL), lambda i: (i, 0)),
            _resident((1, D_MODEL)),
            _resident((1, D_MODEL)),
            _resident((D_MODEL, D_FF)),
            _resident((D_MODEL, D_FF)),
            _resident((D_FF, D_MODEL)),
        ],
        out_specs=pl.BlockSpec((FFN_TILE, D_MODEL), lambda i: (i, 0)),
        out_shape=jax.ShapeDtypeStruct(h.shape, F32),
        scratch_shapes=[pltpu.VMEM((FFN_TILE, D_FF), BF16)],
        compiler_params=pltpu.CompilerParams(
            dimension_semantics=("arbitrary",), vmem_limit_bytes=VMEM_LIMIT),
        name="ffn",
    )(h, pre, post, wg, wu, wd)


def _mixer_kernel(h_ref, pre_ref, post_ref, win_ref, wout_ref, poolw_ref, pscale_ref,
                  convw_ref, w2_ref, b2_ref, gnorm_ref, o_ref,
                  a_s, pbuf, zbuf, q_s, k_s, v_s, la_s, og_s, y_s, state_s):
    t_idx = pl.program_id(1)
    ts = MIX_TILE

    @pl.when(t_idx == 0)
    def _():
        pbuf[0:POOL_HIST, :] = jnp.zeros((POOL_HIST, POOL_WIDTH), F32)
        zbuf[0:CONV_HIST, :] = jnp.zeros((CONV_HIST, CONV_WIDTH), F32)
        state_s[...] = jnp.zeros_like(state_s)

    h = h_ref[...]
    a_s[...] = _rms(h, pre_ref[...]).astype(BF16)

    def proj(off, width):
        return jnp.dot(a_s[...], win_ref[:, off:off + width], preferred_element_type=F32)

    pbuf[POOL_HIST:, :] = proj(OFF_POOL, POOL_WIDTH)
    x1 = pbuf[...]
    s2 = x1 + pltpu.roll(x1, 1, 0)
    s4 = s2 + pltpu.roll(s2, 2, 0)
    s8 = s4 + pltpu.roll(s4, 4, 0)
    s16 = s8 + pltpu.roll(s8, 8, 0)
    lane = lax.broadcasted_iota(jnp.int32, (ts, POOL_WIDTH), 1)
    row = lax.broadcasted_iota(jnp.int32, (ts, POOL_WIDTH), 0)
    win = jnp.where(lane < 64, 2, jnp.where(lane < 128, 4, jnp.where(lane < 192, 8, 16)))
    wsum = jnp.where(lane < 64, s2[POOL_HIST:], jnp.where(
        lane < 128, s4[POOL_HIST:], jnp.where(lane < 192, s8[POOL_HIST:], s16[POOL_HIST:])))
    pos1 = t_idx * ts + row - (FRONT_PAD - 1)
    cnt = jnp.clip(pos1, 1, win).astype(F32)
    p_cur = x1[POOL_HIST:]
    m = (wsum / cnt - p_cur).astype(BF16)
    y_pool = jnp.dot(m, poolw_ref[...], preferred_element_type=F32) * pscale_ref[...]
    y_s[:, 0:POOL_WIDTH] = y_pool.astype(BF16)
    pbuf[0:POOL_HIST, :] = x1[ts:ts + POOL_HIST]

    zbuf[CONV_HIST:, :] = proj(OFF_CC, CONV_WIDTH) * proj(OFF_CU, CONV_WIDTH)
    zz = zbuf[...]
    z1 = pltpu.roll(zz, 1, 0)
    z2 = pltpu.roll(zz, 2, 0)
    cw = convw_ref[...]
    yc = (cw[0:1, :] * z2[CONV_HIST:] + cw[1:2, :] * z1[CONV_HIST:] + cw[2:3, :] * zz[CONV_HIST:])
    y_s[:, POOL_WIDTH:POOL_WIDTH + CONV_WIDTH] = (proj(OFF_CB, CONV_WIDTH) * yc).astype(BF16)
    zbuf[0:CONV_HIST, :] = zz[ts:ts + CONV_HIST]

    q_s[...] = proj(OFF_Q, GLA_KW) * (GLA_DK ** -0.5)
    k_s[...] = proj(OFF_K, GLA_KW)
    v_s[...] = proj(OFF_V, GLA_WIDTH).astype(BF16)
    og_s[...] = proj(OFF_OG, GLA_WIDTH)
    g_low = proj(OFF_GL, GL_PAD).astype(BF16)
    gx = jnp.dot(g_low, w2_ref[...], preferred_element_type=F32) + b2_ref[...]
    log_sig = jnp.minimum(gx, 0.0) - jnp.log(1.0 + jnp.exp(-jnp.abs(gx)))
    la_s[...] = log_sig * (1.0 / GLA_TAU)

    ri = lax.broadcasted_iota(jnp.int32, (CHUNK, CHUNK), 0)
    ci = lax.broadcasted_iota(jnp.int32, (CHUNK, CHUNK), 1)
    tril = jnp.where(ri >= ci, 1.0, 0.0).astype(BF16)
    causal = ri >= ci

    def chunk_body(c, carry):
        r0 = pl.multiple_of(c * CHUNK, CHUNK)
        rows = pl.ds(r0, CHUNK)
        la = la_s[rows, :]
        la_hi = la.astype(BF16)
        la_lo = (la - la_hi.astype(F32)).astype(BF16)
        b = (jnp.dot(tril, la_hi, preferred_element_type=F32)
             + jnp.dot(tril, la_lo, preferred_element_type=F32))
        b_last = b[CHUNK - 1:CHUNK, :]
        q = q_s[rows, :]
        k = k_s[rows, :]
        v = v_s[rows, :]
        e_pos = jnp.exp(b)
        qe = (q * e_pos).astype(BF16)
        kn = (k * jnp.exp(-b)).astype(BF16)
        kd = (k * jnp.exp(b_last - b)).astype(BF16)
        s_decay = jnp.exp(b_last)
        outs = []
        for hd in range(GLA_HEADS):
            ks = slice(hd * GLA_DK, (hd + 1) * GLA_DK)
            vs = slice(hd * GLA_DV, (hd + 1) * GLA_DV)
            st = state_s[hd]
            inter = lax.dot_general(qe[:, ks], st.astype(BF16), (((1,), (1,)), ((), ())),
                                    preferred_element_type=F32)
            sc = lax.dot_general(qe[:, ks], kn[:, ks], (((1,), (1,)), ((), ())),
                                 preferred_element_type=F32)
            sc = jnp.where(causal, sc, 0.0).astype(BF16)
            intra = jnp.dot(sc, v[:, vs], preferred_element_type=F32)
            upd = lax.dot_general(v[:, vs], kd[:, ks], (((0,), (0,)), ((), ())),
                                  preferred_element_type=F32)
            state_s[hd] = st * s_decay[:, ks] + upd
            o = inter + intra
            o = o * lax.rsqrt(jnp.mean(o * o, axis=-1, keepdims=True) + EPS)
            outs.append(o)
        o_all = jnp.concatenate(outs, axis=-1) * gnorm_ref[...]
        og = og_s[rows, :]
        y_s[rows, POOL_WIDTH + CONV_WIDTH:] = (o_all * (og * jax.nn.sigmoid(og))).astype(BF16)
        return carry

    lax.fori_loop(0, ts // CHUNK, chunk_body, 0)

    mix = jnp.dot(y_s[...], wout_ref[...], preferred_element_type=F32)
    o_ref[...] = h + _rms(mix, post_ref[...])


def _mixer(h, n_batch, pre, post, win, wout, poolw, pscale, convw, w2, b2, gnorm):
    tiles_per_seq = h.shape[0] // n_batch // MIX_TILE
    ts = MIX_TILE
    return pl.pallas_call(
        _mixer_kernel,
        grid=(n_batch, tiles_per_seq),
        in_specs=[
            pl.BlockSpec((ts, D_MODEL), lambda b, t: (b * tiles_per_seq + t, 0)),
            _resident((1, D_MODEL)),
            _resident((1, D_MODEL)),
            _resident((D_MODEL, D_PROJ_PACKED)),
            _resident((D_MODEL, D_MODEL)),
            _resident((POOL_WIDTH, POOL_WIDTH)),
            _resident((1, POOL_WIDTH)),
            _resident((8, CONV_WIDTH)),
            _resident((GL_PAD, GLA_KW)),
            _resident((1, GLA_KW)),
            _resident((1, GLA_WIDTH)),
        ],
        out_specs=pl.BlockSpec((ts, D_MODEL), lambda b, t: (b * tiles_per_seq + t, 0)),
        out_shape=jax.ShapeDtypeStruct(h.shape, F32),
        scratch_shapes=[
            pltpu.VMEM((ts, D_MODEL), BF16),
            pltpu.VMEM((POOL_HIST + ts, POOL_WIDTH), F32),
            pltpu.VMEM((CONV_HIST + ts, CONV_WIDTH), F32),
            pltpu.VMEM((ts, GLA_KW), F32),
            pltpu.VMEM((ts, GLA_KW), F32),
            pltpu.VMEM((ts, GLA_WIDTH), BF16),
            pltpu.VMEM((ts, GLA_KW), F32),
            pltpu.VMEM((ts, GLA_WIDTH), F32),
            pltpu.VMEM((ts, D_MODEL), BF16),
            pltpu.VMEM((GLA_HEADS, GLA_DV, GLA_DK), F32),
        ],
        compiler_params=pltpu.CompilerParams(
            dimension_semantics=("arbitrary", "arbitrary"), vmem_limit_bytes=VMEM_LIMIT),
        name="mixer",
    )(h, pre, post, win, wout, poolw, pscale, convw, w2, b2, gnorm)


def _pack_w_in(w):
    pool, cb, cc, cu, q, k, v, gl, og = jnp.split(
        w, np.cumsum((256, 256, 256, 256, 256, 256, 512, 16, 512))[:-1].tolist(), axis=-1)
    gl = jnp.pad(gl, ((0, 0), (0, GL_PAD - GLA_GATE_RANK)))
    return jnp.concatenate([pool, cb, cc, cu, q, k, v, og, gl], axis=-1).astype(BF16)


def _block_diag(w):
    out = jnp.zeros((POOL_WIDTH, POOL_WIDTH), w.dtype)
    for g in range(len(POOL_WINDOWS)):
        sl = slice(g * POOL_GROUP, (g + 1) * POOL_GROUP)
        out = out.at[sl, sl].set(w[g])
    return out


def kernel(x, meta, ffn1_pre, ffn1_post, ffn1_wg, ffn1_wu, ffn1_wd, mix_pre, mix_post, w_in,
           pool_w, pool_scale, conv_w, gla_w2, gla_b2, gla_norm, w_out,
           ffn2_pre, ffn2_post, ffn2_wg, ffn2_wu, ffn2_wd):
    n_batch, seq, d = x.shape
    depth = w_in.shape[0]
    seq_pad = HEAD_ROWS + seq
    assert d == D_MODEL and seq_pad % MIX_TILE == 0 and (n_batch * seq_pad) % FFN_TILE == 0
    h = jnp.concatenate([
        jnp.zeros((n_batch, FRONT_PAD, d), x.dtype),
        jnp.broadcast_to(meta.astype(x.dtype)[None], (n_batch, N_META, d)),
        x], axis=1).reshape(n_batch * seq_pad, d)
    row = lambda v: v.reshape(1, -1).astype(F32)
    for l in range(depth):
        h = _ffn(h, row(ffn1_pre[l]), row(ffn1_post[l]), ffn1_wg[l].astype(BF16),
                 ffn1_wu[l].astype(BF16), ffn1_wd[l].astype(BF16))
        h = _mixer(
            h, n_batch, row(mix_pre[l]), row(mix_post[l]), _pack_w_in(w_in[l]),
            w_out[l].astype(BF16), _block_diag(pool_w[l]).astype(BF16), row(pool_scale[l]),
            jnp.pad(conv_w[l].astype(F32), ((0, 8 - CONV_K), (0, 0))),
            jnp.pad(gla_w2[l], ((0, GL_PAD - GLA_GATE_RANK), (0, 0))).astype(BF16),
            row(gla_b2[l]), row(gla_norm[l]))
        h = _ffn(h, row(ffn2_pre[l]), row(ffn2_post[l]), ffn2_wg[l].astype(BF16),
                 ffn2_wu[l].astype(BF16), ffn2_wd[l].astype(BF16))
    return h.reshape(n_batch, seq_pad, d)[:, HEAD_ROWS:]
```

```python
import functools

import jax
import jax.numpy as jnp
import numpy as np
from jax import lax
from jax.experimental import pallas as pl
from jax.experimental.pallas import tpu as pltpu

F32 = jnp.float32
BF16 = jnp.bfloat16

D_MODEL = 1024
D_FF = 2816
N_META = 16
EPS = 1e-6
FFN_RESID = 0.5

POOL_WIDTH = 256
POOL_WINDOWS = (2, 4, 8, 16)
POOL_GROUP = 64
CONV_WIDTH = 256
CONV_K = 3
GLA_HEADS = 4
GLA_DK = 64
GLA_DV = 128
GLA_KW = GLA_HEADS * GLA_DK
GLA_WIDTH = GLA_HEADS * GLA_DV
GLA_GATE_RANK = 16
GLA_TAU = 16.0
CHUNK = 64

FRONT_PAD = 48
HEAD_ROWS = FRONT_PAD + N_META

OFF_POOL, OFF_CB, OFF_CC, OFF_CU, OFF_Q, OFF_K = 0, 256, 512, 768, 1024, 1280
OFF_V, OFF_OG, OFF_GL = 1536, 2048, 2560
GL_PAD = 128
D_PROJ_PACKED = OFF_GL + GL_PAD

V7X_VMEM_BYTES = 64 * 1024 * 1024
VMEM_LIMIT = 56 * 1024 * 1024

FFN_TILE = 512
FF_CHUNK = 256
MIX_TILE = 704
POOL_HIST = 16
CONV_HIST = 8


def _rms(x, g):
    return x * lax.rsqrt(jnp.mean(x * x, axis=-1, keepdims=True) + EPS) * g


def _ffn_kernel(h_ref, pre_ref, post_ref, wg_ref, wu_ref, wd_ref, o_ref, z_ref):
    h = h_ref[...]
    a = _rms(h, pre_ref[...]).astype(BF16)
    for j in range(D_FF // FF_CHUNK):
        sl = slice(j * FF_CHUNK, (j + 1) * FF_CHUNK)
        g = jnp.dot(a, wg_ref[:, sl], preferred_element_type=F32)
        u = jnp.dot(a, wu_ref[:, sl], preferred_element_type=F32)
        z_ref[:, sl] = (g * jax.nn.sigmoid(g) * u).astype(BF16)
    f = jnp.dot(z_ref[...], wd_ref[...], preferred_element_type=F32)
    o_ref[...] = h + FFN_RESID * _rms(f, post_ref[...])


def _resident(shape):
    return pl.BlockSpec(shape, lambda *_: (0,) * len(shape), pipeline_mode=pl.Buffered(1))


def _ffn(h, pre, post, wg, wu, wd):
    n = h.shape[0]
    return pl.pallas_call(
        _ffn_kernel,
        grid=(n // FFN_TILE,),
        in_specs=[
            pl.BlockSpec((FFN_TILE, D_MODEL), lambda i: (i, 0)),
            _resident((1, D_MODEL)),
            _resident((1, D_MODEL)),
            _resident((D_MODEL, D_FF)),
            _resident((D_MODEL, D_FF)),
            _resident((D_FF, D_MODEL)),
        ],
        out_specs=pl.BlockSpec((FFN_TILE, D_MODEL), lambda i: (i, 0)),
        out_shape=jax.ShapeDtypeStruct(h.shape, F32),
        scratch_shapes=[pltpu.VMEM((FFN_TILE, D_FF), BF16)],
        compiler_params=pltpu.CompilerParams(
            dimension_semantics=("arbitrary",), vmem_limit_bytes=VMEM_LIMIT),
        name="ffn",
    )(h, pre, post, wg, wu, wd)


def _mixer_kernel(h_ref, pre_ref, post_ref, win_ref, wout_ref, poolw_ref, pscale_ref,
                  convw_ref, w2_ref, b2_ref, gnorm_ref, o_ref,
                  a_s, pbuf, zbuf, b_s, qe_s, kn_s, kd_s, v_s, o_s, i_s, u_s, sb_s, y_s, state_s):
    t_idx = pl.program_id(1)
    ts = MIX_TILE

    @pl.when(t_idx == 0)
    def _():
        pbuf[0:POOL_HIST, :] = jnp.zeros((POOL_HIST, POOL_WIDTH), F32)
        zbuf[0:CONV_HIST, :] = jnp.zeros((CONV_HIST, CONV_WIDTH), F32)
        state_s[...] = jnp.zeros_like(state_s)

    h = h_ref[...]
    a_s[...] = _rms(h, pre_ref[...]).astype(BF16)

    def proj(off, width):
        return jnp.dot(a_s[...], win_ref[:, off:off + width], preferred_element_type=F32)

    pbuf[POOL_HIST:, :] = proj(OFF_POOL, POOL_WIDTH)
    x1 = pbuf[...]
    s2 = x1 + pltpu.roll(x1, 1, 0)
    s4 = s2 + pltpu.roll(s2, 2, 0)
    s8 = s4 + pltpu.roll(s4, 4, 0)
    s16 = s8 + pltpu.roll(s8, 8, 0)
    lane = lax.broadcasted_iota(jnp.int32, (ts, POOL_WIDTH), 1)
    row = lax.broadcasted_iota(jnp.int32, (ts, POOL_WIDTH), 0)
    win = jnp.where(lane < 64, 2, jnp.where(lane < 128, 4, jnp.where(lane < 192, 8, 16)))
    wsum = jnp.where(lane < 64, s2[POOL_HIST:], jnp.where(
        lane < 128, s4[POOL_HIST:], jnp.where(lane < 192, s8[POOL_HIST:], s16[POOL_HIST:])))
    pos1 = t_idx * ts + row - (FRONT_PAD - 1)
    cnt = jnp.clip(pos1, 1, win).astype(F32)
    p_cur = x1[POOL_HIST:]
    m = (wsum / cnt - p_cur).astype(BF16)
    y_pool = jnp.dot(m, poolw_ref[...], preferred_element_type=F32) * pscale_ref[...]
    y_s[:, 0:POOL_WIDTH] = y_pool.astype(BF16)
    pbuf[0:POOL_HIST, :] = x1[ts:ts + POOL_HIST]

    zbuf[CONV_HIST:, :] = proj(OFF_CC, CONV_WIDTH) * proj(OFF_CU, CONV_WIDTH)
    zz = zbuf[...]
    z1 = pltpu.roll(zz, 1, 0)
    z2 = pltpu.roll(zz, 2, 0)
    cw = convw_ref[...]
    yc = (cw[0:1, :] * z2[CONV_HIST:] + cw[1:2, :] * z1[CONV_HIST:] + cw[2:3, :] * zz[CONV_HIST:])
    y_s[:, POOL_WIDTH:POOL_WIDTH + CONV_WIDTH] = (proj(OFF_CB, CONV_WIDTH) * yc).astype(BF16)
    zbuf[0:CONV_HIST, :] = zz[ts:ts + CONV_HIST]

    nc = ts // CHUNK
    v_s[...] = proj(OFF_V, GLA_WIDTH).astype(BF16)
    g_low = proj(OFF_GL, GL_PAD).astype(BF16)
    gx = jnp.dot(g_low, w2_ref[...], preferred_element_type=F32) + b2_ref[...]
    log_sig = jnp.minimum(gx, 0.0) - jnp.log(1.0 + jnp.exp(-jnp.abs(gx)))
    la = log_sig * (1.0 / GLA_TAU)
    la_hi = la.astype(BF16)
    la_lo = (la - la_hi.astype(F32)).astype(BF16)
    la_cat = jnp.concatenate([la_hi, la_lo], axis=1)

    ri = lax.broadcasted_iota(jnp.int32, (CHUNK, CHUNK), 0)
    ci = lax.broadcasted_iota(jnp.int32, (CHUNK, CHUNK), 1)
    tril = jnp.where(ri >= ci, 1.0, 0.0).astype(BF16)
    causal = ri >= ci

    for c in range(nc):
        rows = slice(c * CHUNK, (c + 1) * CHUNK)
        bb = jnp.dot(tril, la_cat[rows, :], preferred_element_type=F32)
        b_s[rows, :] = bb[:, :GLA_KW] + bb[:, GLA_KW:]

    b3 = b_s[...].reshape(nc, CHUNK, GLA_KW)
    b_last = b3[:, CHUNK - 1:CHUNK, :]
    q3 = (proj(OFF_Q, GLA_KW) * (GLA_DK ** -0.5)).reshape(nc, CHUNK, GLA_KW)
    k3 = proj(OFF_K, GLA_KW).reshape(nc, CHUNK, GLA_KW)
    qe_s[...] = (q3 * jnp.exp(b3)).reshape(ts, GLA_KW).astype(BF16)
    kn_s[...] = (k3 * jnp.exp(-b3)).reshape(ts, GLA_KW).astype(BF16)
    kd_s[...] = (k3 * jnp.exp(b_last - b3)).reshape(ts, GLA_KW).astype(BF16)
    decay = jnp.exp(b_last)

    nt = (((1,), (1,)), ((), ()))
    tn = (((0,), (0,)), ((), ()))
    ksl = [slice(hd * GLA_DK, (hd + 1) * GLA_DK) for hd in range(GLA_HEADS)]
    vsl = [slice(hd * GLA_DV, (hd + 1) * GLA_DV) for hd in range(GLA_HEADS)]
    pairs = [(c, hd) for c in range(nc) for hd in range(GLA_HEADS)]
    group = 2 * GLA_HEADS
    for g0 in range(0, len(pairs), group):
        grp = pairs[g0:g0 + group]
        rws = [slice(c * CHUNK, (c + 1) * CHUNK) for c, _ in grp]
        scs = [lax.dot_general(qe_s[r, ksl[hd]], kn_s[r, ksl[hd]], nt, preferred_element_type=F32)
               for r, (c, hd) in zip(rws, grp)]
        us = [lax.dot_general(v_s[r, vsl[hd]], kd_s[r, ksl[hd]], tn, preferred_element_type=F32)
              for r, (c, hd) in zip(rws, grp)]
        ps = [jnp.where(causal, sc, 0.0).astype(BF16) for sc in scs]
        pvs = [jnp.dot(p, v_s[r, vsl[hd]], preferred_element_type=F32)
               for p, r, (c, hd) in zip(ps, rws, grp)]
        for pv, u, r, (c, hd) in zip(pvs, us, rws, grp):
            o_s[r, vsl[hd]] = pv
            u_s[c * GLA_HEADS + hd] = u

    sts = [state_s[hd] for hd in range(GLA_HEADS)]
    for c in range(nc):
        for hd in range(GLA_HEADS):
            sb_s[c * GLA_HEADS + hd] = sts[hd].astype(BF16)
        sts = [sts[hd] * decay[c][:, ksl[hd]] + u_s[c * GLA_HEADS + hd]
               for hd in range(GLA_HEADS)]
    for hd in range(GLA_HEADS):
        state_s[hd] = sts[hd]

    for g0 in range(0, len(pairs), group):
        grp = pairs[g0:g0 + group]
        inters = [lax.dot_general(qe_s[c * CHUNK:(c + 1) * CHUNK, ksl[hd]],
                                  sb_s[c * GLA_HEADS + hd], nt, preferred_element_type=F32)
                  for c, hd in grp]
        for inter, (c, hd) in zip(inters, grp):
            i_s[c * CHUNK:(c + 1) * CHUNK, vsl[hd]] = inter

    og = proj(OFF_OG, GLA_WIDTH)
    gate = og * jax.nn.sigmoid(og)
    for hd in range(GLA_HEADS):
        vs = slice(hd * GLA_DV, (hd + 1) * GLA_DV)
        o = o_s[:, vs] + i_s[:, vs]
        o = o * lax.rsqrt(jnp.mean(o * o, axis=-1, keepdims=True) + EPS) * gnorm_ref[:, vs]
        y_s[:, POOL_WIDTH + CONV_WIDTH + hd * GLA_DV:POOL_WIDTH + CONV_WIDTH + (hd + 1) * GLA_DV] = (
            o * gate[:, vs]).astype(BF16)

    mix = jnp.dot(y_s[...], wout_ref[...], preferred_element_type=F32)
    o_ref[...] = h + _rms(mix, post_ref[...])


def _mixer(h, n_batch, pre, post, win, wout, poolw, pscale, convw, w2, b2, gnorm):
    tiles_per_seq = h.shape[0] // n_batch // MIX_TILE
    ts = MIX_TILE
    return pl.pallas_call(
        _mixer_kernel,
        grid=(n_batch, tiles_per_seq),
        in_specs=[
            pl.BlockSpec((ts, D_MODEL), lambda b, t: (b * tiles_per_seq + t, 0)),
            _resident((1, D_MODEL)),
            _resident((1, D_MODEL)),
            _resident((D_MODEL, D_PROJ_PACKED)),
            _resident((D_MODEL, D_MODEL)),
            _resident((POOL_WIDTH, POOL_WIDTH)),
            _resident((1, POOL_WIDTH)),
            _resident((8, CONV_WIDTH)),
            _resident((GL_PAD, GLA_KW)),
            _resident((1, GLA_KW)),
            _resident((1, GLA_WIDTH)),
        ],
        out_specs=pl.BlockSpec((ts, D_MODEL), lambda b, t: (b * tiles_per_seq + t, 0)),
        out_shape=jax.ShapeDtypeStruct(h.shape, F32),
        scratch_shapes=[
            pltpu.VMEM((ts, D_MODEL), BF16),
            pltpu.VMEM((POOL_HIST + ts, POOL_WIDTH), F32),
            pltpu.VMEM((CONV_HIST + ts, CONV_WIDTH), F32),
            pltpu.VMEM((ts, GLA_KW), F32),
            pltpu.VMEM((ts, GLA_KW), BF16),
            pltpu.VMEM((ts, GLA_KW), BF16),
            pltpu.VMEM((ts, GLA_KW), BF16),
            pltpu.VMEM((ts, GLA_WIDTH), BF16),
            pltpu.VMEM((ts, GLA_WIDTH), F32),
            pltpu.VMEM((ts, GLA_WIDTH), F32),
            pltpu.VMEM((ts // CHUNK * GLA_HEADS, GLA_DV, GLA_DK), F32),
            pltpu.VMEM((ts // CHUNK * GLA_HEADS, GLA_DV, GLA_DK), BF16),
            pltpu.VMEM((ts, D_MODEL), BF16),
            pltpu.VMEM((GLA_HEADS, GLA_DV, GLA_DK), F32),
        ],
        compiler_params=pltpu.CompilerParams(
            dimension_semantics=("arbitrary", "arbitrary"), vmem_limit_bytes=VMEM_LIMIT),
        name="mixer",
    )(h, pre, post, win, wout, poolw, pscale, convw, w2, b2, gnorm)


def _pack_w_in(w):
    pool, cb, cc, cu, q, k, v, gl, og = jnp.split(
        w, np.cumsum((256, 256, 256, 256, 256, 256, 512, 16, 512))[:-1].tolist(), axis=-1)
    gl = jnp.pad(gl, ((0, 0), (0, GL_PAD - GLA_GATE_RANK)))
    return jnp.concatenate([pool, cb, cc, cu, q, k, v, og, gl], axis=-1).astype(BF16)


def _block_diag(w):
    out = jnp.zeros((POOL_WIDTH, POOL_WIDTH), w.dtype)
    for g in range(len(POOL_WINDOWS)):
        sl = slice(g * POOL_GROUP, (g + 1) * POOL_GROUP)
        out = out.at[sl, sl].set(w[g])
    return out


def kernel(x, meta, ffn1_pre, ffn1_post, ffn1_wg, ffn1_wu, ffn1_wd, mix_pre, mix_post, w_in,
           pool_w, pool_scale, conv_w, gla_w2, gla_b2, gla_norm, w_out,
           ffn2_pre, ffn2_post, ffn2_wg, ffn2_wu, ffn2_wd):
    n_batch, seq, d = x.shape
    depth = w_in.shape[0]
    seq_pad = HEAD_ROWS + seq
    assert d == D_MODEL and seq_pad % MIX_TILE == 0 and (n_batch * seq_pad) % FFN_TILE == 0
    h = jnp.concatenate([
        jnp.zeros((n_batch, FRONT_PAD, d), x.dtype),
        jnp.broadcast_to(meta.astype(x.dtype)[None], (n_batch, N_META, d)),
        x], axis=1).reshape(n_batch * seq_pad, d)
    row = lambda v: v.reshape(1, -1).astype(F32)
    for l in range(depth):
        h = _ffn(h, row(ffn1_pre[l]), row(ffn1_post[l]), ffn1_wg[l].astype(BF16),
                 ffn1_wu[l].astype(BF16), ffn1_wd[l].astype(BF16))
        h = _mixer(
            h, n_batch, row(mix_pre[l]), row(mix_post[l]), _pack_w_in(w_in[l]),
            w_out[l].astype(BF16), _block_diag(pool_w[l]).astype(BF16), row(pool_scale[l]),
            jnp.pad(conv_w[l].astype(F32), ((0, 8 - CONV_K), (0, 0))),
            jnp.pad(gla_w2[l], ((0, GL_PAD - GLA_GATE_RANK), (0, 0))).astype(BF16),
            row(gla_b2[l]), row(gla_norm[l]))
        h = _ffn(h, row(ffn2_pre[l]), row(ffn2_post[l]), ffn2_wg[l].astype(BF16),
                 ffn2_wu[l].astype(BF16), ffn2_wd[l].astype(BF16))
    return h.reshape(n_batch, seq_pad, d)[:, HEAD_ROWS:]
```

```python
import functools

import jax
import jax.numpy as jnp
import numpy as np
from jax import lax
from jax.experimental import pallas as pl
from jax.experimental.pallas import tpu as pltpu

F32 = jnp.float32
BF16 = jnp.bfloat16

D_MODEL = 1024
D_FF = 2816
N_META = 16
EPS = 1e-6
FFN_RESID = 0.5

POOL_WIDTH = 256
POOL_WINDOWS = (2, 4, 8, 16)
POOL_GROUP = 64
CONV_WIDTH = 256
CONV_K = 3
GLA_HEADS = 4
GLA_DK = 64
GLA_DV = 128
GLA_KW = GLA_HEADS * GLA_DK
GLA_WIDTH = GLA_HEADS * GLA_DV
GLA_GATE_RANK = 16
GLA_TAU = 16.0
CHUNK = 64


OFF_POOL, OFF_CB, OFF_CC, OFF_CU, OFF_Q, OFF_K = 0, 256, 512, 768, 1024, 1280
OFF_V, OFF_OG, OFF_GL = 1536, 2048, 2560
GL_PAD = 128
D_PROJ_PACKED = OFF_GL + GL_PAD

V7X_VMEM_BYTES = 64 * 1024 * 1024
VMEM_LIMIT = 56 * 1024 * 1024

TILE = 512
FF_CHUNK = 256
META_PAD = TILE - N_META
POOL_HIST = 16
CONV_HIST = 8


def _rms(x, g):
    return x * lax.rsqrt(jnp.mean(x * x, axis=-1, keepdims=True) + EPS) * g


def _ffn_kernel(h_ref, meta_ref, pre_ref, post_ref, wg_ref, wu_ref, wd_ref, o_ref, z_ref, *,
                meta_at_step0):
    h = h_ref[...]
    if meta_at_step0:
        h = jnp.where(pl.program_id(0) == 0, meta_ref[...], h)
    a = _rms(h, pre_ref[...]).astype(BF16)
    for j in range(D_FF // FF_CHUNK):
        sl = slice(j * FF_CHUNK, (j + 1) * FF_CHUNK)
        g = jnp.dot(a, wg_ref[:, sl], preferred_element_type=F32)
        u = jnp.dot(a, wu_ref[:, sl], preferred_element_type=F32)
        z_ref[:, sl] = (g * jax.nn.sigmoid(g) * u).astype(BF16)
    f = jnp.dot(z_ref[...], wd_ref[...], preferred_element_type=F32)
    o_ref[...] = h + FFN_RESID * _rms(f, post_ref[...])


def _resident(shape):
    return pl.BlockSpec(shape, lambda *_: (0,) * len(shape), pipeline_mode=pl.Buffered(1))


def _ffn(src, meta_tile, pre, post, wg, wu, wd, *, mode):
    n_tok_tiles = src.shape[0] // TILE - (0 if mode == "first" else 1)
    if mode == "first":
        grid, src_map = n_tok_tiles + 1, lambda i: (jnp.maximum(i - 1, 0), 0)
    elif mode == "mid":
        grid, src_map = n_tok_tiles + 1, lambda i: (i, 0)
    else:
        grid, src_map = n_tok_tiles, lambda i: (i + 1, 0)
    out_rows = grid * TILE
    return pl.pallas_call(
        functools.partial(_ffn_kernel, meta_at_step0=(mode == "first")),
        grid=(grid,),
        in_specs=[
            pl.BlockSpec((TILE, D_MODEL), src_map),
            _resident((TILE, D_MODEL)),
            _resident((1, D_MODEL)),
            _resident((1, D_MODEL)),
            _resident((D_MODEL, D_FF)),
            _resident((D_MODEL, D_FF)),
            _resident((D_FF, D_MODEL)),
        ],
        out_specs=pl.BlockSpec((TILE, D_MODEL), lambda i: (i, 0)),
        out_shape=jax.ShapeDtypeStruct((out_rows, D_MODEL), F32),
        scratch_shapes=[pltpu.VMEM((TILE, D_FF), BF16)],
        compiler_params=pltpu.CompilerParams(
            dimension_semantics=("arbitrary",), vmem_limit_bytes=VMEM_LIMIT),
        name="ffn_" + mode,
    )(src, meta_tile, pre, post, wg, wu, wd)


def _mixer_kernel(h_ref, pre_ref, post_ref, win_ref, wout_ref, poolw_ref, pscale_ref,
                  convw_ref, w2_ref, b2_ref, gnorm_ref, o_ref,
                  a_s, pbuf, zbuf, b_s, qe_s, kn_s, kd_s, v_s, o_s, i_s, u_s, sb_s, y_s, state_s,
                  pmeta, zmeta, smeta, *, tiles_per_seq):
    step = pl.program_id(0)
    ts = TILE
    is_meta = step == 0

    @pl.when(is_meta)
    def _():
        pbuf[0:POOL_HIST, :] = jnp.zeros((POOL_HIST, POOL_WIDTH), F32)
        zbuf[0:CONV_HIST, :] = jnp.zeros((CONV_HIST, CONV_WIDTH), F32)
        state_s[...] = jnp.zeros_like(state_s)

    @pl.when(jnp.logical_and(step > 0, lax.rem(step - 1, tiles_per_seq) == 0))
    def _():
        pbuf[0:POOL_HIST, :] = pmeta[...]
        zbuf[0:CONV_HIST, :] = zmeta[...]
        state_s[...] = smeta[...]

    h = h_ref[...]
    a_s[...] = _rms(h, pre_ref[...]).astype(BF16)

    def proj(off, width):
        return jnp.dot(a_s[...], win_ref[:, off:off + width], preferred_element_type=F32)

    pbuf[POOL_HIST:, :] = proj(OFF_POOL, POOL_WIDTH)
    x1 = pbuf[...]
    s2 = x1 + pltpu.roll(x1, 1, 0)
    s4 = s2 + pltpu.roll(s2, 2, 0)
    s8 = s4 + pltpu.roll(s4, 4, 0)
    s16 = s8 + pltpu.roll(s8, 8, 0)
    lane = lax.broadcasted_iota(jnp.int32, (ts, POOL_WIDTH), 1)
    row = lax.broadcasted_iota(jnp.int32, (ts, POOL_WIDTH), 0)
    win = jnp.where(lane < 64, 2, jnp.where(lane < 128, 4, jnp.where(lane < 192, 8, 16)))
    wsum = jnp.where(lane < 64, s2[POOL_HIST:], jnp.where(
        lane < 128, s4[POOL_HIST:], jnp.where(lane < 192, s8[POOL_HIST:], s16[POOL_HIST:])))
    pos1 = jnp.where(is_meta, row - (META_PAD - 1), max(POOL_WINDOWS))
    cnt = jnp.clip(pos1, 1, win).astype(F32)
    p_cur = x1[POOL_HIST:]
    m = (wsum / cnt - p_cur).astype(BF16)
    y_pool = jnp.dot(m, poolw_ref[...], preferred_element_type=F32) * pscale_ref[...]
    y_s[:, 0:POOL_WIDTH] = y_pool.astype(BF16)
    pbuf[0:POOL_HIST, :] = x1[ts:ts + POOL_HIST]

    zbuf[CONV_HIST:, :] = proj(OFF_CC, CONV_WIDTH) * proj(OFF_CU, CONV_WIDTH)
    zz = zbuf[...]
    z1 = pltpu.roll(zz, 1, 0)
    z2 = pltpu.roll(zz, 2, 0)
    cw = convw_ref[...]
    yc = (cw[0:1, :] * z2[CONV_HIST:] + cw[1:2, :] * z1[CONV_HIST:] + cw[2:3, :] * zz[CONV_HIST:])
    y_s[:, POOL_WIDTH:POOL_WIDTH + CONV_WIDTH] = (proj(OFF_CB, CONV_WIDTH) * yc).astype(BF16)
    zbuf[0:CONV_HIST, :] = zz[ts:ts + CONV_HIST]

    nc = ts // CHUNK
    v_s[...] = proj(OFF_V, GLA_WIDTH).astype(BF16)
    g_low = proj(OFF_GL, GL_PAD).astype(BF16)
    gx = jnp.dot(g_low, w2_ref[...], preferred_element_type=F32) + b2_ref[...]
    log_sig = jnp.minimum(gx, 0.0) - jnp.log(1.0 + jnp.exp(-jnp.abs(gx)))
    la = log_sig * (1.0 / GLA_TAU)
    la_hi = la.astype(BF16)
    la_lo = (la - la_hi.astype(F32)).astype(BF16)
    la_cat = jnp.concatenate([la_hi, la_lo], axis=1)

    ri = lax.broadcasted_iota(jnp.int32, (CHUNK, CHUNK), 0)
    ci = lax.broadcasted_iota(jnp.int32, (CHUNK, CHUNK), 1)
    tril = jnp.where(ri >= ci, 1.0, 0.0).astype(BF16)
    causal = ri >= ci

    for c in range(nc):
        rows = slice(c * CHUNK, (c + 1) * CHUNK)
        bb = jnp.dot(tril, la_cat[rows, :], preferred_element_type=F32)
        b_s[rows, :] = bb[:, :GLA_KW] + bb[:, GLA_KW:]

    b3 = b_s[...].reshape(nc, CHUNK, GLA_KW)
    b_last = b3[:, CHUNK - 1:CHUNK, :]
    q3 = (proj(OFF_Q, GLA_KW) * (GLA_DK ** -0.5)).reshape(nc, CHUNK, GLA_KW)
    k3 = proj(OFF_K, GLA_KW).reshape(nc, CHUNK, GLA_KW)
    qe_s[...] = (q3 * jnp.exp(b3)).reshape(ts, GLA_KW).astype(BF16)
    kn_s[...] = (k3 * jnp.exp(-b3)).reshape(ts, GLA_KW).astype(BF16)
    kd_s[...] = (k3 * jnp.exp(b_last - b3)).reshape(ts, GLA_KW).astype(BF16)
    decay = jnp.exp(b_last)

    nt = (((1,), (1,)), ((), ()))
    tn = (((0,), (0,)), ((), ()))
    ksl = [slice(hd * GLA_DK, (hd + 1) * GLA_DK) for hd in range(GLA_HEADS)]
    vsl = [slice(hd * GLA_DV, (hd + 1) * GLA_DV) for hd in range(GLA_HEADS)]
    pairs = [(c, hd) for c in range(nc) for hd in range(GLA_HEADS)]
    group = 2 * GLA_HEADS
    for g0 in range(0, len(pairs), group):
        grp = pairs[g0:g0 + group]
        rws = [slice(c * CHUNK, (c + 1) * CHUNK) for c, _ in grp]
        scs = [lax.dot_general(qe_s[r, ksl[hd]], kn_s[r, ksl[hd]], nt, preferred_element_type=F32)
               for r, (c, hd) in zip(rws, grp)]
        us = [lax.dot_general(v_s[r, vsl[hd]], kd_s[r, ksl[hd]], tn, preferred_element_type=F32)
              for r, (c, hd) in zip(rws, grp)]
        ps = [jnp.where(causal, sc, 0.0).astype(BF16) for sc in scs]
        pvs = [jnp.dot(p, v_s[r, vsl[hd]], preferred_element_type=F32)
               for p, r, (c, hd) in zip(ps, rws, grp)]
        for pv, u, r, (c, hd) in zip(pvs, us, rws, grp):
            o_s[r, vsl[hd]] = pv
            u_s[c * GLA_HEADS + hd] = u

    sts = [state_s[hd] for hd in range(GLA_HEADS)]
    for c in range(nc):
        for hd in range(GLA_HEADS):
            sb_s[c * GLA_HEADS + hd] = sts[hd].astype(BF16)
        sts = [sts[hd] * decay[c][:, ksl[hd]] + u_s[c * GLA_HEADS + hd]
               for hd in range(GLA_HEADS)]
    for hd in range(GLA_HEADS):
        state_s[hd] = sts[hd]

    for g0 in range(0, len(pairs), group):
        grp = pairs[g0:g0 + group]
        inters = [lax.dot_general(qe_s[c * CHUNK:(c + 1) * CHUNK, ksl[hd]],
                                  sb_s[c * GLA_HEADS + hd], nt, preferred_element_type=F32)
                  for c, hd in grp]
        for inter, (c, hd) in zip(inters, grp):
            i_s[c * CHUNK:(c + 1) * CHUNK, vsl[hd]] = inter

    og = proj(OFF_OG, GLA_WIDTH)
    gate = og * jax.nn.sigmoid(og)
    for hd in range(GLA_HEADS):
        vs = slice(hd * GLA_DV, (hd + 1) * GLA_DV)
        o = o_s[:, vs] + i_s[:, vs]
        o = o * lax.rsqrt(jnp.mean(o * o, axis=-1, keepdims=True) + EPS) * gnorm_ref[:, vs]
        y_s[:, POOL_WIDTH + CONV_WIDTH + hd * GLA_DV:POOL_WIDTH + CONV_WIDTH + (hd + 1) * GLA_DV] = (
            o * gate[:, vs]).astype(BF16)

    mix = jnp.dot(y_s[...], wout_ref[...], preferred_element_type=F32)
    o_ref[...] = h + _rms(mix, post_ref[...])

    @pl.when(is_meta)
    def _():
        pmeta[...] = pbuf[0:POOL_HIST, :]
        zmeta[...] = zbuf[0:CONV_HIST, :]
        smeta[...] = state_s[...]


def _mixer(h, tiles_per_seq, pre, post, win, wout, poolw, pscale, convw, w2, b2, gnorm):
    ts = TILE
    return pl.pallas_call(
        functools.partial(_mixer_kernel, tiles_per_seq=tiles_per_seq),
        grid=(h.shape[0] // ts,),
        in_specs=[
            pl.BlockSpec((ts, D_MODEL), lambda i: (i, 0)),
            _resident((1, D_MODEL)),
            _resident((1, D_MODEL)),
            _resident((D_MODEL, D_PROJ_PACKED)),
            _resident((D_MODEL, D_MODEL)),
            _resident((POOL_WIDTH, POOL_WIDTH)),
            _resident((1, POOL_WIDTH)),
            _resident((8, CONV_WIDTH)),
            _resident((GL_PAD, GLA_KW)),
            _resident((1, GLA_KW)),
            _resident((1, GLA_WIDTH)),
        ],
        out_specs=pl.BlockSpec((ts, D_MODEL), lambda i: (i, 0)),
        out_shape=jax.ShapeDtypeStruct(h.shape, F32),
        scratch_shapes=[
            pltpu.VMEM((ts, D_MODEL), BF16),
            pltpu.VMEM((POOL_HIST + ts, POOL_WIDTH), F32),
            pltpu.VMEM((CONV_HIST + ts, CONV_WIDTH), F32),
            pltpu.VMEM((ts, GLA_KW), F32),
            pltpu.VMEM((ts, GLA_KW), BF16),
            pltpu.VMEM((ts, GLA_KW), BF16),
            pltpu.VMEM((ts, GLA_KW), BF16),
            pltpu.VMEM((ts, GLA_WIDTH), BF16),
            pltpu.VMEM((ts, GLA_WIDTH), F32),
            pltpu.VMEM((ts, GLA_WIDTH), F32),
            pltpu.VMEM((ts // CHUNK * GLA_HEADS, GLA_DV, GLA_DK), F32),
            pltpu.VMEM((ts // CHUNK * GLA_HEADS, GLA_DV, GLA_DK), BF16),
            pltpu.VMEM((ts, D_MODEL), BF16),
            pltpu.VMEM((GLA_HEADS, GLA_DV, GLA_DK), F32),
            pltpu.VMEM((POOL_HIST, POOL_WIDTH), F32),
            pltpu.VMEM((CONV_HIST, CONV_WIDTH), F32),
            pltpu.VMEM((GLA_HEADS, GLA_DV, GLA_DK), F32),
        ],
        compiler_params=pltpu.CompilerParams(
            dimension_semantics=("arbitrary",), vmem_limit_bytes=VMEM_LIMIT),
        name="mixer",
    )(h, pre, post, win, wout, poolw, pscale, convw, w2, b2, gnorm)


def _pack_w_in(w):
    pool, cb, cc, cu, q, k, v, gl, og = jnp.split(
        w, np.cumsum((256, 256, 256, 256, 256, 256, 512, 16, 512))[:-1].tolist(), axis=-1)
    gl = jnp.pad(gl, ((0, 0), (0, GL_PAD - GLA_GATE_RANK)))
    return jnp.concatenate([pool, cb, cc, cu, q, k, v, og, gl], axis=-1).astype(BF16)


def _block_diag(w):
    out = jnp.zeros((POOL_WIDTH, POOL_WIDTH), w.dtype)
    for g in range(len(POOL_WINDOWS)):
        sl = slice(g * POOL_GROUP, (g + 1) * POOL_GROUP)
        out = out.at[sl, sl].set(w[g])
    return out


def kernel(x, meta, ffn1_pre, ffn1_post, ffn1_wg, ffn1_wu, ffn1_wd, mix_pre, mix_post, w_in,
           pool_w, pool_scale, conv_w, gla_w2, gla_b2, gla_norm, w_out,
           ffn2_pre, ffn2_post, ffn2_wg, ffn2_wu, ffn2_wd):
    n_batch, seq, d = x.shape
    depth = w_in.shape[0]
    assert d == D_MODEL and seq % TILE == 0 and meta.shape[0] == N_META
    meta_tile = jnp.pad(meta.astype(F32), ((META_PAD, 0), (0, 0)))
    h = x.reshape(n_batch * seq, d)
    row = lambda v: v.reshape(1, -1).astype(F32)
    for l in range(depth):
        h = _ffn(h, meta_tile, row(ffn1_pre[l]), row(ffn1_post[l]), ffn1_wg[l].astype(BF16),
                 ffn1_wu[l].astype(BF16), ffn1_wd[l].astype(BF16),
                 mode="first" if l == 0 else "mid")
        h = _mixer(
            h, seq // TILE, row(mix_pre[l]), row(mix_post[l]), _pack_w_in(w_in[l]),
            w_out[l].astype(BF16), _block_diag(pool_w[l]).astype(BF16), row(pool_scale[l]),
            jnp.pad(conv_w[l].astype(F32), ((0, 8 - CONV_K), (0, 0))),
            jnp.pad(gla_w2[l], ((0, GL_PAD - GLA_GATE_RANK), (0, 0))).astype(BF16),
            row(gla_b2[l]), row(gla_norm[l]))
        h = _ffn(h, meta_tile, row(ffn2_pre[l]), row(ffn2_post[l]), ffn2_wg[l].astype(BF16),
                 ffn2_wu[l].astype(BF16), ffn2_wd[l].astype(BF16),
                 mode="last" if l == depth - 1 else "mid")
    return h.reshape(n_batch, seq, d)
```

```python
import functools

import jax
import jax.numpy as jnp
import numpy as np
from jax import lax
from jax.experimental import pallas as pl
from jax.experimental.pallas import tpu as pltpu

F32 = jnp.float32
BF16 = jnp.bfloat16

D_MODEL = 1024
D_FF = 2816
N_META = 16
EPS = 1e-6
FFN_RESID = 0.5

POOL_WIDTH = 256
POOL_WINDOWS = (2, 4, 8, 16)
POOL_GROUP = 64
CONV_WIDTH = 256
CONV_K = 3
GLA_HEADS = 4
GLA_DK = 64
GLA_DV = 128
GLA_KW = GLA_HEADS * GLA_DK
GLA_WIDTH = GLA_HEADS * GLA_DV
GLA_GATE_RANK = 16
GLA_TAU = 16.0
LOG2_E = 1.4426950408889634
CHUNK = 64


OFF_POOL, OFF_CB, OFF_CC, OFF_CU, OFF_Q, OFF_K = 0, 256, 512, 768, 1024, 1280
OFF_V, OFF_OG, OFF_GL = 1536, 2048, 2560
GL_PAD = 128
D_PROJ_PACKED = OFF_GL + GL_PAD

V7X_VMEM_BYTES = 64 * 1024 * 1024
VMEM_LIMIT = 56 * 1024 * 1024

TILE = 512
FF_CHUNK = 256
META_PAD = TILE - N_META
POOL_HIST = 16
CONV_HIST = 8


def _rms(x, g):
    return x * lax.rsqrt(jnp.mean(x * x, axis=-1, keepdims=True) + EPS) * g


def _ffn_kernel(h_ref, meta_ref, pre_ref, post_ref, wg_ref, wu_ref, wd_ref, o_ref, z_ref, *,
                meta_at_step0):
    h = h_ref[...]
    if meta_at_step0:
        h = jnp.where(pl.program_id(0) == 0, meta_ref[...], h)
    a = _rms(h, pre_ref[...]).astype(BF16)
    for j in range(D_FF // FF_CHUNK):
        sl = slice(j * FF_CHUNK, (j + 1) * FF_CHUNK)
        g = jnp.dot(a, wg_ref[:, sl], preferred_element_type=F32)
        u = jnp.dot(a, wu_ref[:, sl], preferred_element_type=F32)
        z_ref[:, sl] = (g * jax.nn.sigmoid(g) * u).astype(BF16)
    f = jnp.dot(z_ref[...], wd_ref[...], preferred_element_type=F32)
    o_ref[...] = h + FFN_RESID * _rms(f, post_ref[...])


def _resident(shape):
    return pl.BlockSpec(shape, lambda *_: (0,) * len(shape), pipeline_mode=pl.Buffered(1))


def _ffn(src, meta_tile, pre, post, wg, wu, wd, *, mode):
    n_tok_tiles = src.shape[0] // TILE - (0 if mode == "first" else 1)
    if mode == "first":
        grid, src_map = n_tok_tiles + 1, lambda i: (jnp.maximum(i - 1, 0), 0)
    elif mode == "mid":
        grid, src_map = n_tok_tiles + 1, lambda i: (i, 0)
    else:
        grid, src_map = n_tok_tiles, lambda i: (i + 1, 0)
    out_rows = grid * TILE
    return pl.pallas_call(
        functools.partial(_ffn_kernel, meta_at_step0=(mode == "first")),
        grid=(grid,),
        in_specs=[
            pl.BlockSpec((TILE, D_MODEL), src_map),
            _resident((TILE, D_MODEL)),
            _resident((1, D_MODEL)),
            _resident((1, D_MODEL)),
            _resident((D_MODEL, D_FF)),
            _resident((D_MODEL, D_FF)),
            _resident((D_FF, D_MODEL)),
        ],
        out_specs=pl.BlockSpec((TILE, D_MODEL), lambda i: (i, 0)),
        out_shape=jax.ShapeDtypeStruct((out_rows, D_MODEL), F32),
        scratch_shapes=[pltpu.VMEM((TILE, D_FF), BF16)],
        compiler_params=pltpu.CompilerParams(
            dimension_semantics=("arbitrary",), vmem_limit_bytes=VMEM_LIMIT),
        name="ffn_" + mode,
    )(src, meta_tile, pre, post, wg, wu, wd)


def _mixer_kernel(h_ref, pre_ref, post_ref, win_ref, wout_ref, poolw_ref, pscale_ref,
                  convw_ref, w2_ref, b2_ref, gnorm_ref, o_ref,
                  a_s, pbuf, zbuf, b_s, qe_s, kd_s, lq_s, lk_s, v_s, o_s, i_s, u_s, sb_s, y_s, state_s,
                  pmeta, zmeta, smeta, *, tiles_per_seq):
    step = pl.program_id(0)
    ts = TILE
    is_meta = step == 0

    @pl.when(is_meta)
    def _():
        pbuf[0:POOL_HIST, :] = jnp.zeros((POOL_HIST, POOL_WIDTH), F32)
        zbuf[0:CONV_HIST, :] = jnp.zeros((CONV_HIST, CONV_WIDTH), F32)
        state_s[...] = jnp.zeros_like(state_s)

    @pl.when(jnp.logical_and(step > 0, lax.rem(step - 1, tiles_per_seq) == 0))
    def _():
        pbuf[0:POOL_HIST, :] = pmeta[...]
        zbuf[0:CONV_HIST, :] = zmeta[...]
        state_s[...] = smeta[...]

    h = h_ref[...]
    a_s[...] = _rms(h, pre_ref[...]).astype(BF16)

    def proj(off, width):
        return jnp.dot(a_s[...], win_ref[:, off:off + width], preferred_element_type=F32)

    pbuf[POOL_HIST:, :] = proj(OFF_POOL, POOL_WIDTH)
    x1 = pbuf[...]
    s2 = x1 + pltpu.roll(x1, 1, 0)
    s4 = s2 + pltpu.roll(s2, 2, 0)
    s8 = s4 + pltpu.roll(s4, 4, 0)
    s16 = s8 + pltpu.roll(s8, 8, 0)
    lane = lax.broadcasted_iota(jnp.int32, (ts, POOL_WIDTH), 1)
    row = lax.broadcasted_iota(jnp.int32, (ts, POOL_WIDTH), 0)
    win = jnp.where(lane < 64, 2, jnp.where(lane < 128, 4, jnp.where(lane < 192, 8, 16)))
    wsum = jnp.where(lane < 64, s2[POOL_HIST:], jnp.where(
        lane < 128, s4[POOL_HIST:], jnp.where(lane < 192, s8[POOL_HIST:], s16[POOL_HIST:])))
    pos1 = jnp.where(is_meta, row - (META_PAD - 1), max(POOL_WINDOWS))
    cnt = jnp.clip(pos1, 1, win).astype(F32)
    p_cur = x1[POOL_HIST:]
    m = (wsum / cnt - p_cur).astype(BF16)
    y_pool = jnp.dot(m, poolw_ref[...], preferred_element_type=F32) * pscale_ref[...]
    y_s[:, 0:POOL_WIDTH] = y_pool.astype(BF16)
    pbuf[0:POOL_HIST, :] = x1[ts:ts + POOL_HIST]

    zbuf[CONV_HIST:, :] = proj(OFF_CC, CONV_WIDTH) * proj(OFF_CU, CONV_WIDTH)
    zz = zbuf[...]
    z1 = pltpu.roll(zz, 1, 0)
    z2 = pltpu.roll(zz, 2, 0)
    cw = convw_ref[...]
    yc = (cw[0:1, :] * z2[CONV_HIST:] + cw[1:2, :] * z1[CONV_HIST:] + cw[2:3, :] * zz[CONV_HIST:])
    y_s[:, POOL_WIDTH:POOL_WIDTH + CONV_WIDTH] = (proj(OFF_CB, CONV_WIDTH) * yc).astype(BF16)
    zbuf[0:CONV_HIST, :] = zz[ts:ts + CONV_HIST]

    nc = ts // CHUNK
    v_s[...] = proj(OFF_V, GLA_WIDTH).astype(BF16)
    g_low = proj(OFF_GL, GL_PAD).astype(BF16)
    gx = jnp.dot(g_low, w2_ref[...], preferred_element_type=F32) + b2_ref[...]
    log_sig = jnp.minimum(gx, 0.0) - jnp.log(1.0 + jnp.exp(-jnp.abs(gx)))
    la = log_sig * (LOG2_E / GLA_TAU)
    la_hi = la.astype(BF16)
    la_lo = (la - la_hi.astype(F32)).astype(BF16)
    la_cat = jnp.concatenate([la_hi, la_lo], axis=1)

    ri = lax.broadcasted_iota(jnp.int32, (CHUNK, CHUNK), 0)
    ci = lax.broadcasted_iota(jnp.int32, (CHUNK, CHUNK), 1)
    tril = jnp.where(ri >= ci, 1.0, 0.0).astype(BF16)

    for c in range(nc):
        rows = slice(c * CHUNK, (c + 1) * CHUNK)
        bb = jnp.dot(tril, la_cat[rows, :], preferred_element_type=F32)
        b_s[rows, :] = bb[:, :GLA_KW] + bb[:, GLA_KW:]

    b = b_s[...]
    b3 = b.reshape(nc, CHUNK, GLA_KW)
    b_last = b3[:, CHUNK - 1:CHUNK, :]
    q = proj(OFF_Q, GLA_KW) * (GLA_DK ** -0.5)
    k = proj(OFF_K, GLA_KW)
    qe_s[...] = (q.reshape(b3.shape) * jnp.exp2(b3)).reshape(ts, GLA_KW).astype(BF16)
    kd_s[...] = (k.reshape(b3.shape) * jnp.exp2(b_last - b3)).reshape(ts, GLA_KW).astype(BF16)
    decay = jnp.exp2(b_last)

    lane_hi = lax.broadcasted_iota(jnp.int32, (1, 128), 1) >= GLA_DK
    trow = lax.broadcasted_iota(jnp.int32, (ts, 1), 0) % CHUNK
    blk16, blk4, pos4 = trow // 16, (trow // 4) % 4, trow % 4
    in_hi4 = (trow % 8) >= 4
    neg = -1e30

    def head_rep(x, x_sw, hd):
        col = slice(128 * (hd // 2), 128 * (hd // 2) + 128)
        return jnp.where(lane_hi == (hd % 2 == 1), x[:, col], x_sw[:, col])

    def swap_groups(x):
        return jnp.concatenate([pltpu.roll(x[:, 0:128], GLA_DK, 1),
                                pltpu.roll(x[:, 128:256], GLA_DK, 1)], axis=1)

    b_sw, q_sw, k_sw = swap_groups(b), swap_groups(q), swap_groups(k)

    def rows_of(x, rows_per_group, r):
        x3 = x.reshape(ts // rows_per_group, rows_per_group, 128)
        return jnp.broadcast_to(x3[:, r:r + 1, :], x3.shape).reshape(ts, 128)

    def row_in_4blk(x, j):
        return jnp.where(in_hi4, rows_of(x, 8, 4 + j), rows_of(x, 8, j))

    def build_level_operands(hd):
        rb, rq, rk = head_rep(b, b_sw, hd), head_rep(q, q_sw, hd), head_rep(k, k_sw, hd)
        end16 = rows_of(rb, 16, 15)
        end4 = row_in_4blk(rb, 3)
        k1 = rk * jnp.exp2(end16 - rb)
        k2 = rk * jnp.exp2(end4 - rb)
        for p in range(2):
            col = slice(128 * p, 128 * p + 128)
            j = jnp.where(lane_hi, 2 * p + 1, 2 * p)
            ref1 = jnp.where(lane_hi, rows_of(rb, CHUNK, 16 * (2 * p + 1) + 15),
                             rows_of(rb, CHUNK, 16 * (2 * p) + 15))
            ref2 = jnp.where(lane_hi, rows_of(rb, 16, 4 * (2 * p + 1) + 3),
                             rows_of(rb, 16, 4 * (2 * p) + 3))
            ref3 = jnp.where(lane_hi, row_in_4blk(rb, 2 * p + 1), row_in_4blk(rb, 2 * p))
            q1 = rq * jnp.exp2(jnp.where(blk16 > j, rb - ref1, neg))
            q2 = rq * jnp.exp2(jnp.where(blk4 > j, rb - ref2, neg))
            q3 = rq * jnp.exp2(jnp.where(pos4 >= j, rb - ref3, neg))
            lq_s[3 * hd + 0, :, col] = q1.astype(BF16)
            lq_s[3 * hd + 1, :, col] = q2.astype(BF16)
            lq_s[3 * hd + 2, :, col] = q3.astype(BF16)
            lk_s[3 * hd + 0, :, col] = jnp.where(blk16 == j, k1, 0.0).astype(BF16)
            lk_s[3 * hd + 1, :, col] = jnp.where(blk4 == j, k2, 0.0).astype(BF16)
            lk_s[3 * hd + 2, :, col] = jnp.where(pos4 == j, rk, 0.0).astype(BF16)

    for hd in range(GLA_HEADS):
        build_level_operands(hd)

    same16 = (ri // 16) == (ci // 16)
    same4 = (ri // 4) == (ci // 4)
    nt = (((1,), (1,)), ((), ()))
    tn = (((0,), (0,)), ((), ()))
    ksl = [slice(hd * GLA_DK, (hd + 1) * GLA_DK) for hd in range(GLA_HEADS)]
    vsl = [slice(hd * GLA_DV, (hd + 1) * GLA_DV) for hd in range(GLA_HEADS)]
    pairs = [(c, hd) for c in range(nc) for hd in range(GLA_HEADS)]
    group = 2 * GLA_HEADS
    for g0 in range(0, len(pairs), group):
        grp = pairs[g0:g0 + group]
        rws = [slice(c * CHUNK, (c + 1) * CHUNK) for c, _ in grp]
        lvl = [[lax.dot_general(lq_s[3 * hd + n, r, :], lk_s[3 * hd + n, r, :], nt,
                                preferred_element_type=F32) for n in range(3)]
               for r, (c, hd) in zip(rws, grp)]
        us = [lax.dot_general(v_s[r, vsl[hd]], kd_s[r, ksl[hd]], tn, preferred_element_type=F32)
              for r, (c, hd) in zip(rws, grp)]
        ps = [(g1 + jnp.where(same16, g2, 0.0) + jnp.where(same4, g3, 0.0)).astype(BF16)
              for g1, g2, g3 in lvl]
        pvs = [jnp.dot(p, v_s[r, vsl[hd]], preferred_element_type=F32)
               for p, r, (c, hd) in zip(ps, rws, grp)]
        for pv, u, r, (c, hd) in zip(pvs, us, rws, grp):
            o_s[r, vsl[hd]] = pv
            u_s[c * GLA_HEADS + hd] = u

    sts = [state_s[hd] for hd in range(GLA_HEADS)]
    for c in range(nc):
        for hd in range(GLA_HEADS):
            sb_s[c * GLA_HEADS + hd] = sts[hd].astype(BF16)
        sts = [sts[hd] * decay[c][:, ksl[hd]] + u_s[c * GLA_HEADS + hd]
               for hd in range(GLA_HEADS)]
    for hd in range(GLA_HEADS):
        state_s[hd] = sts[hd]

    for g0 in range(0, len(pairs), group):
        grp = pairs[g0:g0 + group]
        inters = [lax.dot_general(qe_s[c * CHUNK:(c + 1) * CHUNK, ksl[hd]],
                                  sb_s[c * GLA_HEADS + hd], nt, preferred_element_type=F32)
                  for c, hd in grp]
        for inter, (c, hd) in zip(inters, grp):
            i_s[c * CHUNK:(c + 1) * CHUNK, vsl[hd]] = inter

    og = proj(OFF_OG, GLA_WIDTH)
    gate = og * jax.nn.sigmoid(og)
    for hd in range(GLA_HEADS):
        vs = slice(hd * GLA_DV, (hd + 1) * GLA_DV)
        o = o_s[:, vs] + i_s[:, vs]
        o = o * lax.rsqrt(jnp.mean(o * o, axis=-1, keepdims=True) + EPS) * gnorm_ref[:, vs]
        y_s[:, POOL_WIDTH + CONV_WIDTH + hd * GLA_DV:POOL_WIDTH + CONV_WIDTH + (hd + 1) * GLA_DV] = (
            o * gate[:, vs]).astype(BF16)

    mix = jnp.dot(y_s[...], wout_ref[...], preferred_element_type=F32)
    o_ref[...] = h + _rms(mix, post_ref[...])

    @pl.when(is_meta)
    def _():
        pmeta[...] = pbuf[0:POOL_HIST, :]
        zmeta[...] = zbuf[0:CONV_HIST, :]
        smeta[...] = state_s[...]


def _mixer(h, tiles_per_seq, pre, post, win, wout, poolw, pscale, convw, w2, b2, gnorm):
    ts = TILE
    return pl.pallas_call(
        functools.partial(_mixer_kernel, tiles_per_seq=tiles_per_seq),
        grid=(h.shape[0] // ts,),
        in_specs=[
            pl.BlockSpec((ts, D_MODEL), lambda i: (i, 0)),
            _resident((1, D_MODEL)),
            _resident((1, D_MODEL)),
            _resident((D_MODEL, D_PROJ_PACKED)),
            _resident((D_MODEL, D_MODEL)),
            _resident((POOL_WIDTH, POOL_WIDTH)),
            _resident((1, POOL_WIDTH)),
            _resident((8, CONV_WIDTH)),
            _resident((GL_PAD, GLA_KW)),
            _resident((1, GLA_KW)),
            _resident((1, GLA_WIDTH)),
        ],
        out_specs=pl.BlockSpec((ts, D_MODEL), lambda i: (i, 0)),
        out_shape=jax.ShapeDtypeStruct(h.shape, F32),
        scratch_shapes=[
            pltpu.VMEM((ts, D_MODEL), BF16),
            pltpu.VMEM((POOL_HIST + ts, POOL_WIDTH), F32),
            pltpu.VMEM((CONV_HIST + ts, CONV_WIDTH), F32),
            pltpu.VMEM((ts, GLA_KW), F32),
            pltpu.VMEM((ts, GLA_KW), BF16),
            pltpu.VMEM((ts, GLA_KW), BF16),
            pltpu.VMEM((3 * GLA_HEADS, ts, 4 * GLA_DK), BF16),
            pltpu.VMEM((3 * GLA_HEADS, ts, 4 * GLA_DK), BF16),
            pltpu.VMEM((ts, GLA_WIDTH), BF16),
            pltpu.VMEM((ts, GLA_WIDTH), F32),
            pltpu.VMEM((ts, GLA_WIDTH), F32),
            pltpu.VMEM((ts // CHUNK * GLA_HEADS, GLA_DV, GLA_DK), F32),
            pltpu.VMEM((ts // CHUNK * GLA_HEADS, GLA_DV, GLA_DK), BF16),
            pltpu.VMEM((ts, D_MODEL), BF16),
            pltpu.VMEM((GLA_HEADS, GLA_DV, GLA_DK), F32),
            pltpu.VMEM((POOL_HIST, POOL_WIDTH), F32),
            pltpu.VMEM((CONV_HIST, CONV_WIDTH), F32),
            pltpu.VMEM((GLA_HEADS, GLA_DV, GLA_DK), F32),
        ],
        compiler_params=pltpu.CompilerParams(
            dimension_semantics=("arbitrary",), vmem_limit_bytes=VMEM_LIMIT),
        name="mixer",
    )(h, pre, post, win, wout, poolw, pscale, convw, w2, b2, gnorm)


def _pack_w_in(w):
    pool, cb, cc, cu, q, k, v, gl, og = jnp.split(
        w, np.cumsum((256, 256, 256, 256, 256, 256, 512, 16, 512))[:-1].tolist(), axis=-1)
    gl = jnp.pad(gl, ((0, 0), (0, GL_PAD - GLA_GATE_RANK)))
    return jnp.concatenate([pool, cb, cc, cu, q, k, v, og, gl], axis=-1).astype(BF16)


def _block_diag(w):
    out = jnp.zeros((POOL_WIDTH, POOL_WIDTH), w.dtype)
    for g in range(len(POOL_WINDOWS)):
        sl = slice(g * POOL_GROUP, (g + 1) * POOL_GROUP)
        out = out.at[sl, sl].set(w[g])
    return out


def kernel(x, meta, ffn1_pre, ffn1_post, ffn1_wg, ffn1_wu, ffn1_wd, mix_pre, mix_post, w_in,
           pool_w, pool_scale, conv_w, gla_w2, gla_b2, gla_norm, w_out,
           ffn2_pre, ffn2_post, ffn2_wg, ffn2_wu, ffn2_wd):
    n_batch, seq, d = x.shape
    depth = w_in.shape[0]
    assert d == D_MODEL and seq % TILE == 0 and meta.shape[0] == N_META
    meta_tile = jnp.pad(meta.astype(F32), ((META_PAD, 0), (0, 0)))
    h = x.reshape(n_batch * seq, d)
    row = lambda v: v.reshape(1, -1).astype(F32)
    for l in range(depth):
        h = _ffn(h, meta_tile, row(ffn1_pre[l]), row(ffn1_post[l]), ffn1_wg[l].astype(BF16),
                 ffn1_wu[l].astype(BF16), ffn1_wd[l].astype(BF16),
                 mode="first" if l == 0 else "mid")
        h = _mixer(
            h, seq // TILE, row(mix_pre[l]), row(mix_post[l]), _pack_w_in(w_in[l]),
            w_out[l].astype(BF16), _block_diag(pool_w[l]).astype(BF16), row(pool_scale[l]),
            jnp.pad(conv_w[l].astype(F32), ((0, 8 - CONV_K), (0, 0))),
            jnp.pad(gla_w2[l], ((0, GL_PAD - GLA_GATE_RANK), (0, 0))).astype(BF16),
            row(gla_b2[l]), row(gla_norm[l]))
        h = _ffn(h, meta_tile, row(ffn2_pre[l]), row(ffn2_post[l]), ffn2_wg[l].astype(BF16),
                 ffn2_wu[l].astype(BF16), ffn2_wd[l].astype(BF16),
                 mode="last" if l == depth - 1 else "mid")
    return h.reshape(n_batch, seq, d)
```

```python
import functools

import jax
import jax.numpy as jnp
import numpy as np
from jax import lax
from jax.experimental import pallas as pl
from jax.experimental.pallas import tpu as pltpu

F32 = jnp.float32
BF16 = jnp.bfloat16

D_MODEL = 1024
D_FF = 2816
N_META = 16
EPS = 1e-6
FFN_RESID = 0.5

POOL_WIDTH = 256
POOL_WINDOWS = (2, 4, 8, 16)
POOL_GROUP = 64
CONV_WIDTH = 256
CONV_K = 3
GLA_HEADS = 4
GLA_DK = 64
GLA_DV = 128
GLA_KW = GLA_HEADS * GLA_DK
GLA_WIDTH = GLA_HEADS * GLA_DV
GLA_GATE_RANK = 16
GLA_TAU = 16.0
LOG2_E = 1.4426950408889634
CHUNK = 64


OFF_POOL, OFF_CB, OFF_CC, OFF_CU, OFF_Q, OFF_K = 0, 256, 512, 768, 1024, 1280
OFF_V, OFF_OG, OFF_GL = 1536, 2048, 2560
GL_PAD = 128
D_PROJ_PACKED = OFF_GL + GL_PAD

V7X_VMEM_BYTES = 64 * 1024 * 1024
VMEM_LIMIT = 56 * 1024 * 1024

TILE = 512
FF_CHUNK = 256
META_PAD = TILE - N_META
POOL_HIST = 16
CONV_HIST = 8


def _rms(x, g):
    return x * lax.rsqrt(jnp.mean(x * x, axis=-1, keepdims=True) + EPS) * g


def _ffn_kernel(h_ref, meta_ref, pre_ref, post_ref, wg_ref, wu_ref, wd_ref, o_ref, z_ref, *,
                meta_at_step0):
    h = h_ref[...]
    if meta_at_step0:
        h = jnp.where(pl.program_id(0) == 0, meta_ref[...], h)
    a = _rms(h, pre_ref[...]).astype(BF16)
    for j in range(D_FF // FF_CHUNK):
        sl = slice(j * FF_CHUNK, (j + 1) * FF_CHUNK)
        g = jnp.dot(a, wg_ref[:, sl], preferred_element_type=F32)
        u = jnp.dot(a, wu_ref[:, sl], preferred_element_type=F32)
        z_ref[:, sl] = (g * jax.nn.sigmoid(g) * u).astype(BF16)
    f = jnp.dot(z_ref[...], wd_ref[...], preferred_element_type=F32)
    o_ref[...] = h + FFN_RESID * _rms(f, post_ref[...])


def _resident(shape):
    return pl.BlockSpec(shape, lambda *_: (0,) * len(shape), pipeline_mode=pl.Buffered(1))


def _ffn(src, meta_tile, pre, post, wg, wu, wd, *, mode):
    n_tok_tiles = src.shape[0] // TILE - (0 if mode == "first" else 1)
    if mode == "first":
        grid, src_map = n_tok_tiles + 1, lambda i: (jnp.maximum(i - 1, 0), 0)
    elif mode == "mid":
        grid, src_map = n_tok_tiles + 1, lambda i: (i, 0)
    else:
        grid, src_map = n_tok_tiles, lambda i: (i + 1, 0)
    out_rows = grid * TILE
    return pl.pallas_call(
        functools.partial(_ffn_kernel, meta_at_step0=(mode == "first")),
        grid=(grid,),
        in_specs=[
            pl.BlockSpec((TILE, D_MODEL), src_map),
            _resident((TILE, D_MODEL)),
            _resident((1, D_MODEL)),
            _resident((1, D_MODEL)),
            _resident((D_MODEL, D_FF)),
            _resident((D_MODEL, D_FF)),
            _resident((D_FF, D_MODEL)),
        ],
        out_specs=pl.BlockSpec((TILE, D_MODEL), lambda i: (i, 0)),
        out_shape=jax.ShapeDtypeStruct((out_rows, D_MODEL), F32),
        scratch_shapes=[pltpu.VMEM((TILE, D_FF), BF16)],
        compiler_params=pltpu.CompilerParams(
            dimension_semantics=("arbitrary",), vmem_limit_bytes=VMEM_LIMIT),
        name="ffn_" + mode,
    )(src, meta_tile, pre, post, wg, wu, wd)


def _mixer_kernel(h_ref, pre_ref, post_ref, win_ref, wout_ref, poolw_ref, pscale_ref,
                  convw_ref, w2_ref, b2_ref, gnorm_ref, o_ref,
                  a_s, pbuf, zbuf, b_s, q_s, k_s, rep_s, qe_s, kd_s, lq_s, lk_s, v_s, gate_s, o_s, i_s, u_s, sb_s, y_s, state_s,
                  pmeta, zmeta, smeta, *, tiles_per_seq):
    step = pl.program_id(0)
    ts = TILE
    is_meta = step == 0

    @pl.when(is_meta)
    def _():
        pbuf[0:POOL_HIST, :] = jnp.zeros((POOL_HIST, POOL_WIDTH), F32)
        zbuf[0:CONV_HIST, :] = jnp.zeros((CONV_HIST, CONV_WIDTH), F32)
        state_s[...] = jnp.zeros_like(state_s)

    @pl.when(jnp.logical_and(step > 0, lax.rem(step - 1, tiles_per_seq) == 0))
    def _():
        pbuf[0:POOL_HIST, :] = pmeta[...]
        zbuf[0:CONV_HIST, :] = zmeta[...]
        state_s[...] = smeta[...]

    nc = ts // CHUNK
    ri = lax.broadcasted_iota(jnp.int32, (CHUNK, CHUNK), 0)
    ci = lax.broadcasted_iota(jnp.int32, (CHUNK, CHUNK), 1)

    def proj(off, width):
        return jnp.dot(a_s[...], win_ref[:, off:off + width], preferred_element_type=F32)

    def pool_mixer():
        x1 = pbuf[...]
        s2 = x1 + pltpu.roll(x1, 1, 0)
        s4 = s2 + pltpu.roll(s2, 2, 0)
        s8 = s4 + pltpu.roll(s4, 4, 0)
        s16 = s8 + pltpu.roll(s8, 8, 0)
        lane = lax.broadcasted_iota(jnp.int32, (ts, POOL_WIDTH), 1)
        row = lax.broadcasted_iota(jnp.int32, (ts, POOL_WIDTH), 0)
        win = jnp.where(lane < 64, 2, jnp.where(lane < 128, 4, jnp.where(lane < 192, 8, 16)))
        wsum = jnp.where(lane < 64, s2[POOL_HIST:], jnp.where(
            lane < 128, s4[POOL_HIST:], jnp.where(lane < 192, s8[POOL_HIST:], s16[POOL_HIST:])))
        pos1 = jnp.where(is_meta, row - (META_PAD - 1), max(POOL_WINDOWS))
        cnt = jnp.clip(pos1, 1, win).astype(F32)
        m = (wsum / cnt - x1[POOL_HIST:]).astype(BF16)
        y_pool = jnp.dot(m, poolw_ref[...], preferred_element_type=F32) * pscale_ref[...]
        y_s[:, 0:POOL_WIDTH] = y_pool.astype(BF16)
        pbuf[0:POOL_HIST, :] = x1[ts:ts + POOL_HIST]

    def conv_mixer():
        zz = zbuf[...]
        z1 = pltpu.roll(zz, 1, 0)
        z2 = pltpu.roll(zz, 2, 0)
        cw = convw_ref[...]
        yc = (cw[0:1, :] * z2[CONV_HIST:] + cw[1:2, :] * z1[CONV_HIST:]
              + cw[2:3, :] * zz[CONV_HIST:])
        y_s[:, POOL_WIDTH:POOL_WIDTH + CONV_WIDTH] = (proj(OFF_CB, CONV_WIDTH) * yc).astype(BF16)
        zbuf[0:CONV_HIST, :] = zz[ts:ts + CONV_HIST]

    @pl.when(step >= 0)
    def _():
        a_s[...] = _rms(h_ref[...], pre_ref[...]).astype(BF16)
        g_low = proj(OFF_GL, GL_PAD).astype(BF16)
        gx = jnp.dot(g_low, w2_ref[...], preferred_element_type=F32) + b2_ref[...]
        log_sig = jnp.minimum(gx, 0.0) - jnp.log(1.0 + jnp.exp(-jnp.abs(gx)))
        la = log_sig * (LOG2_E / GLA_TAU)
        la_hi = la.astype(BF16)
        la_lo = (la - la_hi.astype(F32)).astype(BF16)
        la_cat = jnp.concatenate([la_hi, la_lo], axis=1)
        pbuf[POOL_HIST:, :] = proj(OFF_POOL, POOL_WIDTH)
        zbuf[CONV_HIST:, :] = proj(OFF_CC, CONV_WIDTH) * proj(OFF_CU, CONV_WIDTH)
        tril = jnp.where(ri >= ci, 1.0, 0.0).astype(BF16)
        for c in range(nc):
            rows = slice(c * CHUNK, (c + 1) * CHUNK)
            bb = jnp.dot(tril, la_cat[rows, :], preferred_element_type=F32)
            b_s[rows, :] = bb[:, :GLA_KW] + bb[:, GLA_KW:]
        q_s[...] = proj(OFF_Q, GLA_KW) * (GLA_DK ** -0.5)
        k_s[...] = proj(OFF_K, GLA_KW)
        pool_mixer()
        conv_mixer()
        lane_hi2 = lax.broadcasted_iota(jnp.int32, (1, 128), 1) >= GLA_DK
        for n, src in enumerate((b_s, q_s, k_s)):
            for c2 in range(2):
                x = src[:, 128 * c2:128 * c2 + 128]
                x_sw = pltpu.roll(x, GLA_DK, 1)
                rep_s[GLA_HEADS * n + 2 * c2] = jnp.where(lane_hi2, x_sw, x)
                rep_s[GLA_HEADS * n + 2 * c2 + 1] = jnp.where(lane_hi2, x, x_sw)

    b, q, k = b_s[...], q_s[...], k_s[...]
    b3 = b.reshape(nc, CHUNK, GLA_KW)
    b_last = b3[:, CHUNK - 1:CHUNK, :]
    qe_s[...] = (q.reshape(b3.shape) * jnp.exp2(b3)).reshape(ts, GLA_KW).astype(BF16)
    kd_s[...] = (k.reshape(b3.shape) * jnp.exp2(b_last - b3)).reshape(ts, GLA_KW).astype(BF16)
    decay = jnp.exp2(b_last)

    lane_hi = lax.broadcasted_iota(jnp.int32, (1, 1, 128), 2) >= GLA_DK
    t64 = lax.broadcasted_iota(jnp.int32, (1, CHUNK, 1), 1)
    blk16, blk4, pos4 = t64 // 16, (t64 // 4) % 4, t64 % 4
    in_hi4 = lax.broadcasted_iota(jnp.int32, (1, 8, 1), 1) >= 4
    neg = -1e30

    def build_level_operands(hd):
        rb, rq, rk = (rep_s[GLA_HEADS * n + hd] for n in range(3))
        rb64, rq64, rk64 = (x.reshape(nc, CHUNK, 128) for x in (rb, rq, rk))
        rb16 = rb.reshape(ts // 16, 16, 128)
        rb8 = rb.reshape(ts // 8, 8, 128)
        to64 = lambda x: x.reshape(nc, CHUNK, 128)
        end4 = jnp.where(in_hi4, rb8[:, 7:8, :], rb8[:, 3:4, :])
        k1 = rk64 * to64(jnp.exp2(rb16[:, 15:16, :] - rb16))
        k2 = rk64 * to64(jnp.exp2(end4 - rb8))
        nxt = jnp.where(lane_hi, pltpu.roll(rb8, 7, 1), rb8)
        for p in range(2):
            col = slice(128 * p, 128 * p + 128)
            j = jnp.where(lane_hi, 2 * p + 1, 2 * p)
            ref1 = jnp.where(lane_hi, rb64[:, 32 * p + 31:32 * p + 32, :],
                             rb64[:, 32 * p + 15:32 * p + 16, :])
            ref2 = jnp.where(lane_hi, rb16[:, 8 * p + 7:8 * p + 8, :], rb16[:, 8 * p + 3:8 * p + 4, :])
            ref3 = jnp.where(in_hi4, nxt[:, 4 + 2 * p:5 + 2 * p, :], nxt[:, 2 * p:2 * p + 1, :])
            q1 = rq64 * jnp.exp2(jnp.where(blk16 > j, rb64 - ref1, neg))
            q2 = rq64 * jnp.exp2(jnp.where(blk4 > j, to64(rb16 - ref2), neg))
            q3 = rq64 * jnp.exp2(jnp.where(pos4 >= j, to64(rb8 - ref3), neg))
            for n, (ql, kl) in enumerate(((q1, jnp.where(blk16 == j, k1, 0.0)),
                                          (q2, jnp.where(blk4 == j, k2, 0.0)),
                                          (q3, jnp.where(pos4 == j, rk64, 0.0)))):
                lq_s[3 * hd + n, :, col] = ql.reshape(ts, 128).astype(BF16)
                lk_s[3 * hd + n, :, col] = kl.reshape(ts, 128).astype(BF16)

    def proj_rows(r0, nr, off, width):
        return jnp.dot(a_s[r0:r0 + nr, :], win_ref[:, off:off + width],
                       preferred_element_type=F32)

    quarter = ts // 4
    build_level_operands(0)
    for r0 in range(0, ts, quarter):
        v_s[r0:r0 + quarter, :] = proj_rows(r0, quarter, OFF_V, GLA_WIDTH).astype(BF16)
    build_level_operands(1)
    for r0 in range(0, ts, quarter):
        og = proj_rows(r0, quarter, OFF_OG, GLA_WIDTH)
        gate_s[r0:r0 + quarter, :] = og * jax.nn.sigmoid(og)
    build_level_operands(2)
    build_level_operands(3)

    same16 = (ri // 16) == (ci // 16)
    same4 = (ri // 4) == (ci // 4)
    nt = (((1,), (1,)), ((), ()))
    tn = (((0,), (0,)), ((), ()))
    ksl = [slice(hd * GLA_DK, (hd + 1) * GLA_DK) for hd in range(GLA_HEADS)]
    vsl = [slice(hd * GLA_DV, (hd + 1) * GLA_DV) for hd in range(GLA_HEADS)]
    pairs = [(c, hd) for c in range(nc) for hd in range(GLA_HEADS)]
    group = 2 * GLA_HEADS

    def score_group(g0):
        grp = pairs[g0:g0 + group]
        rws = [slice(c * CHUNK, (c + 1) * CHUNK) for c, _ in grp]
        lvl = [[lax.dot_general(lq_s[3 * hd + n, r, :], lk_s[3 * hd + n, r, :], nt,
                                preferred_element_type=F32) for n in range(3)]
               for r, (c, hd) in zip(rws, grp)]
        us = [lax.dot_general(v_s[r, vsl[hd]], kd_s[r, ksl[hd]], tn, preferred_element_type=F32)
              for r, (c, hd) in zip(rws, grp)]
        ps = [(g1 + jnp.where(same16, g2, 0.0) + jnp.where(same4, g3, 0.0)).astype(BF16)
              for g1, g2, g3 in lvl]
        pvs = [jnp.dot(p, v_s[r, vsl[hd]], preferred_element_type=F32)
               for p, r, (c, hd) in zip(ps, rws, grp)]
        for pv, u, r, (c, hd) in zip(pvs, us, rws, grp):
            o_s[r, vsl[hd]] = pv
            u_s[c * GLA_HEADS + hd] = u

    for g0 in range(0, len(pairs), group):
        score_group(g0)

    sts = [state_s[hd] for hd in range(GLA_HEADS)]
    for c in range(nc):
        for hd in range(GLA_HEADS):
            sb_s[c * GLA_HEADS + hd] = sts[hd].astype(BF16)
        sts = [sts[hd] * decay[c][:, ksl[hd]] + u_s[c * GLA_HEADS + hd]
               for hd in range(GLA_HEADS)]
    for hd in range(GLA_HEADS):
        state_s[hd] = sts[hd]

    for g0 in range(0, len(pairs), group):
        grp = pairs[g0:g0 + group]
        inters = [lax.dot_general(qe_s[c * CHUNK:(c + 1) * CHUNK, ksl[hd]],
                                  sb_s[c * GLA_HEADS + hd], nt, preferred_element_type=F32)
                  for c, hd in grp]
        for inter, (c, hd) in zip(inters, grp):
            i_s[c * CHUNK:(c + 1) * CHUNK, vsl[hd]] = inter

    for hd in range(GLA_HEADS):
        vs = slice(hd * GLA_DV, (hd + 1) * GLA_DV)
        o = o_s[:, vs] + i_s[:, vs]
        o = o * lax.rsqrt(jnp.mean(o * o, axis=-1, keepdims=True) + EPS) * gnorm_ref[:, vs]
        y_s[:, POOL_WIDTH + CONV_WIDTH + hd * GLA_DV:POOL_WIDTH + CONV_WIDTH + (hd + 1) * GLA_DV] = (
            o * gate_s[:, vs]).astype(BF16)

    mix = jnp.dot(y_s[...], wout_ref[...], preferred_element_type=F32)
    o_ref[...] = h_ref[...] + _rms(mix, post_ref[...])

    @pl.when(is_meta)
    def _():
        pmeta[...] = pbuf[0:POOL_HIST, :]
        zmeta[...] = zbuf[0:CONV_HIST, :]
        smeta[...] = state_s[...]


def _mixer(h, tiles_per_seq, pre, post, win, wout, poolw, pscale, convw, w2, b2, gnorm):
    ts = TILE
    return pl.pallas_call(
        functools.partial(_mixer_kernel, tiles_per_seq=tiles_per_seq),
        grid=(h.shape[0] // ts,),
        in_specs=[
            pl.BlockSpec((ts, D_MODEL), lambda i: (i, 0)),
            _resident((1, D_MODEL)),
            _resident((1, D_MODEL)),
            _resident((D_MODEL, D_PROJ_PACKED)),
            _resident((D_MODEL, D_MODEL)),
            _resident((POOL_WIDTH, POOL_WIDTH)),
            _resident((1, POOL_WIDTH)),
            _resident((8, CONV_WIDTH)),
            _resident((GL_PAD, GLA_KW)),
            _resident((1, GLA_KW)),
            _resident((1, GLA_WIDTH)),
        ],
        out_specs=pl.BlockSpec((ts, D_MODEL), lambda i: (i, 0)),
        out_shape=jax.ShapeDtypeStruct(h.shape, F32),
        scratch_shapes=[
            pltpu.VMEM((ts, D_MODEL), BF16),
            pltpu.VMEM((POOL_HIST + ts, POOL_WIDTH), F32),
            pltpu.VMEM((CONV_HIST + ts, CONV_WIDTH), F32),
            pltpu.VMEM((ts, GLA_KW), F32),
            pltpu.VMEM((ts, GLA_KW), F32),
            pltpu.VMEM((ts, GLA_KW), F32),
            pltpu.VMEM((3 * GLA_HEADS, ts, 128), F32),
            pltpu.VMEM((ts, GLA_KW), BF16),
            pltpu.VMEM((ts, GLA_KW), BF16),
            pltpu.VMEM((3 * GLA_HEADS, ts, 4 * GLA_DK), BF16),
            pltpu.VMEM((3 * GLA_HEADS, ts, 4 * GLA_DK), BF16),
            pltpu.VMEM((ts, GLA_WIDTH), BF16),
            pltpu.VMEM((ts, GLA_WIDTH), F32),
            pltpu.VMEM((ts, GLA_WIDTH), F32),
            pltpu.VMEM((ts, GLA_WIDTH), F32),
            pltpu.VMEM((ts // CHUNK * GLA_HEADS, GLA_DV, GLA_DK), F32),
            pltpu.VMEM((ts // CHUNK * GLA_HEADS, GLA_DV, GLA_DK), BF16),
            pltpu.VMEM((ts, D_MODEL), BF16),
            pltpu.VMEM((GLA_HEADS, GLA_DV, GLA_DK), F32),
            pltpu.VMEM((POOL_HIST, POOL_WIDTH), F32),
            pltpu.VMEM((CONV_HIST, CONV_WIDTH), F32),
            pltpu.VMEM((GLA_HEADS, GLA_DV, GLA_DK), F32),
        ],
        compiler_params=pltpu.CompilerParams(
            dimension_semantics=("arbitrary",), vmem_limit_bytes=VMEM_LIMIT),
        name="mixer",
    )(h, pre, post, win, wout, poolw, pscale, convw, w2, b2, gnorm)


def _pack_w_in(w):
    pool, cb, cc, cu, q, k, v, gl, og = jnp.split(
        w, np.cumsum((256, 256, 256, 256, 256, 256, 512, 16, 512))[:-1].tolist(), axis=-1)
    gl = jnp.pad(gl, ((0, 0), (0, GL_PAD - GLA_GATE_RANK)))
    return jnp.concatenate([pool, cb, cc, cu, q, k, v, og, gl], axis=-1).astype(BF16)


def _block_diag(w):
    out = jnp.zeros((POOL_WIDTH, POOL_WIDTH), w.dtype)
    for g in range(len(POOL_WINDOWS)):
        sl = slice(g * POOL_GROUP, (g + 1) * POOL_GROUP)
        out = out.at[sl, sl].set(w[g])
    return out


def kernel(x, meta, ffn1_pre, ffn1_post, ffn1_wg, ffn1_wu, ffn1_wd, mix_pre, mix_post, w_in,
           pool_w, pool_scale, conv_w, gla_w2, gla_b2, gla_norm, w_out,
           ffn2_pre, ffn2_post, ffn2_wg, ffn2_wu, ffn2_wd):
    n_batch, seq, d = x.shape
    depth = w_in.shape[0]
    assert d == D_MODEL and seq % TILE == 0 and meta.shape[0] == N_META
    meta_tile = jnp.pad(meta.astype(F32), ((META_PAD, 0), (0, 0)))
    h = x.reshape(n_batch * seq, d)
    row = lambda v: v.reshape(1, -1).astype(F32)
    for l in range(depth):
        h = _ffn(h, meta_tile, row(ffn1_pre[l]), row(ffn1_post[l]), ffn1_wg[l].astype(BF16),
                 ffn1_wu[l].astype(BF16), ffn1_wd[l].astype(BF16),
                 mode="first" if l == 0 else "mid")
        h = _mixer(
            h, seq // TILE, row(mix_pre[l]), row(mix_post[l]), _pack_w_in(w_in[l]),
            w_out[l].astype(BF16), _block_diag(pool_w[l]).astype(BF16), row(pool_scale[l]),
            jnp.pad(conv_w[l].astype(F32), ((0, 8 - CONV_K), (0, 0))),
            jnp.pad(gla_w2[l], ((0, GL_PAD - GLA_GATE_RANK), (0, 0))).astype(BF16),
            row(gla_b2[l]), row(gla_norm[l]))
        h = _ffn(h, meta_tile, row(ffn2_pre[l]), row(ffn2_post[l]), ffn2_wg[l].astype(BF16),
                 ffn2_wu[l].astype(BF16), ffn2_wd[l].astype(BF16),
                 mode="last" if l == depth - 1 else "mid")
    return h.reshape(n_batch, seq, d)
```

```python
import functools

import jax
import jax.numpy as jnp
import numpy as np
from jax import lax
from jax.experimental import pallas as pl
from jax.experimental.pallas import tpu as pltpu

F32 = jnp.float32
BF16 = jnp.bfloat16

D_MODEL = 1024
D_FF = 2816
N_META = 16
EPS = 1e-6
FFN_RESID = 0.5

POOL_WIDTH = 256
POOL_WINDOWS = (2, 4, 8, 16)
POOL_GROUP = 64
CONV_WIDTH = 256
CONV_K = 3
GLA_HEADS = 4
GLA_DK = 64
GLA_DV = 128
GLA_KW = GLA_HEADS * GLA_DK
GLA_WIDTH = GLA_HEADS * GLA_DV
GLA_GATE_RANK = 16
GLA_TAU = 16.0
LOG2_E = 1.4426950408889634
CHUNK = 64


OFF_POOL, OFF_CB, OFF_CC, OFF_CU, OFF_Q, OFF_K = 0, 256, 512, 768, 1024, 1280
OFF_V, OFF_OG, OFF_GL = 1536, 2048, 2560
GL_PAD = 128
D_PROJ_PACKED = OFF_GL + GL_PAD

V7X_VMEM_BYTES = 64 * 1024 * 1024
VMEM_LIMIT = 56 * 1024 * 1024

TILE = 512
FF_CHUNK = 256
META_PAD = TILE - N_META
POOL_HIST = 16
CONV_HIST = 8


def _rms(x, g):
    return x * lax.rsqrt(jnp.mean(x * x, axis=-1, keepdims=True) + EPS) * g


def _load_ffn_weights(layer, wg_hbm, wu_hbm, wd_hbm, wg_ref, wu_ref, wd_ref, stage_cols,
                      stage_rows, sems):
    n_chunks = D_FF // FF_CHUNK

    def copies(j, slot):
        off = pl.multiple_of(j * FF_CHUNK, FF_CHUNK)
        return (
            pltpu.make_async_copy(wg_hbm.at[layer, :, pl.ds(off, FF_CHUNK)],
                                  stage_cols.at[0, slot], sems.at[0, slot]),
            pltpu.make_async_copy(wu_hbm.at[layer, :, pl.ds(off, FF_CHUNK)],
                                  stage_cols.at[1, slot], sems.at[1, slot]),
            pltpu.make_async_copy(wd_hbm.at[layer, pl.ds(off, FF_CHUNK), :],
                                  stage_rows.at[slot], sems.at[2, slot]),
        )

    for cp in copies(0, 0):
        cp.start()

    def body(j, carry):
        slot = lax.rem(j, 2)

        @pl.when(j + 1 < n_chunks)
        def _():
            for cp in copies(j + 1, 1 - slot):
                cp.start()

        for cp in copies(j, slot):
            cp.wait()
        off = pl.multiple_of(j * FF_CHUNK, FF_CHUNK)
        wg_ref[:, pl.ds(off, FF_CHUNK)] = stage_cols[0, slot].astype(BF16)
        wu_ref[:, pl.ds(off, FF_CHUNK)] = stage_cols[1, slot].astype(BF16)
        wd_ref[pl.ds(off, FF_CHUNK), :] = stage_rows[slot].astype(BF16)
        return carry

    lax.fori_loop(0, n_chunks, body, 0)


def _ffn_kernel(h_ref, meta_ref, pre_ref, post_ref, wg_hbm, wu_hbm, wd_hbm, o_ref,
                z_ref, wg_ref, wu_ref, wd_ref, stage_cols, stage_rows, sems, *,
                meta_at_step0, layer):
    @pl.when(pl.program_id(0) == 0)
    def _():
        _load_ffn_weights(layer, wg_hbm, wu_hbm, wd_hbm, wg_ref, wu_ref, wd_ref,
                          stage_cols, stage_rows, sems)

    h = h_ref[...]
    if meta_at_step0:
        h = jnp.where(pl.program_id(0) == 0, meta_ref[...], h)
    a = _rms(h, pre_ref[...]).astype(BF16)
    for j in range(D_FF // FF_CHUNK):
        sl = slice(j * FF_CHUNK, (j + 1) * FF_CHUNK)
        g = jnp.dot(a, wg_ref[:, sl], preferred_element_type=F32)
        u = jnp.dot(a, wu_ref[:, sl], preferred_element_type=F32)
        z_ref[:, sl] = (g * jax.nn.sigmoid(g) * u).astype(BF16)
    f = jnp.dot(z_ref[...], wd_ref[...], preferred_element_type=F32)
    o_ref[...] = h + FFN_RESID * _rms(f, post_ref[...])


def _resident(shape):
    return pl.BlockSpec(shape, lambda *_: (0,) * len(shape), pipeline_mode=pl.Buffered(1))


def _ffn(src, meta_tile, pre, post, wg, wu, wd, *, mode, layer):
    n_tok_tiles = src.shape[0] // TILE - (0 if mode == "first" else 1)
    if mode == "first":
        grid, src_map = n_tok_tiles + 1, lambda i: (jnp.maximum(i - 1, 0), 0)
    elif mode == "mid":
        grid, src_map = n_tok_tiles + 1, lambda i: (i, 0)
    else:
        grid, src_map = n_tok_tiles, lambda i: (i + 1, 0)
    out_rows = grid * TILE
    return pl.pallas_call(
        functools.partial(_ffn_kernel, meta_at_step0=(mode == "first"), layer=layer),
        grid=(grid,),
        in_specs=[
            pl.BlockSpec((TILE, D_MODEL), src_map),
            _resident((TILE, D_MODEL)),
            _resident((1, D_MODEL)),
            _resident((1, D_MODEL)),
            pl.BlockSpec(memory_space=pl.ANY),
            pl.BlockSpec(memory_space=pl.ANY),
            pl.BlockSpec(memory_space=pl.ANY),
        ],
        out_specs=pl.BlockSpec((TILE, D_MODEL), lambda i: (i, 0)),
        out_shape=jax.ShapeDtypeStruct((out_rows, D_MODEL), F32),
        scratch_shapes=[
            pltpu.VMEM((TILE, D_FF), BF16),
            pltpu.VMEM((D_MODEL, D_FF), BF16),
            pltpu.VMEM((D_MODEL, D_FF), BF16),
            pltpu.VMEM((D_FF, D_MODEL), BF16),
            pltpu.VMEM((2, 2, D_MODEL, FF_CHUNK), F32),
            pltpu.VMEM((2, FF_CHUNK, D_MODEL), F32),
            pltpu.SemaphoreType.DMA((3, 2)),
        ],
        compiler_params=pltpu.CompilerParams(
            dimension_semantics=("arbitrary",), vmem_limit_bytes=VMEM_LIMIT),
        name="ffn_" + mode,
    )(src, meta_tile, pre, post, wg, wu, wd)


def _mixer_kernel(h_ref, pre_ref, post_ref, win_ref, wout_ref, poolw_ref, pscale_ref,
                  convw_ref, w2_ref, b2_ref, gnorm_ref, o_ref,
                  a_s, pbuf, zbuf, b_s, q_s, k_s, rep_s, qe_s, kd_s, lq_s, lk_s, v_s, gate_s, o_s, i_s, u_s, sb_s, y_s, state_s,
                  pmeta, zmeta, smeta, *, tiles_per_seq):
    step = pl.program_id(0)
    ts = TILE
    is_meta = step == 0

    @pl.when(is_meta)
    def _():
        pbuf[0:POOL_HIST, :] = jnp.zeros((POOL_HIST, POOL_WIDTH), F32)
        zbuf[0:CONV_HIST, :] = jnp.zeros((CONV_HIST, CONV_WIDTH), F32)
        state_s[...] = jnp.zeros_like(state_s)

    @pl.when(jnp.logical_and(step > 0, lax.rem(step - 1, tiles_per_seq) == 0))
    def _():
        pbuf[0:POOL_HIST, :] = pmeta[...]
        zbuf[0:CONV_HIST, :] = zmeta[...]
        state_s[...] = smeta[...]

    nc = ts // CHUNK
    ri = lax.broadcasted_iota(jnp.int32, (CHUNK, CHUNK), 0)
    ci = lax.broadcasted_iota(jnp.int32, (CHUNK, CHUNK), 1)

    def proj(off, width):
        return jnp.dot(a_s[...], win_ref[:, off:off + width], preferred_element_type=F32)

    def pool_mixer():
        x1 = pbuf[...]
        s2 = x1 + pltpu.roll(x1, 1, 0)
        s4 = s2 + pltpu.roll(s2, 2, 0)
        s8 = s4 + pltpu.roll(s4, 4, 0)
        s16 = s8 + pltpu.roll(s8, 8, 0)
        lane = lax.broadcasted_iota(jnp.int32, (ts, POOL_WIDTH), 1)
        row = lax.broadcasted_iota(jnp.int32, (ts, POOL_WIDTH), 0)
        win = jnp.where(lane < 64, 2, jnp.where(lane < 128, 4, jnp.where(lane < 192, 8, 16)))
        wsum = jnp.where(lane < 64, s2[POOL_HIST:], jnp.where(
            lane < 128, s4[POOL_HIST:], jnp.where(lane < 192, s8[POOL_HIST:], s16[POOL_HIST:])))
        pos1 = jnp.where(is_meta, row - (META_PAD - 1), max(POOL_WINDOWS))
        cnt = jnp.clip(pos1, 1, win).astype(F32)
        m = (wsum / cnt - x1[POOL_HIST:]).astype(BF16)
        y_pool = jnp.dot(m, poolw_ref[...], preferred_element_type=F32) * pscale_ref[...]
        y_s[:, 0:POOL_WIDTH] = y_pool.astype(BF16)
        pbuf[0:POOL_HIST, :] = x1[ts:ts + POOL_HIST]

    def conv_mixer():
        zz = zbuf[...]
        z1 = pltpu.roll(zz, 1, 0)
        z2 = pltpu.roll(zz, 2, 0)
        cw = convw_ref[...]
        yc = (cw[0:1, :] * z2[CONV_HIST:] + cw[1:2, :] * z1[CONV_HIST:]
              + cw[2:3, :] * zz[CONV_HIST:])
        y_s[:, POOL_WIDTH:POOL_WIDTH + CONV_WIDTH] = (proj(OFF_CB, CONV_WIDTH) * yc).astype(BF16)
        zbuf[0:CONV_HIST, :] = zz[ts:ts + CONV_HIST]

    @pl.when(step >= 0)
    def _():
        a_s[...] = _rms(h_ref[...], pre_ref[...]).astype(BF16)
        g_low = proj(OFF_GL, GL_PAD).astype(BF16)
        gx = jnp.dot(g_low, w2_ref[...], preferred_element_type=F32) + b2_ref[...]
        log_sig = jnp.minimum(gx, 0.0) - jnp.log(1.0 + jnp.exp(-jnp.abs(gx)))
        la = log_sig * (LOG2_E / GLA_TAU)
        la_hi = la.astype(BF16)
        la_lo = (la - la_hi.astype(F32)).astype(BF16)
        la_cat = jnp.concatenate([la_hi, la_lo], axis=1)
        pbuf[POOL_HIST:, :] = proj(OFF_POOL, POOL_WIDTH)
        zbuf[CONV_HIST:, :] = proj(OFF_CC, CONV_WIDTH) * proj(OFF_CU, CONV_WIDTH)
        tril = jnp.where(ri >= ci, 1.0, 0.0).astype(BF16)
        for c in range(nc):
            rows = slice(c * CHUNK, (c + 1) * CHUNK)
            bb = jnp.dot(tril, la_cat[rows, :], preferred_element_type=F32)
            b_s[rows, :] = bb[:, :GLA_KW] + bb[:, GLA_KW:]
        q_s[...] = proj(OFF_Q, GLA_KW) * (GLA_DK ** -0.5)
        k_s[...] = proj(OFF_K, GLA_KW)
        pool_mixer()
        conv_mixer()
        lane_hi2 = lax.broadcasted_iota(jnp.int32, (1, 128), 1) >= GLA_DK
        for n, src in enumerate((b_s, q_s, k_s)):
            for c2 in range(2):
                x = src[:, 128 * c2:128 * c2 + 128]
                x_sw = pltpu.roll(x, GLA_DK, 1)
                rep_s[GLA_HEADS * n + 2 * c2] = jnp.where(lane_hi2, x_sw, x)
                rep_s[GLA_HEADS * n + 2 * c2 + 1] = jnp.where(lane_hi2, x, x_sw)

    b, q, k = b_s[...], q_s[...], k_s[...]
    b3 = b.reshape(nc, CHUNK, GLA_KW)
    b_last = b3[:, CHUNK - 1:CHUNK, :]
    qe_s[...] = (q.reshape(b3.shape) * jnp.exp2(b3)).reshape(ts, GLA_KW).astype(BF16)
    kd_s[...] = (k.reshape(b3.shape) * jnp.exp2(b_last - b3)).reshape(ts, GLA_KW).astype(BF16)
    decay = jnp.exp2(b_last)

    lane_hi = lax.broadcasted_iota(jnp.int32, (1, 1, 128), 2) >= GLA_DK
    t64 = lax.broadcasted_iota(jnp.int32, (1, CHUNK, 1), 1)
    blk16, blk4, pos4 = t64 // 16, (t64 // 4) % 4, t64 % 4
    in_hi4 = lax.broadcasted_iota(jnp.int32, (1, 8, 1), 1) >= 4
    neg = -1e30

    def build_level_operands(hd):
        rb, rq, rk = (rep_s[GLA_HEADS * n + hd] for n in range(3))
        rb64, rq64, rk64 = (x.reshape(nc, CHUNK, 128) for x in (rb, rq, rk))
        rb16 = rb.reshape(ts // 16, 16, 128)
        rb8 = rb.reshape(ts // 8, 8, 128)
        to64 = lambda x: x.reshape(nc, CHUNK, 128)
        end4 = jnp.where(in_hi4, rb8[:, 7:8, :], rb8[:, 3:4, :])
        k1 = rk64 * to64(jnp.exp2(rb16[:, 15:16, :] - rb16))
        k2 = rk64 * to64(jnp.exp2(end4 - rb8))
        nxt = jnp.where(lane_hi, pltpu.roll(rb8, 7, 1), rb8)
        for p in range(2):
            col = slice(128 * p, 128 * p + 128)
            j = jnp.where(lane_hi, 2 * p + 1, 2 * p)
            ref1 = jnp.where(lane_hi, rb64[:, 32 * p + 31:32 * p + 32, :],
                             rb64[:, 32 * p + 15:32 * p + 16, :])
            ref2 = jnp.where(lane_hi, rb16[:, 8 * p + 7:8 * p + 8, :], rb16[:, 8 * p + 3:8 * p + 4, :])
            ref3 = jnp.where(in_hi4, nxt[:, 4 + 2 * p:5 + 2 * p, :], nxt[:, 2 * p:2 * p + 1, :])
            q1 = rq64 * jnp.exp2(jnp.where(blk16 > j, rb64 - ref1, neg))
            q2 = rq64 * jnp.exp2(jnp.where(blk4 > j, to64(rb16 - ref2), neg))
            q3 = rq64 * jnp.exp2(jnp.where(pos4 >= j, to64(rb8 - ref3), neg))
            for n, (ql, kl) in enumerate(((q1, jnp.where(blk16 == j, k1, 0.0)),
                                          (q2, jnp.where(blk4 == j, k2, 0.0)),
                                          (q3, jnp.where(pos4 == j, rk64, 0.0)))):
                lq_s[3 * hd + n, :, col] = ql.reshape(ts, 128).astype(BF16)
                lk_s[3 * hd + n, :, col] = kl.reshape(ts, 128).astype(BF16)

    def proj_rows(r0, nr, off, width):
        return jnp.dot(a_s[r0:r0 + nr, :], win_ref[:, off:off + width],
                       preferred_element_type=F32)

    quarter = ts // 4
    build_level_operands(0)
    for r0 in range(0, ts, quarter):
        v_s[r0:r0 + quarter, :] = proj_rows(r0, quarter, OFF_V, GLA_WIDTH).astype(BF16)
    build_level_operands(1)
    for r0 in range(0, ts, quarter):
        og = proj_rows(r0, quarter, OFF_OG, GLA_WIDTH)
        gate_s[r0:r0 + quarter, :] = og * jax.nn.sigmoid(og)
    build_level_operands(2)
    build_level_operands(3)

    same16 = (ri // 16) == (ci // 16)
    same4 = (ri // 4) == (ci // 4)
    nt = (((1,), (1,)), ((), ()))
    tn = (((0,), (0,)), ((), ()))
    ksl = [slice(hd * GLA_DK, (hd + 1) * GLA_DK) for hd in range(GLA_HEADS)]
    vsl = [slice(hd * GLA_DV, (hd + 1) * GLA_DV) for hd in range(GLA_HEADS)]
    pairs = [(c, hd) for c in range(nc) for hd in range(GLA_HEADS)]
    group = 2 * GLA_HEADS

    def score_group(g0):
        grp = pairs[g0:g0 + group]
        rws = [slice(c * CHUNK, (c + 1) * CHUNK) for c, _ in grp]
        lvl = [[lax.dot_general(lq_s[3 * hd + n, r, :], lk_s[3 * hd + n, r, :], nt,
                                preferred_element_type=F32) for n in range(3)]
               for r, (c, hd) in zip(rws, grp)]
        us = [lax.dot_general(v_s[r, vsl[hd]], kd_s[r, ksl[hd]], tn, preferred_element_type=F32)
              for r, (c, hd) in zip(rws, grp)]
        ps = [(g1 + jnp.where(same16, g2, 0.0) + jnp.where(same4, g3, 0.0)).astype(BF16)
              for g1, g2, g3 in lvl]
        pvs = [jnp.dot(p, v_s[r, vsl[hd]], preferred_element_type=F32)
               for p, r, (c, hd) in zip(ps, rws, grp)]
        for pv, u, r, (c, hd) in zip(pvs, us, rws, grp):
            o_s[r, vsl[hd]] = pv
            u_s[c * GLA_HEADS + hd] = u

    for g0 in range(0, len(pairs), group):
        score_group(g0)

    sts = [state_s[hd] for hd in range(GLA_HEADS)]
    for c in range(nc):
        for hd in range(GLA_HEADS):
            sb_s[c * GLA_HEADS + hd] = sts[hd].astype(BF16)
        sts = [sts[hd] * decay[c][:, ksl[hd]] + u_s[c * GLA_HEADS + hd]
               for hd in range(GLA_HEADS)]
    for hd in range(GLA_HEADS):
        state_s[hd] = sts[hd]

    for g0 in range(0, len(pairs), group):
        grp = pairs[g0:g0 + group]
        inters = [lax.dot_general(qe_s[c * CHUNK:(c + 1) * CHUNK, ksl[hd]],
                                  sb_s[c * GLA_HEADS + hd], nt, preferred_element_type=F32)
                  for c, hd in grp]
        for inter, (c, hd) in zip(inters, grp):
            i_s[c * CHUNK:(c + 1) * CHUNK, vsl[hd]] = inter

    for hd in range(GLA_HEADS):
        vs = slice(hd * GLA_DV, (hd + 1) * GLA_DV)
        o = o_s[:, vs] + i_s[:, vs]
        o = o * lax.rsqrt(jnp.mean(o * o, axis=-1, keepdims=True) + EPS) * gnorm_ref[:, vs]
        y_s[:, POOL_WIDTH + CONV_WIDTH + hd * GLA_DV:POOL_WIDTH + CONV_WIDTH + (hd + 1) * GLA_DV] = (
            o * gate_s[:, vs]).astype(BF16)

    mix = jnp.dot(y_s[...], wout_ref[...], preferred_element_type=F32)
    o_ref[...] = h_ref[...] + _rms(mix, post_ref[...])

    @pl.when(is_meta)
    def _():
        pmeta[...] = pbuf[0:POOL_HIST, :]
        zmeta[...] = zbuf[0:CONV_HIST, :]
        smeta[...] = state_s[...]


def _mixer(h, tiles_per_seq, pre, post, win, wout, poolw, pscale, convw, w2, b2, gnorm):
    ts = TILE
    return pl.pallas_call(
        functools.partial(_mixer_kernel, tiles_per_seq=tiles_per_seq),
        grid=(h.shape[0] // ts,),
        in_specs=[
            pl.BlockSpec((ts, D_MODEL), lambda i: (i, 0)),
            _resident((1, D_MODEL)),
            _resident((1, D_MODEL)),
            _resident((D_MODEL, D_PROJ_PACKED)),
            _resident((D_MODEL, D_MODEL)),
            _resident((POOL_WIDTH, POOL_WIDTH)),
            _resident((1, POOL_WIDTH)),
            _resident((8, CONV_WIDTH)),
            _resident((GL_PAD, GLA_KW)),
            _resident((1, GLA_KW)),
            _resident((1, GLA_WIDTH)),
        ],
        out_specs=pl.BlockSpec((ts, D_MODEL), lambda i: (i, 0)),
        out_shape=jax.ShapeDtypeStruct(h.shape, F32),
        scratch_shapes=[
            pltpu.VMEM((ts, D_MODEL), BF16),
            pltpu.VMEM((POOL_HIST + ts, POOL_WIDTH), F32),
            pltpu.VMEM((CONV_HIST + ts, CONV_WIDTH), F32),
            pltpu.VMEM((ts, GLA_KW), F32),
            pltpu.VMEM((ts, GLA_KW), F32),
            pltpu.VMEM((ts, GLA_KW), F32),
            pltpu.VMEM((3 * GLA_HEADS, ts, 128), F32),
            pltpu.VMEM((ts, GLA_KW), BF16),
            pltpu.VMEM((ts, GLA_KW), BF16),
            pltpu.VMEM((3 * GLA_HEADS, ts, 4 * GLA_DK), BF16),
            pltpu.VMEM((3 * GLA_HEADS, ts, 4 * GLA_DK), BF16),
            pltpu.VMEM((ts, GLA_WIDTH), BF16),
            pltpu.VMEM((ts, GLA_WIDTH), F32),
            pltpu.VMEM((ts, GLA_WIDTH), F32),
            pltpu.VMEM((ts, GLA_WIDTH), F32),
            pltpu.VMEM((ts // CHUNK * GLA_HEADS, GLA_DV, GLA_DK), F32),
            pltpu.VMEM((ts // CHUNK * GLA_HEADS, GLA_DV, GLA_DK), BF16),
            pltpu.VMEM((ts, D_MODEL), BF16),
            pltpu.VMEM((GLA_HEADS, GLA_DV, GLA_DK), F32),
            pltpu.VMEM((POOL_HIST, POOL_WIDTH), F32),
            pltpu.VMEM((CONV_HIST, CONV_WIDTH), F32),
            pltpu.VMEM((GLA_HEADS, GLA_DV, GLA_DK), F32),
        ],
        compiler_params=pltpu.CompilerParams(
            dimension_semantics=("arbitrary",), vmem_limit_bytes=VMEM_LIMIT),
        name="mixer",
    )(h, pre, post, win, wout, poolw, pscale, convw, w2, b2, gnorm)


def _pack_w_in(w):
    pool, cb, cc, cu, q, k, v, gl, og = jnp.split(
        w, np.cumsum((256, 256, 256, 256, 256, 256, 512, 16, 512))[:-1].tolist(), axis=-1)
    gl = jnp.pad(gl, ((0, 0), (0, GL_PAD - GLA_GATE_RANK)))
    return jnp.concatenate([pool, cb, cc, cu, q, k, v, og, gl], axis=-1).astype(BF16)


def _block_diag(w):
    out = jnp.zeros((POOL_WIDTH, POOL_WIDTH), w.dtype)
    for g in range(len(POOL_WINDOWS)):
        sl = slice(g * POOL_GROUP, (g + 1) * POOL_GROUP)
        out = out.at[sl, sl].set(w[g])
    return out


def kernel(x, meta, ffn1_pre, ffn1_post, ffn1_wg, ffn1_wu, ffn1_wd, mix_pre, mix_post, w_in,
           pool_w, pool_scale, conv_w, gla_w2, gla_b2, gla_norm, w_out,
           ffn2_pre, ffn2_post, ffn2_wg, ffn2_wu, ffn2_wd):
    n_batch, seq, d = x.shape
    depth = w_in.shape[0]
    assert d == D_MODEL and seq % TILE == 0 and meta.shape[0] == N_META
    meta_tile = jnp.pad(meta.astype(F32), ((META_PAD, 0), (0, 0)))
    h = x.reshape(n_batch * seq, d)
    row = lambda v: v.reshape(1, -1).astype(F32)
    for l in range(depth):
        h = _ffn(h, meta_tile, row(ffn1_pre[l]), row(ffn1_post[l]), ffn1_wg, ffn1_wu, ffn1_wd,
                 mode="first" if l == 0 else "mid", layer=l)
        h = _mixer(
            h, seq // TILE, row(mix_pre[l]), row(mix_post[l]), _pack_w_in(w_in[l]),
            w_out[l].astype(BF16), _block_diag(pool_w[l]).astype(BF16), row(pool_scale[l]),
            jnp.pad(conv_w[l].astype(F32), ((0, 8 - CONV_K), (0, 0))),
            jnp.pad(gla_w2[l], ((0, GL_PAD - GLA_GATE_RANK), (0, 0))).astype(BF16),
            row(gla_b2[l]), row(gla_norm[l]))
        h = _ffn(h, meta_tile, row(ffn2_pre[l]), row(ffn2_post[l]), ffn2_wg, ffn2_wu, ffn2_wd,
                 mode="last" if l == depth - 1 else "mid", layer=l)
    return h.reshape(n_batch, seq, d)
```

```python
import functools

import jax
import jax.numpy as jnp
import numpy as np
from jax import lax
from jax.experimental import pallas as pl
from jax.experimental.pallas import tpu as pltpu

F32 = jnp.float32
BF16 = jnp.bfloat16

D_MODEL = 1024
D_FF = 2816
N_META = 16
EPS = 1e-6
FFN_RESID = 0.5

POOL_WIDTH = 256
POOL_WINDOWS = (2, 4, 8, 16)
POOL_GROUP = 64
CONV_WIDTH = 256
CONV_K = 3
GLA_HEADS = 4
GLA_DK = 64
GLA_DV = 128
GLA_KW = GLA_HEADS * GLA_DK
GLA_WIDTH = GLA_HEADS * GLA_DV
GLA_GATE_RANK = 16
GLA_TAU = 16.0
LOG2_E = 1.4426950408889634
CHUNK = 64


OFF_POOL, OFF_CB, OFF_CC, OFF_CU, OFF_Q, OFF_K = 0, 256, 512, 768, 1024, 1280
OFF_V, OFF_OG, OFF_GL = 1536, 2048, 2560
GL_PAD = 128
D_PROJ_PACKED = OFF_GL + GL_PAD

V7X_VMEM_BYTES = 64 * 1024 * 1024
VMEM_LIMIT = 56 * 1024 * 1024

TILE = 512
FF_CHUNK = 256
DOWN_ROWS = 128
META_PAD = TILE - N_META
POOL_HIST = 16
CONV_HIST = 8


def _rms(x, g):
    return x * lax.rsqrt(jnp.mean(x * x, axis=-1, keepdims=True) + EPS) * g


def _load_ffn_weights(layer, wg_hbm, wu_hbm, wd_hbm, wg_ref, wu_ref, wd_ref, stage_cols,
                      stage_rows, sems):
    n_chunks = D_FF // FF_CHUNK

    def copies(j, slot):
        off = pl.multiple_of(j * FF_CHUNK, FF_CHUNK)
        return (
            pltpu.make_async_copy(wg_hbm.at[layer, :, pl.ds(off, FF_CHUNK)],
                                  stage_cols.at[0, slot], sems.at[0, slot]),
            pltpu.make_async_copy(wu_hbm.at[layer, :, pl.ds(off, FF_CHUNK)],
                                  stage_cols.at[1, slot], sems.at[1, slot]),
            pltpu.make_async_copy(wd_hbm.at[layer, pl.ds(off, FF_CHUNK), :],
                                  stage_rows.at[slot], sems.at[2, slot]),
        )

    for cp in copies(0, 0):
        cp.start()

    def body(j, carry):
        slot = lax.rem(j, 2)

        @pl.when(j + 1 < n_chunks)
        def _():
            for cp in copies(j + 1, 1 - slot):
                cp.start()

        for cp in copies(j, slot):
            cp.wait()
        off = pl.multiple_of(j * FF_CHUNK, FF_CHUNK)
        wg_ref[:, pl.ds(off, FF_CHUNK)] = stage_cols[0, slot].astype(BF16)
        wu_ref[:, pl.ds(off, FF_CHUNK)] = stage_cols[1, slot].astype(BF16)
        wd_ref[pl.ds(off, FF_CHUNK), :] = stage_rows[slot].astype(BF16)
        return carry

    lax.fori_loop(0, n_chunks, body, 0)


def _ffn_kernel(h_ref, meta_ref, pre_ref, post_ref, wg_hbm, wu_hbm, wd_hbm, o_ref,
                a_ref, z_ref, wg_ref, wu_ref, wd_ref, stage_cols, stage_rows, sems, *,
                meta_at_step0, layer):
    @pl.when(pl.program_id(0) == 0)
    def _():
        _load_ffn_weights(layer, wg_hbm, wu_hbm, wd_hbm, wg_ref, wu_ref, wd_ref,
                          stage_cols, stage_rows, sems)

    h = h_ref[...]
    if meta_at_step0:
        h = jnp.where(pl.program_id(0) == 0, meta_ref[...], h)
    blocks = [slice(r0, r0 + DOWN_ROWS) for r0 in range(0, TILE, DOWN_ROWS)]
    a_blocks = [_rms(h[rows], pre_ref[...]).astype(BF16) for rows in blocks]
    for j in range(D_FF // FF_CHUNK):
        sl = slice(j * FF_CHUNK, (j + 1) * FF_CHUNK)
        for rows, a in zip(blocks, a_blocks) if j == 0 else ((slice(0, TILE), a_ref[...]),):
            g = jnp.dot(a, wg_ref[:, sl], preferred_element_type=F32)
            u = jnp.dot(a, wu_ref[:, sl], preferred_element_type=F32)
            z_ref[rows, sl] = (g * jax.nn.sigmoid(g) * u).astype(BF16)
            if j == 0:
                a_ref[rows, :] = a
    for r0 in range(0, TILE, DOWN_ROWS):
        rows = slice(r0, r0 + DOWN_ROWS)
        f = jnp.dot(z_ref[rows, :], wd_ref[...], preferred_element_type=F32)
        o_ref[rows, :] = h[rows] + FFN_RESID * _rms(f, post_ref[...])


def _resident(shape):
    return pl.BlockSpec(shape, lambda *_: (0,) * len(shape), pipeline_mode=pl.Buffered(1))


def _ffn(src, meta_tile, pre, post, wg, wu, wd, *, mode, layer):
    n_tok_tiles = src.shape[0] // TILE - (0 if mode == "first" else 1)
    if mode == "first":
        grid, src_map = n_tok_tiles + 1, lambda i: (jnp.maximum(i - 1, 0), 0)
    elif mode == "mid":
        grid, src_map = n_tok_tiles + 1, lambda i: (i, 0)
    else:
        grid, src_map = n_tok_tiles, lambda i: (i + 1, 0)
    out_rows = grid * TILE
    return pl.pallas_call(
        functools.partial(_ffn_kernel, meta_at_step0=(mode == "first"), layer=layer),
        grid=(grid,),
        in_specs=[
            pl.BlockSpec((TILE, D_MODEL), src_map),
            _resident((TILE, D_MODEL)),
            _resident((1, D_MODEL)),
            _resident((1, D_MODEL)),
            pl.BlockSpec(memory_space=pl.ANY),
            pl.BlockSpec(memory_space=pl.ANY),
            pl.BlockSpec(memory_space=pl.ANY),
        ],
        out_specs=pl.BlockSpec((TILE, D_MODEL), lambda i: (i, 0)),
        out_shape=jax.ShapeDtypeStruct((out_rows, D_MODEL), F32),
        scratch_shapes=[
            pltpu.VMEM((TILE, D_MODEL), BF16),
            pltpu.VMEM((TILE, D_FF), BF16),
            pltpu.VMEM((D_MODEL, D_FF), BF16),
            pltpu.VMEM((D_MODEL, D_FF), BF16),
            pltpu.VMEM((D_FF, D_MODEL), BF16),
            pltpu.VMEM((2, 2, D_MODEL, FF_CHUNK), F32),
            pltpu.VMEM((2, FF_CHUNK, D_MODEL), F32),
            pltpu.SemaphoreType.DMA((3, 2)),
        ],
        compiler_params=pltpu.CompilerParams(
            dimension_semantics=("arbitrary",), vmem_limit_bytes=VMEM_LIMIT),
        name="ffn_" + mode,
    )(src, meta_tile, pre, post, wg, wu, wd)


def _mixer_kernel(h_ref, pre_ref, post_ref, win_ref, wout_ref, poolw_ref, pscale_ref,
                  convw_ref, w2_ref, b2_ref, gnorm_ref, o_ref,
                  a_s, pbuf, zbuf, b_s, q_s, k_s, rep_s, qe_s, kd_s, lq_s, lk_s, v_s, gate_s, o_s, i_s, u_s, sb_s, y_s, state_s,
                  pmeta, zmeta, smeta, *, tiles_per_seq):
    step = pl.program_id(0)
    ts = TILE
    is_meta = step == 0

    @pl.when(is_meta)
    def _():
        pbuf[0:POOL_HIST, :] = jnp.zeros((POOL_HIST, POOL_WIDTH), F32)
        zbuf[0:CONV_HIST, :] = jnp.zeros((CONV_HIST, CONV_WIDTH), F32)
        state_s[...] = jnp.zeros_like(state_s)

    @pl.when(jnp.logical_and(step > 0, lax.rem(step - 1, tiles_per_seq) == 0))
    def _():
        pbuf[0:POOL_HIST, :] = pmeta[...]
        zbuf[0:CONV_HIST, :] = zmeta[...]
        state_s[...] = smeta[...]

    nc = ts // CHUNK
    ri = lax.broadcasted_iota(jnp.int32, (CHUNK, CHUNK), 0)
    ci = lax.broadcasted_iota(jnp.int32, (CHUNK, CHUNK), 1)

    def proj(off, width):
        return jnp.dot(a_s[...], win_ref[:, off:off + width], preferred_element_type=F32)

    def pool_mixer():
        x1 = pbuf[...]
        s2 = x1 + pltpu.roll(x1, 1, 0)
        s4 = s2 + pltpu.roll(s2, 2, 0)
        s8 = s4 + pltpu.roll(s4, 4, 0)
        s16 = s8 + pltpu.roll(s8, 8, 0)
        lane = lax.broadcasted_iota(jnp.int32, (ts, POOL_WIDTH), 1)
        row = lax.broadcasted_iota(jnp.int32, (ts, POOL_WIDTH), 0)
        win = jnp.where(lane < 64, 2, jnp.where(lane < 128, 4, jnp.where(lane < 192, 8, 16)))
        wsum = jnp.where(lane < 64, s2[POOL_HIST:], jnp.where(
            lane < 128, s4[POOL_HIST:], jnp.where(lane < 192, s8[POOL_HIST:], s16[POOL_HIST:])))
        pos1 = jnp.where(is_meta, row - (META_PAD - 1), max(POOL_WINDOWS))
        cnt = jnp.clip(pos1, 1, win).astype(F32)
        m = (wsum / cnt - x1[POOL_HIST:]).astype(BF16)
        y_pool = jnp.dot(m, poolw_ref[...], preferred_element_type=F32) * pscale_ref[...]
        y_s[:, 0:POOL_WIDTH] = y_pool.astype(BF16)
        pbuf[0:POOL_HIST, :] = x1[ts:ts + POOL_HIST]

    def conv_mixer():
        zz = zbuf[...]
        z1 = pltpu.roll(zz, 1, 0)
        z2 = pltpu.roll(zz, 2, 0)
        cw = convw_ref[...]
        yc = (cw[0:1, :] * z2[CONV_HIST:] + cw[1:2, :] * z1[CONV_HIST:]
              + cw[2:3, :] * zz[CONV_HIST:])
        y_s[:, POOL_WIDTH:POOL_WIDTH + CONV_WIDTH] = (proj(OFF_CB, CONV_WIDTH) * yc).astype(BF16)
        zbuf[0:CONV_HIST, :] = zz[ts:ts + CONV_HIST]

    @pl.when(step >= 0)
    def _():
        a_s[...] = _rms(h_ref[...], pre_ref[...]).astype(BF16)
        g_low = proj(OFF_GL, GL_PAD).astype(BF16)
        gx = jnp.dot(g_low, w2_ref[...], preferred_element_type=F32) + b2_ref[...]
        log_sig = jnp.minimum(gx, 0.0) - jnp.log(1.0 + jnp.exp(-jnp.abs(gx)))
        la = log_sig * (LOG2_E / GLA_TAU)
        la_hi = la.astype(BF16)
        la_lo = (la - la_hi.astype(F32)).astype(BF16)
        la_cat = jnp.concatenate([la_hi, la_lo], axis=1)
        pbuf[POOL_HIST:, :] = proj(OFF_POOL, POOL_WIDTH)
        zbuf[CONV_HIST:, :] = proj(OFF_CC, CONV_WIDTH) * proj(OFF_CU, CONV_WIDTH)
        tril = jnp.where(ri >= ci, 1.0, 0.0).astype(BF16)
        for c in range(nc):
            rows = slice(c * CHUNK, (c + 1) * CHUNK)
            bb = jnp.dot(tril, la_cat[rows, :], preferred_element_type=F32)
            b_s[rows, :] = bb[:, :GLA_KW] + bb[:, GLA_KW:]
        q_s[...] = proj(OFF_Q, GLA_KW) * (GLA_DK ** -0.5)
        k_s[...] = proj(OFF_K, GLA_KW)
        pool_mixer()
        conv_mixer()
        lane_hi2 = lax.broadcasted_iota(jnp.int32, (1, 128), 1) >= GLA_DK
        for n, src in enumerate((b_s, q_s, k_s)):
            for c2 in range(2):
                x = src[:, 128 * c2:128 * c2 + 128]
                x_sw = pltpu.roll(x, GLA_DK, 1)
                rep_s[GLA_HEADS * n + 2 * c2] = jnp.where(lane_hi2, x_sw, x)
                rep_s[GLA_HEADS * n + 2 * c2 + 1] = jnp.where(lane_hi2, x, x_sw)

    b, q, k = b_s[...], q_s[...], k_s[...]
    b3 = b.reshape(nc, CHUNK, GLA_KW)
    b_last = b3[:, CHUNK - 1:CHUNK, :]
    qe_s[...] = (q.reshape(b3.shape) * jnp.exp2(b3)).reshape(ts, GLA_KW).astype(BF16)
    kd_s[...] = (k.reshape(b3.shape) * jnp.exp2(b_last - b3)).reshape(ts, GLA_KW).astype(BF16)
    decay = jnp.exp2(b_last)

    lane_hi = lax.broadcasted_iota(jnp.int32, (1, 1, 128), 2) >= GLA_DK
    t64 = lax.broadcasted_iota(jnp.int32, (1, CHUNK, 1), 1)
    blk16, blk4, pos4 = t64 // 16, (t64 // 4) % 4, t64 % 4
    in_hi4 = lax.broadcasted_iota(jnp.int32, (1, 8, 1), 1) >= 4
    neg = -1e30

    def build_level_operands(hd):
        rb, rq, rk = (rep_s[GLA_HEADS * n + hd] for n in range(3))
        rb64, rq64, rk64 = (x.reshape(nc, CHUNK, 128) for x in (rb, rq, rk))
        rb16 = rb.reshape(ts // 16, 16, 128)
        rb8 = rb.reshape(ts // 8, 8, 128)
        to64 = lambda x: x.reshape(nc, CHUNK, 128)
        end4 = jnp.where(in_hi4, rb8[:, 7:8, :], rb8[:, 3:4, :])
        k1 = rk64 * to64(jnp.exp2(rb16[:, 15:16, :] - rb16))
        k2 = rk64 * to64(jnp.exp2(end4 - rb8))
        nxt = jnp.where(lane_hi, pltpu.roll(rb8, 7, 1), rb8)
        for p in range(2):
            col = slice(128 * p, 128 * p + 128)
            j = jnp.where(lane_hi, 2 * p + 1, 2 * p)
            ref1 = jnp.where(lane_hi, rb64[:, 32 * p + 31:32 * p + 32, :],
                             rb64[:, 32 * p + 15:32 * p + 16, :])
            ref2 = jnp.where(lane_hi, rb16[:, 8 * p + 7:8 * p + 8, :], rb16[:, 8 * p + 3:8 * p + 4, :])
            ref3 = jnp.where(in_hi4, nxt[:, 4 + 2 * p:5 + 2 * p, :], nxt[:, 2 * p:2 * p + 1, :])
            q1 = rq64 * jnp.exp2(jnp.where(blk16 > j, rb64 - ref1, neg))
            q2 = rq64 * jnp.exp2(jnp.where(blk4 > j, to64(rb16 - ref2), neg))
            q3 = rq64 * jnp.exp2(jnp.where(pos4 >= j, to64(rb8 - ref3), neg))
            for n, (ql, kl) in enumerate(((q1, jnp.where(blk16 == j, k1, 0.0)),
                                          (q2, jnp.where(blk4 == j, k2, 0.0)),
                                          (q3, jnp.where(pos4 == j, rk64, 0.0)))):
                lq_s[3 * hd + n, :, col] = ql.reshape(ts, 128).astype(BF16)
                lk_s[3 * hd + n, :, col] = kl.reshape(ts, 128).astype(BF16)

    def proj_rows(r0, nr, off, width):
        return jnp.dot(a_s[r0:r0 + nr, :], win_ref[:, off:off + width],
                       preferred_element_type=F32)

    quarter = ts // 4
    build_level_operands(0)
    for r0 in range(0, ts, quarter):
        v_s[r0:r0 + quarter, :] = proj_rows(r0, quarter, OFF_V, GLA_WIDTH).astype(BF16)
    build_level_operands(1)
    for r0 in range(0, ts, quarter):
        og = proj_rows(r0, quarter, OFF_OG, GLA_WIDTH)
        gate_s[r0:r0 + quarter, :] = og * jax.nn.sigmoid(og)
    build_level_operands(2)
    build_level_operands(3)

    same16 = (ri // 16) == (ci // 16)
    same4 = (ri // 4) == (ci // 4)
    nt = (((1,), (1,)), ((), ()))
    tn = (((0,), (0,)), ((), ()))
    ksl = [slice(hd * GLA_DK, (hd + 1) * GLA_DK) for hd in range(GLA_HEADS)]
    vsl = [slice(hd * GLA_DV, (hd + 1) * GLA_DV) for hd in range(GLA_HEADS)]
    pairs = [(c, hd) for c in range(nc) for hd in range(GLA_HEADS)]
    group = 2 * GLA_HEADS

    def score_group(g0):
        grp = pairs[g0:g0 + group]
        rws = [slice(c * CHUNK, (c + 1) * CHUNK) for c, _ in grp]
        lvl = [[lax.dot_general(lq_s[3 * hd + n, r, :], lk_s[3 * hd + n, r, :], nt,
                                preferred_element_type=F32) for n in range(3)]
               for r, (c, hd) in zip(rws, grp)]
        us = [lax.dot_general(v_s[r, vsl[hd]], kd_s[r, ksl[hd]], tn, preferred_element_type=F32)
              for r, (c, hd) in zip(rws, grp)]
        ps = [(g1 + jnp.where(same16, g2, 0.0) + jnp.where(same4, g3, 0.0)).astype(BF16)
              for g1, g2, g3 in lvl]
        pvs = [jnp.dot(p, v_s[r, vsl[hd]], preferred_element_type=F32)
               for p, r, (c, hd) in zip(ps, rws, grp)]
        for pv, u, r, (c, hd) in zip(pvs, us, rws, grp):
            o_s[r, vsl[hd]] = pv
            u_s[c * GLA_HEADS + hd] = u

    for g0 in range(0, len(pairs), group):
        score_group(g0)

    sts = [state_s[hd] for hd in range(GLA_HEADS)]
    for c in range(nc):
        for hd in range(GLA_HEADS):
            sb_s[c * GLA_HEADS + hd] = sts[hd].astype(BF16)
        sts = [sts[hd] * decay[c][:, ksl[hd]] + u_s[c * GLA_HEADS + hd]
               for hd in range(GLA_HEADS)]
    for hd in range(GLA_HEADS):
        state_s[hd] = sts[hd]

    for g0 in range(0, len(pairs), group):
        grp = pairs[g0:g0 + group]
        inters = [lax.dot_general(qe_s[c * CHUNK:(c + 1) * CHUNK, ksl[hd]],
                                  sb_s[c * GLA_HEADS + hd], nt, preferred_element_type=F32)
                  for c, hd in grp]
        for inter, (c, hd) in zip(inters, grp):
            i_s[c * CHUNK:(c + 1) * CHUNK, vsl[hd]] = inter

    for hd in range(GLA_HEADS):
        vs = slice(hd * GLA_DV, (hd + 1) * GLA_DV)
        o = o_s[:, vs] + i_s[:, vs]
        o = o * lax.rsqrt(jnp.mean(o * o, axis=-1, keepdims=True) + EPS) * gnorm_ref[:, vs]
        y_s[:, POOL_WIDTH + CONV_WIDTH + hd * GLA_DV:POOL_WIDTH + CONV_WIDTH + (hd + 1) * GLA_DV] = (
            o * gate_s[:, vs]).astype(BF16)

    mix = jnp.dot(y_s[...], wout_ref[...], preferred_element_type=F32)
    o_ref[...] = h_ref[...] + _rms(mix, post_ref[...])

    @pl.when(is_meta)
    def _():
        pmeta[...] = pbuf[0:POOL_HIST, :]
        zmeta[...] = zbuf[0:CONV_HIST, :]
        smeta[...] = state_s[...]


def _mixer(h, tiles_per_seq, pre, post, win, wout, poolw, pscale, convw, w2, b2, gnorm):
    ts = TILE
    return pl.pallas_call(
        functools.partial(_mixer_kernel, tiles_per_seq=tiles_per_seq),
        grid=(h.shape[0] // ts,),
        in_specs=[
            pl.BlockSpec((ts, D_MODEL), lambda i: (i, 0)),
            _resident((1, D_MODEL)),
            _resident((1, D_MODEL)),
            _resident((D_MODEL, D_PROJ_PACKED)),
            _resident((D_MODEL, D_MODEL)),
            _resident((POOL_WIDTH, POOL_WIDTH)),
            _resident((1, POOL_WIDTH)),
            _resident((8, CONV_WIDTH)),
            _resident((GL_PAD, GLA_KW)),
            _resident((1, GLA_KW)),
            _resident((1, GLA_WIDTH)),
        ],
        out_specs=pl.BlockSpec((ts, D_MODEL), lambda i: (i, 0)),
        out_shape=jax.ShapeDtypeStruct(h.shape, F32),
        scratch_shapes=[
            pltpu.VMEM((ts, D_MODEL), BF16),
            pltpu.VMEM((POOL_HIST + ts, POOL_WIDTH), F32),
            pltpu.VMEM((CONV_HIST + ts, CONV_WIDTH), F32),
            pltpu.VMEM((ts, GLA_KW), F32),
            pltpu.VMEM((ts, GLA_KW), F32),
            pltpu.VMEM((ts, GLA_KW), F32),
            pltpu.VMEM((3 * GLA_HEADS, ts, 128), F32),
            pltpu.VMEM((ts, GLA_KW), BF16),
            pltpu.VMEM((ts, GLA_KW), BF16),
            pltpu.VMEM((3 * GLA_HEADS, ts, 4 * GLA_DK), BF16),
            pltpu.VMEM((3 * GLA_HEADS, ts, 4 * GLA_DK), BF16),
            pltpu.VMEM((ts, GLA_WIDTH), BF16),
            pltpu.VMEM((ts, GLA_WIDTH), F32),
            pltpu.VMEM((ts, GLA_WIDTH), F32),
            pltpu.VMEM((ts, GLA_WIDTH), F32),
            pltpu.VMEM((ts // CHUNK * GLA_HEADS, GLA_DV, GLA_DK), F32),
            pltpu.VMEM((ts // CHUNK * GLA_HEADS, GLA_DV, GLA_DK), BF16),
            pltpu.VMEM((ts, D_MODEL), BF16),
            pltpu.VMEM((GLA_HEADS, GLA_DV, GLA_DK), F32),
            pltpu.VMEM((POOL_HIST, POOL_WIDTH), F32),
            pltpu.VMEM((CONV_HIST, CONV_WIDTH), F32),
            pltpu.VMEM((GLA_HEADS, GLA_DV, GLA_DK), F32),
        ],
        compiler_params=pltpu.CompilerParams(
            dimension_semantics=("arbitrary",), vmem_limit_bytes=VMEM_LIMIT),
        name="mixer",
    )(h, pre, post, win, wout, poolw, pscale, convw, w2, b2, gnorm)


def _pack_w_in(w):
    pool, cb, cc, cu, q, k, v, gl, og = jnp.split(
        w, np.cumsum((256, 256, 256, 256, 256, 256, 512, 16, 512))[:-1].tolist(), axis=-1)
    gl = jnp.pad(gl, ((0, 0), (0, GL_PAD - GLA_GATE_RANK)))
    return jnp.concatenate([pool, cb, cc, cu, q, k, v, og, gl], axis=-1).astype(BF16)


def _block_diag(w):
    out = jnp.zeros((POOL_WIDTH, POOL_WIDTH), w.dtype)
    for g in range(len(POOL_WINDOWS)):
        sl = slice(g * POOL_GROUP, (g + 1) * POOL_GROUP)
        out = out.at[sl, sl].set(w[g])
    return out


def kernel(x, meta, ffn1_pre, ffn1_post, ffn1_wg, ffn1_wu, ffn1_wd, mix_pre, mix_post, w_in,
           pool_w, pool_scale, conv_w, gla_w2, gla_b2, gla_norm, w_out,
           ffn2_pre, ffn2_post, ffn2_wg, ffn2_wu, ffn2_wd):
    n_batch, seq, d = x.shape
    depth = w_in.shape[0]
    assert d == D_MODEL and seq % TILE == 0 and meta.shape[0] == N_META
    meta_tile = jnp.pad(meta.astype(F32), ((META_PAD, 0), (0, 0)))
    h = x.reshape(n_batch * seq, d)
    row = lambda v: v.reshape(1, -1).astype(F32)
    for l in range(depth):
        h = _ffn(h, meta_tile, row(ffn1_pre[l]), row(ffn1_post[l]), ffn1_wg, ffn1_wu, ffn1_wd,
                 mode="first" if l == 0 else "mid", layer=l)
        h = _mixer(
            h, seq // TILE, row(mix_pre[l]), row(mix_post[l]), _pack_w_in(w_in[l]),
            w_out[l].astype(BF16), _block_diag(pool_w[l]).astype(BF16), row(pool_scale[l]),
            jnp.pad(conv_w[l].astype(F32), ((0, 8 - CONV_K), (0, 0))),
            jnp.pad(gla_w2[l], ((0, GL_PAD - GLA_GATE_RANK), (0, 0))).astype(BF16),
            row(gla_b2[l]), row(gla_norm[l]))
        h = _ffn(h, meta_tile, row(ffn2_pre[l]), row(ffn2_post[l]), ffn2_wg, ffn2_wu, ffn2_wd,
                 mode="last" if l == depth - 1 else "mid", layer=l)
    return h.reshape(n_batch, seq, d)
```

```python
import functools

import jax
import jax.numpy as jnp
import numpy as np
from jax import lax
from jax.experimental import pallas as pl
from jax.experimental.pallas import tpu as pltpu

F32 = jnp.float32
BF16 = jnp.bfloat16

D_MODEL = 1024
D_FF = 2816
N_META = 16
EPS = 1e-6
FFN_RESID = 0.5

POOL_WIDTH = 256
POOL_WINDOWS = (2, 4, 8, 16)
POOL_GROUP = 64
CONV_WIDTH = 256
CONV_K = 3
GLA_HEADS = 4
GLA_DK = 64
GLA_DV = 128
GLA_KW = GLA_HEADS * GLA_DK
GLA_WIDTH = GLA_HEADS * GLA_DV
GLA_GATE_RANK = 16
GLA_TAU = 16.0
LOG2_E = 1.4426950408889634
CHUNK = 64


OFF_POOL, OFF_CB, OFF_CC, OFF_CU, OFF_Q, OFF_K = 0, 256, 512, 768, 1024, 1280
OFF_V, OFF_OG, OFF_GL = 1536, 2048, 2560
GL_PAD = 128
D_PROJ_PACKED = OFF_GL + GL_PAD

V7X_VMEM_BYTES = 64 * 1024 * 1024
VMEM_LIMIT = 56 * 1024 * 1024

TILE = 512
FF_CHUNK = 256
DOWN_ROWS = 256
META_PAD = TILE - N_META
POOL_HIST = 16
CONV_HIST = 8


def _rms(x, g):
    return x * lax.rsqrt(jnp.mean(x * x, axis=-1, keepdims=True) + EPS) * g


def _load_ffn_weights(layer, wg_hbm, wu_hbm, wd_hbm, wg_ref, wu_ref, wd_ref, stage_cols,
                      stage_rows, sems):
    n_chunks = D_FF // FF_CHUNK

    def copies(j, slot):
        off = pl.multiple_of(j * FF_CHUNK, FF_CHUNK)
        return (
            pltpu.make_async_copy(wg_hbm.at[layer, :, pl.ds(off, FF_CHUNK)],
                                  stage_cols.at[0, slot], sems.at[0, slot]),
            pltpu.make_async_copy(wu_hbm.at[layer, :, pl.ds(off, FF_CHUNK)],
                                  stage_cols.at[1, slot], sems.at[1, slot]),
            pltpu.make_async_copy(wd_hbm.at[layer, pl.ds(off, FF_CHUNK), :],
                                  stage_rows.at[slot], sems.at[2, slot]),
        )

    for cp in copies(0, 0):
        cp.start()

    def body(j, carry):
        slot = lax.rem(j, 2)

        @pl.when(j + 1 < n_chunks)
        def _():
            for cp in copies(j + 1, 1 - slot):
                cp.start()

        for cp in copies(j, slot):
            cp.wait()
        off = pl.multiple_of(j * FF_CHUNK, FF_CHUNK)
        wg_ref[:, pl.ds(off, FF_CHUNK)] = stage_cols[0, slot].astype(BF16)
        wu_ref[:, pl.ds(off, FF_CHUNK)] = stage_cols[1, slot].astype(BF16)
        wd_ref[pl.ds(off, FF_CHUNK), :] = stage_rows[slot].astype(BF16)
        return carry

    lax.fori_loop(0, n_chunks, body, 0)


def _ffn_kernel(h_ref, meta_ref, pre_ref, post_ref, wg_hbm, wu_hbm, wd_hbm, o_ref,
                a_ref, z_ref, wg_ref, wu_ref, wd_ref, stage_cols, stage_rows, sems, *,
                meta_at_step0, layer):
    @pl.when(pl.program_id(0) == 0)
    def _():
        _load_ffn_weights(layer, wg_hbm, wu_hbm, wd_hbm, wg_ref, wu_ref, wd_ref,
                          stage_cols, stage_rows, sems)

    h = h_ref[...]
    if meta_at_step0:
        h = jnp.where(pl.program_id(0) == 0, meta_ref[...], h)
    blocks = [slice(r0, r0 + DOWN_ROWS) for r0 in range(0, TILE, DOWN_ROWS)]
    a_blocks = [_rms(h[rows], pre_ref[...]).astype(BF16) for rows in blocks]
    for j in range(D_FF // FF_CHUNK):
        sl = slice(j * FF_CHUNK, (j + 1) * FF_CHUNK)
        for rows, a in zip(blocks, a_blocks) if j == 0 else ((slice(0, TILE), a_ref[...]),):
            g = jnp.dot(a, wg_ref[:, sl], preferred_element_type=F32)
            u = jnp.dot(a, wu_ref[:, sl], preferred_element_type=F32)
            z_ref[rows, sl] = (g * jax.nn.sigmoid(g) * u).astype(BF16)
            if j == 0:
                a_ref[rows, :] = a
    for r0 in range(0, TILE, DOWN_ROWS):
        rows = slice(r0, r0 + DOWN_ROWS)
        f = jnp.dot(z_ref[rows, :], wd_ref[...], preferred_element_type=F32)
        o_ref[rows, :] = h[rows] + FFN_RESID * _rms(f, post_ref[...])


def _resident(shape):
    return pl.BlockSpec(shape, lambda *_: (0,) * len(shape), pipeline_mode=pl.Buffered(1))


def _ffn(src, meta_tile, pre, post, wg, wu, wd, *, mode, layer):
    n_tok_tiles = src.shape[0] // TILE - (0 if mode == "first" else 1)
    if mode == "first":
        grid, src_map = n_tok_tiles + 1, lambda i: (jnp.maximum(i - 1, 0), 0)
    elif mode == "mid":
        grid, src_map = n_tok_tiles + 1, lambda i: (i, 0)
    else:
        grid, src_map = n_tok_tiles, lambda i: (i + 1, 0)
    out_rows = grid * TILE
    return pl.pallas_call(
        functools.partial(_ffn_kernel, meta_at_step0=(mode == "first"), layer=layer),
        grid=(grid,),
        in_specs=[
            pl.BlockSpec((TILE, D_MODEL), src_map),
            _resident((TILE, D_MODEL)),
            _resident((1, D_MODEL)),
            _resident((1, D_MODEL)),
            pl.BlockSpec(memory_space=pl.ANY),
            pl.BlockSpec(memory_space=pl.ANY),
            pl.BlockSpec(memory_space=pl.ANY),
        ],
        out_specs=pl.BlockSpec((TILE, D_MODEL), lambda i: (i, 0)),
        out_shape=jax.ShapeDtypeStruct((out_rows, D_MODEL), F32),
        scratch_shapes=[
            pltpu.VMEM((TILE, D_MODEL), BF16),
            pltpu.VMEM((TILE, D_FF), BF16),
            pltpu.VMEM((D_MODEL, D_FF), BF16),
            pltpu.VMEM((D_MODEL, D_FF), BF16),
            pltpu.VMEM((D_FF, D_MODEL), BF16),
            pltpu.VMEM((2, 2, D_MODEL, FF_CHUNK), F32),
            pltpu.VMEM((2, FF_CHUNK, D_MODEL), F32),
            pltpu.SemaphoreType.DMA((3, 2)),
        ],
        compiler_params=pltpu.CompilerParams(
            dimension_semantics=("arbitrary",), vmem_limit_bytes=VMEM_LIMIT),
        name="ffn_" + mode,
    )(src, meta_tile, pre, post, wg, wu, wd)


def _mixer_kernel(h_ref, pre_ref, post_ref, win_ref, wout_ref, poolw_ref, pscale_ref,
                  convw_ref, w2_ref, b2_ref, gnorm_ref, o_ref,
                  a_s, pbuf, zbuf, b_s, q_s, k_s, rep_s, qe_s, kd_s, lq_s, lk_s, v_s, gate_s, o_s, i_s, u_s, sb_s, y_s, state_s,
                  pmeta, zmeta, smeta, *, tiles_per_seq):
    step = pl.program_id(0)
    ts = TILE
    is_meta = step == 0

    @pl.when(is_meta)
    def _():
        pbuf[0:POOL_HIST, :] = jnp.zeros((POOL_HIST, POOL_WIDTH), F32)
        zbuf[0:CONV_HIST, :] = jnp.zeros((CONV_HIST, CONV_WIDTH), F32)
        state_s[...] = jnp.zeros_like(state_s)

    @pl.when(jnp.logical_and(step > 0, lax.rem(step - 1, tiles_per_seq) == 0))
    def _():
        pbuf[0:POOL_HIST, :] = pmeta[...]
        zbuf[0:CONV_HIST, :] = zmeta[...]
        state_s[...] = smeta[...]

    nc = ts // CHUNK
    ri = lax.broadcasted_iota(jnp.int32, (CHUNK, CHUNK), 0)
    ci = lax.broadcasted_iota(jnp.int32, (CHUNK, CHUNK), 1)

    def proj(off, width):
        return jnp.dot(a_s[...], win_ref[:, off:off + width], preferred_element_type=F32)

    def pool_mixer():
        x1 = pbuf[...]
        s2 = x1 + pltpu.roll(x1, 1, 0)
        s4 = s2 + pltpu.roll(s2, 2, 0)
        s8 = s4 + pltpu.roll(s4, 4, 0)
        s16 = s8 + pltpu.roll(s8, 8, 0)
        lane = lax.broadcasted_iota(jnp.int32, (ts, POOL_WIDTH), 1)
        row = lax.broadcasted_iota(jnp.int32, (ts, POOL_WIDTH), 0)
        win = jnp.where(lane < 64, 2, jnp.where(lane < 128, 4, jnp.where(lane < 192, 8, 16)))
        wsum = jnp.where(lane < 64, s2[POOL_HIST:], jnp.where(
            lane < 128, s4[POOL_HIST:], jnp.where(lane < 192, s8[POOL_HIST:], s16[POOL_HIST:])))
        pos1 = jnp.where(is_meta, row - (META_PAD - 1), max(POOL_WINDOWS))
        cnt = jnp.clip(pos1, 1, win).astype(F32)
        m = (wsum / cnt - x1[POOL_HIST:]).astype(BF16)
        y_pool = jnp.dot(m, poolw_ref[...], preferred_element_type=F32) * pscale_ref[...]
        y_s[:, 0:POOL_WIDTH] = y_pool.astype(BF16)
        pbuf[0:POOL_HIST, :] = x1[ts:ts + POOL_HIST]

    def conv_mixer():
        zz = zbuf[...]
        z1 = pltpu.roll(zz, 1, 0)
        z2 = pltpu.roll(zz, 2, 0)
        cw = convw_ref[...]
        yc = (cw[0:1, :] * z2[CONV_HIST:] + cw[1:2, :] * z1[CONV_HIST:]
              + cw[2:3, :] * zz[CONV_HIST:])
        y_s[:, POOL_WIDTH:POOL_WIDTH + CONV_WIDTH] = (proj(OFF_CB, CONV_WIDTH) * yc).astype(BF16)
        zbuf[0:CONV_HIST, :] = zz[ts:ts + CONV_HIST]

    @pl.when(step >= 0)
    def _():
        a_s[...] = _rms(h_ref[...], pre_ref[...]).astype(BF16)
        g_low = proj(OFF_GL, GL_PAD).astype(BF16)
        gx = jnp.dot(g_low, w2_ref[...], preferred_element_type=F32) + b2_ref[...]
        log_sig = jnp.minimum(gx, 0.0) - jnp.log(1.0 + jnp.exp(-jnp.abs(gx)))
        la = log_sig * (LOG2_E / GLA_TAU)
        la_hi = la.astype(BF16)
        la_lo = (la - la_hi.astype(F32)).astype(BF16)
        la_cat = jnp.concatenate([la_hi, la_lo], axis=1)
        pbuf[POOL_HIST:, :] = proj(OFF_POOL, POOL_WIDTH)
        zbuf[CONV_HIST:, :] = proj(OFF_CC, CONV_WIDTH) * proj(OFF_CU, CONV_WIDTH)
        tril = jnp.where(ri >= ci, 1.0, 0.0).astype(BF16)
        for c in range(nc):
            rows = slice(c * CHUNK, (c + 1) * CHUNK)
            bb = jnp.dot(tril, la_cat[rows, :], preferred_element_type=F32)
            b_s[rows, :] = bb[:, :GLA_KW] + bb[:, GLA_KW:]
        q_s[...] = proj(OFF_Q, GLA_KW) * (GLA_DK ** -0.5)
        k_s[...] = proj(OFF_K, GLA_KW)
        pool_mixer()
        conv_mixer()
        lane_hi2 = lax.broadcasted_iota(jnp.int32, (1, 128), 1) >= GLA_DK
        for n, src in enumerate((b_s, q_s, k_s)):
            for c2 in range(2):
                x = src[:, 128 * c2:128 * c2 + 128]
                x_sw = pltpu.roll(x, GLA_DK, 1)
                rep_s[GLA_HEADS * n + 2 * c2] = jnp.where(lane_hi2, x_sw, x)
                rep_s[GLA_HEADS * n + 2 * c2 + 1] = jnp.where(lane_hi2, x, x_sw)

    b, q, k = b_s[...], q_s[...], k_s[...]
    b3 = b.reshape(nc, CHUNK, GLA_KW)
    b_last = b3[:, CHUNK - 1:CHUNK, :]
    qe_s[...] = (q.reshape(b3.shape) * jnp.exp2(b3)).reshape(ts, GLA_KW).astype(BF16)
    kd_s[...] = (k.reshape(b3.shape) * jnp.exp2(b_last - b3)).reshape(ts, GLA_KW).astype(BF16)
    decay = jnp.exp2(b_last)

    lane_hi = lax.broadcasted_iota(jnp.int32, (1, 1, 128), 2) >= GLA_DK
    t64 = lax.broadcasted_iota(jnp.int32, (1, CHUNK, 1), 1)
    blk16, blk4, pos4 = t64 // 16, (t64 // 4) % 4, t64 % 4
    in_hi4 = lax.broadcasted_iota(jnp.int32, (1, 8, 1), 1) >= 4
    neg = -1e30

    def build_level_operands(hd):
        rb, rq, rk = (rep_s[GLA_HEADS * n + hd] for n in range(3))
        rb64, rq64, rk64 = (x.reshape(nc, CHUNK, 128) for x in (rb, rq, rk))
        rb16 = rb.reshape(ts // 16, 16, 128)
        rb8 = rb.reshape(ts // 8, 8, 128)
        to64 = lambda x: x.reshape(nc, CHUNK, 128)
        end4 = jnp.where(in_hi4, rb8[:, 7:8, :], rb8[:, 3:4, :])
        k1 = rk64 * to64(jnp.exp2(rb16[:, 15:16, :] - rb16))
        k2 = rk64 * to64(jnp.exp2(end4 - rb8))
        nxt = jnp.where(lane_hi, pltpu.roll(rb8, 7, 1), rb8)
        for p in range(2):
            col = slice(128 * p, 128 * p + 128)
            j = jnp.where(lane_hi, 2 * p + 1, 2 * p)
            ref1 = jnp.where(lane_hi, rb64[:, 32 * p + 31:32 * p + 32, :],
                             rb64[:, 32 * p + 15:32 * p + 16, :])
            ref2 = jnp.where(lane_hi, rb16[:, 8 * p + 7:8 * p + 8, :], rb16[:, 8 * p + 3:8 * p + 4, :])
            ref3 = jnp.where(in_hi4, nxt[:, 4 + 2 * p:5 + 2 * p, :], nxt[:, 2 * p:2 * p + 1, :])
            q1 = rq64 * jnp.exp2(jnp.where(blk16 > j, rb64 - ref1, neg))
            q2 = rq64 * jnp.exp2(jnp.where(blk4 > j, to64(rb16 - ref2), neg))
            q3 = rq64 * jnp.exp2(jnp.where(pos4 >= j, to64(rb8 - ref3), neg))
            for n, (ql, kl) in enumerate(((q1, jnp.where(blk16 == j, k1, 0.0)),
                                          (q2, jnp.where(blk4 == j, k2, 0.0)),
                                          (q3, jnp.where(pos4 == j, rk64, 0.0)))):
                lq_s[3 * hd + n, :, col] = ql.reshape(ts, 128).astype(BF16)
                lk_s[3 * hd + n, :, col] = kl.reshape(ts, 128).astype(BF16)

    def proj_rows(r0, nr, off, width):
        return jnp.dot(a_s[r0:r0 + nr, :], win_ref[:, off:off + width],
                       preferred_element_type=F32)

    quarter = ts // 4
    build_level_operands(0)
    for r0 in range(0, ts, quarter):
        v_s[r0:r0 + quarter, :] = proj_rows(r0, quarter, OFF_V, GLA_WIDTH).astype(BF16)
    build_level_operands(1)
    for r0 in range(0, ts, quarter):
        og = proj_rows(r0, quarter, OFF_OG, GLA_WIDTH)
        gate_s[r0:r0 + quarter, :] = og * jax.nn.sigmoid(og)
    build_level_operands(2)
    build_level_operands(3)

    same16 = (ri // 16) == (ci // 16)
    same4 = (ri // 4) == (ci // 4)
    nt = (((1,), (1,)), ((), ()))
    tn = (((0,), (0,)), ((), ()))
    ksl = [slice(hd * GLA_DK, (hd + 1) * GLA_DK) for hd in range(GLA_HEADS)]
    vsl = [slice(hd * GLA_DV, (hd + 1) * GLA_DV) for hd in range(GLA_HEADS)]
    pairs = [(c, hd) for c in range(nc) for hd in range(GLA_HEADS)]
    group = 2 * GLA_HEADS

    def score_group(g0):
        grp = pairs[g0:g0 + group]
        rws = [slice(c * CHUNK, (c + 1) * CHUNK) for c, _ in grp]
        lvl = [[lax.dot_general(lq_s[3 * hd + n, r, :], lk_s[3 * hd + n, r, :], nt,
                                preferred_element_type=F32) for n in range(3)]
               for r, (c, hd) in zip(rws, grp)]
        us = [lax.dot_general(v_s[r, vsl[hd]], kd_s[r, ksl[hd]], tn, preferred_element_type=F32)
              for r, (c, hd) in zip(rws, grp)]
        ps = [(g1 + jnp.where(same16, g2, 0.0) + jnp.where(same4, g3, 0.0)).astype(BF16)
              for g1, g2, g3 in lvl]
        pvs = [jnp.dot(p, v_s[r, vsl[hd]], preferred_element_type=F32)
               for p, r, (c, hd) in zip(ps, rws, grp)]
        for pv, u, r, (c, hd) in zip(pvs, us, rws, grp):
            o_s[r, vsl[hd]] = pv
            u_s[c * GLA_HEADS + hd] = u

    for g0 in range(0, len(pairs), group):
        score_group(g0)

    sts = [state_s[hd] for hd in range(GLA_HEADS)]
    for c in range(nc):
        for hd in range(GLA_HEADS):
            sb_s[c * GLA_HEADS + hd] = sts[hd].astype(BF16)
        sts = [sts[hd] * decay[c][:, ksl[hd]] + u_s[c * GLA_HEADS + hd]
               for hd in range(GLA_HEADS)]
    for hd in range(GLA_HEADS):
        state_s[hd] = sts[hd]

    for g0 in range(0, len(pairs), group):
        grp = pairs[g0:g0 + group]
        inters = [lax.dot_general(qe_s[c * CHUNK:(c + 1) * CHUNK, ksl[hd]],
                                  sb_s[c * GLA_HEADS + hd], nt, preferred_element_type=F32)
                  for c, hd in grp]
        for inter, (c, hd) in zip(inters, grp):
            i_s[c * CHUNK:(c + 1) * CHUNK, vsl[hd]] = inter

    for hd in range(GLA_HEADS):
        vs = slice(hd * GLA_DV, (hd + 1) * GLA_DV)
        o = o_s[:, vs] + i_s[:, vs]
        o = o * lax.rsqrt(jnp.mean(o * o, axis=-1, keepdims=True) + EPS) * gnorm_ref[:, vs]
        y_s[:, POOL_WIDTH + CONV_WIDTH + hd * GLA_DV:POOL_WIDTH + CONV_WIDTH + (hd + 1) * GLA_DV] = (
            o * gate_s[:, vs]).astype(BF16)

    mix = jnp.dot(y_s[...], wout_ref[...], preferred_element_type=F32)
    o_ref[...] = h_ref[...] + _rms(mix, post_ref[...])

    @pl.when(is_meta)
    def _():
        pmeta[...] = pbuf[0:POOL_HIST, :]
        zmeta[...] = zbuf[0:CONV_HIST, :]
        smeta[...] = state_s[...]


def _mixer(h, tiles_per_seq, pre, post, win, wout, poolw, pscale, convw, w2, b2, gnorm):
    ts = TILE
    return pl.pallas_call(
        functools.partial(_mixer_kernel, tiles_per_seq=tiles_per_seq),
        grid=(h.shape[0] // ts,),
        in_specs=[
            pl.BlockSpec((ts, D_MODEL), lambda i: (i, 0)),
            _resident((1, D_MODEL)),
            _resident((1, D_MODEL)),
            _resident((D_MODEL, D_PROJ_PACKED)),
            _resident((D_MODEL, D_MODEL)),
            _resident((POOL_WIDTH, POOL_WIDTH)),
            _resident((1, POOL_WIDTH)),
            _resident((8, CONV_WIDTH)),
            _resident((GL_PAD, GLA_KW)),
            _resident((1, GLA_KW)),
            _resident((1, GLA_WIDTH)),
        ],
        out_specs=pl.BlockSpec((ts, D_MODEL), lambda i: (i, 0)),
        out_shape=jax.ShapeDtypeStruct(h.shape, F32),
        scratch_shapes=[
            pltpu.VMEM((ts, D_MODEL), BF16),
            pltpu.VMEM((POOL_HIST + ts, POOL_WIDTH), F32),
            pltpu.VMEM((CONV_HIST + ts, CONV_WIDTH), F32),
            pltpu.VMEM((ts, GLA_KW), F32),
            pltpu.VMEM((ts, GLA_KW), F32),
            pltpu.VMEM((ts, GLA_KW), F32),
            pltpu.VMEM((3 * GLA_HEADS, ts, 128), F32),
            pltpu.VMEM((ts, GLA_KW), BF16),
            pltpu.VMEM((ts, GLA_KW), BF16),
            pltpu.VMEM((3 * GLA_HEADS, ts, 4 * GLA_DK), BF16),
            pltpu.VMEM((3 * GLA_HEADS, ts, 4 * GLA_DK), BF16),
            pltpu.VMEM((ts, GLA_WIDTH), BF16),
            pltpu.VMEM((ts, GLA_WIDTH), F32),
            pltpu.VMEM((ts, GLA_WIDTH), F32),
            pltpu.VMEM((ts, GLA_WIDTH), F32),
            pltpu.VMEM((ts // CHUNK * GLA_HEADS, GLA_DV, GLA_DK), F32),
            pltpu.VMEM((ts // CHUNK * GLA_HEADS, GLA_DV, GLA_DK), BF16),
            pltpu.VMEM((ts, D_MODEL), BF16),
            pltpu.VMEM((GLA_HEADS, GLA_DV, GLA_DK), F32),
            pltpu.VMEM((POOL_HIST, POOL_WIDTH), F32),
            pltpu.VMEM((CONV_HIST, CONV_WIDTH), F32),
            pltpu.VMEM((GLA_HEADS, GLA_DV, GLA_DK), F32),
        ],
        compiler_params=pltpu.CompilerParams(
            dimension_semantics=("arbitrary",), vmem_limit_bytes=VMEM_LIMIT),
        name="mixer",
    )(h, pre, post, win, wout, poolw, pscale, convw, w2, b2, gnorm)


def _pack_w_in(w):
    pool, cb, cc, cu, q, k, v, gl, og = jnp.split(
        w, np.cumsum((256, 256, 256, 256, 256, 256, 512, 16, 512))[:-1].tolist(), axis=-1)
    gl = jnp.pad(gl, ((0, 0), (0, GL_PAD - GLA_GATE_RANK)))
    return jnp.concatenate([pool, cb, cc, cu, q, k, v, og, gl], axis=-1).astype(BF16)


def _block_diag(w):
    out = jnp.zeros((POOL_WIDTH, POOL_WIDTH), w.dtype)
    for g in range(len(POOL_WINDOWS)):
        sl = slice(g * POOL_GROUP, (g + 1) * POOL_GROUP)
        out = out.at[sl, sl].set(w[g])
    return out


def kernel(x, meta, ffn1_pre, ffn1_post, ffn1_wg, ffn1_wu, ffn1_wd, mix_pre, mix_post, w_in,
           pool_w, pool_scale, conv_w, gla_w2, gla_b2, gla_norm, w_out,
           ffn2_pre, ffn2_post, ffn2_wg, ffn2_wu, ffn2_wd):
    n_batch, seq, d = x.shape
    depth = w_in.shape[0]
    assert d == D_MODEL and seq % TILE == 0 and meta.shape[0] == N_META
    meta_tile = jnp.pad(meta.astype(F32), ((META_PAD, 0), (0, 0)))
    h = x.reshape(n_batch * seq, d)
    row = lambda v: v.reshape(1, -1).astype(F32)
    for l in range(depth):
        h = _ffn(h, meta_tile, row(ffn1_pre[l]), row(ffn1_post[l]), ffn1_wg, ffn1_wu, ffn1_wd,
                 mode="first" if l == 0 else "mid", layer=l)
        h = _mixer(
            h, seq // TILE, row(mix_pre[l]), row(mix_post[l]), _pack_w_in(w_in[l]),
            w_out[l].astype(BF16), _block_diag(pool_w[l]).astype(BF16), row(pool_scale[l]),
            jnp.pad(conv_w[l].astype(F32), ((0, 8 - CONV_K), (0, 0))),
            jnp.pad(gla_w2[l], ((0, GL_PAD - GLA_GATE_RANK), (0, 0))).astype(BF16),
            row(gla_b2[l]), row(gla_norm[l]))
        h = _ffn(h, meta_tile, row(ffn2_pre[l]), row(ffn2_post[l]), ffn2_wg, ffn2_wu, ffn2_wd,
                 mode="last" if l == depth - 1 else "mid", layer=l)
    return h.reshape(n_batch, seq, d)
```

```python
import functools

import jax
import jax.numpy as jnp
import numpy as np
from jax import lax
from jax.experimental import pallas as pl
from jax.experimental.pallas import tpu as pltpu

F32 = jnp.float32
BF16 = jnp.bfloat16

D_MODEL = 1024
D_FF = 2816
N_META = 16
EPS = 1e-6
FFN_RESID = 0.5

POOL_WIDTH = 256
POOL_WINDOWS = (2, 4, 8, 16)
POOL_GROUP = 64
CONV_WIDTH = 256
CONV_K = 3
GLA_HEADS = 4
GLA_DK = 64
GLA_DV = 128
GLA_KW = GLA_HEADS * GLA_DK
GLA_WIDTH = GLA_HEADS * GLA_DV
GLA_GATE_RANK = 16
GLA_TAU = 16.0
LOG2_E = 1.4426950408889634
CHUNK = 64


OFF_POOL, OFF_CB, OFF_CC, OFF_CU, OFF_Q, OFF_K = 0, 256, 512, 768, 1024, 1280
OFF_V, OFF_OG, OFF_GL = 1536, 2048, 2560
GL_PAD = 128
D_PROJ_PACKED = OFF_GL + GL_PAD

V7X_VMEM_BYTES = 64 * 1024 * 1024
VMEM_LIMIT = 56 * 1024 * 1024

TILE = 512
FF_CHUNK = 256
FFN_MID_TILE = 768
META_PAD = TILE - N_META
POOL_HIST = 16
CONV_HIST = 8


def _rms(x, g):
    return x * lax.rsqrt(jnp.mean(x * x, axis=-1, keepdims=True) + EPS) * g


def _load_ffn_weights(layer, wg_hbm, wu_hbm, wd_hbm, wg_ref, wu_ref, wd_ref, stage_cols,
                      stage_rows, sems):
    n_chunks = D_FF // FF_CHUNK

    def copies(j, slot):
        off = pl.multiple_of(j * FF_CHUNK, FF_CHUNK)
        return (
            pltpu.make_async_copy(wg_hbm.at[layer, :, pl.ds(off, FF_CHUNK)],
                                  stage_cols.at[0, slot], sems.at[0, slot]),
            pltpu.make_async_copy(wu_hbm.at[layer, :, pl.ds(off, FF_CHUNK)],
                                  stage_cols.at[1, slot], sems.at[1, slot]),
            pltpu.make_async_copy(wd_hbm.at[layer, pl.ds(off, FF_CHUNK), :],
                                  stage_rows.at[slot], sems.at[2, slot]),
        )

    for cp in copies(0, 0):
        cp.start()

    def body(j, carry):
        slot = lax.rem(j, 2)

        @pl.when(j + 1 < n_chunks)
        def _():
            for cp in copies(j + 1, 1 - slot):
                cp.start()

        for cp in copies(j, slot):
            cp.wait()
        off = pl.multiple_of(j * FF_CHUNK, FF_CHUNK)
        wg_ref[:, pl.ds(off, FF_CHUNK)] = stage_cols[0, slot].astype(BF16)
        wu_ref[:, pl.ds(off, FF_CHUNK)] = stage_cols[1, slot].astype(BF16)
        wd_ref[pl.ds(off, FF_CHUNK), :] = stage_rows[slot].astype(BF16)
        return carry

    lax.fori_loop(0, n_chunks, body, 0)


def _ffn_kernel(h_ref, meta_ref, pre_ref, post_ref, wg_hbm, wu_hbm, wd_hbm, o_ref,
                z_ref, wg_ref, wu_ref, wd_ref, stage_cols, stage_rows, sems, *,
                meta_at_step0, layer):
    @pl.when(pl.program_id(0) == 0)
    def _():
        _load_ffn_weights(layer, wg_hbm, wu_hbm, wd_hbm, wg_ref, wu_ref, wd_ref,
                          stage_cols, stage_rows, sems)

    h = h_ref[...]
    if meta_at_step0:
        h = jnp.where(pl.program_id(0) == 0, meta_ref[...], h)
    a = _rms(h, pre_ref[...]).astype(BF16)
    for j in range(D_FF // FF_CHUNK):
        sl = slice(j * FF_CHUNK, (j + 1) * FF_CHUNK)
        g = jnp.dot(a, wg_ref[:, sl], preferred_element_type=F32)
        u = jnp.dot(a, wu_ref[:, sl], preferred_element_type=F32)
        z_ref[:, sl] = (g * jax.nn.sigmoid(g) * u).astype(BF16)
    f = jnp.dot(z_ref[...], wd_ref[...], preferred_element_type=F32)
    o_ref[...] = h + FFN_RESID * _rms(f, post_ref[...])


def _resident(shape):
    return pl.BlockSpec(shape, lambda *_: (0,) * len(shape), pipeline_mode=pl.Buffered(1))


def _ffn(src, meta_tile, pre, post, wg, wu, wd, *, mode, layer):
    tile = FFN_MID_TILE if mode == "mid" else TILE
    n_tiles = src.shape[0] // tile
    assert src.shape[0] % tile == 0
    if mode == "first":
        grid, src_map = n_tiles + 1, lambda i: (jnp.maximum(i - 1, 0), 0)
    elif mode == "mid":
        grid, src_map = n_tiles, lambda i: (i, 0)
    else:
        grid, src_map = n_tiles - 1, lambda i: (i + 1, 0)
    out_rows = grid * tile
    return pl.pallas_call(
        functools.partial(_ffn_kernel, meta_at_step0=(mode == "first"), layer=layer),
        grid=(grid,),
        in_specs=[
            pl.BlockSpec((tile, D_MODEL), src_map),
            _resident((TILE, D_MODEL)),
            _resident((1, D_MODEL)),
            _resident((1, D_MODEL)),
            pl.BlockSpec(memory_space=pl.ANY),
            pl.BlockSpec(memory_space=pl.ANY),
            pl.BlockSpec(memory_space=pl.ANY),
        ],
        out_specs=pl.BlockSpec((tile, D_MODEL), lambda i: (i, 0)),
        out_shape=jax.ShapeDtypeStruct((out_rows, D_MODEL), F32),
        scratch_shapes=[
            pltpu.VMEM((tile, D_FF), BF16),
            pltpu.VMEM((D_MODEL, D_FF), BF16),
            pltpu.VMEM((D_MODEL, D_FF), BF16),
            pltpu.VMEM((D_FF, D_MODEL), BF16),
            pltpu.VMEM((2, 2, D_MODEL, FF_CHUNK), F32),
            pltpu.VMEM((2, FF_CHUNK, D_MODEL), F32),
            pltpu.SemaphoreType.DMA((3, 2)),
        ],
        compiler_params=pltpu.CompilerParams(
            dimension_semantics=("arbitrary",), vmem_limit_bytes=VMEM_LIMIT),
        name="ffn_" + mode,
    )(src, meta_tile, pre, post, wg, wu, wd)


def _mixer_kernel(h_ref, pre_ref, post_ref, win_ref, wout_ref, poolw_ref, pscale_ref,
                  convw_ref, w2_ref, b2_ref, gnorm_ref, o_ref,
                  a_s, pbuf, zbuf, b_s, q_s, k_s, rep_s, qe_s, kd_s, lq_s, lk_s, v_s, gate_s, o_s, i_s, u_s, sb_s, y_s, state_s,
                  pmeta, zmeta, smeta, *, tiles_per_seq):
    step = pl.program_id(0)
    ts = TILE
    is_meta = step == 0

    @pl.when(is_meta)
    def _():
        pbuf[0:POOL_HIST, :] = jnp.zeros((POOL_HIST, POOL_WIDTH), F32)
        zbuf[0:CONV_HIST, :] = jnp.zeros((CONV_HIST, CONV_WIDTH), F32)
        state_s[...] = jnp.zeros_like(state_s)

    @pl.when(jnp.logical_and(step > 0, lax.rem(step - 1, tiles_per_seq) == 0))
    def _():
        pbuf[0:POOL_HIST, :] = pmeta[...]
        zbuf[0:CONV_HIST, :] = zmeta[...]
        state_s[...] = smeta[...]

    nc = ts // CHUNK
    ri = lax.broadcasted_iota(jnp.int32, (CHUNK, CHUNK), 0)
    ci = lax.broadcasted_iota(jnp.int32, (CHUNK, CHUNK), 1)

    def proj(off, width):
        return jnp.dot(a_s[...], win_ref[:, off:off + width], preferred_element_type=F32)

    def pool_mixer():
        x1 = pbuf[...]
        s2 = x1 + pltpu.roll(x1, 1, 0)
        s4 = s2 + pltpu.roll(s2, 2, 0)
        s8 = s4 + pltpu.roll(s4, 4, 0)
        s16 = s8 + pltpu.roll(s8, 8, 0)
        lane = lax.broadcasted_iota(jnp.int32, (ts, POOL_WIDTH), 1)
        row = lax.broadcasted_iota(jnp.int32, (ts, POOL_WIDTH), 0)
        win = jnp.where(lane < 64, 2, jnp.where(lane < 128, 4, jnp.where(lane < 192, 8, 16)))
        wsum = jnp.where(lane < 64, s2[POOL_HIST:], jnp.where(
            lane < 128, s4[POOL_HIST:], jnp.where(lane < 192, s8[POOL_HIST:], s16[POOL_HIST:])))
        pos1 = jnp.where(is_meta, row - (META_PAD - 1), max(POOL_WINDOWS))
        cnt = jnp.clip(pos1, 1, win).astype(F32)
        m = (wsum / cnt - x1[POOL_HIST:]).astype(BF16)
        y_pool = jnp.dot(m, poolw_ref[...], preferred_element_type=F32) * pscale_ref[...]
        y_s[:, 0:POOL_WIDTH] = y_pool.astype(BF16)
        pbuf[0:POOL_HIST, :] = x1[ts:ts + POOL_HIST]

    def conv_mixer():
        zz = zbuf[...]
        z1 = pltpu.roll(zz, 1, 0)
        z2 = pltpu.roll(zz, 2, 0)
        cw = convw_ref[...]
        yc = (cw[0:1, :] * z2[CONV_HIST:] + cw[1:2, :] * z1[CONV_HIST:]
              + cw[2:3, :] * zz[CONV_HIST:])
        y_s[:, POOL_WIDTH:POOL_WIDTH + CONV_WIDTH] = (proj(OFF_CB, CONV_WIDTH) * yc).astype(BF16)
        zbuf[0:CONV_HIST, :] = zz[ts:ts + CONV_HIST]

    @pl.when(step >= 0)
    def _():
        a_s[...] = _rms(h_ref[...], pre_ref[...]).astype(BF16)
        g_low = proj(OFF_GL, GL_PAD).astype(BF16)
        gx = jnp.dot(g_low, w2_ref[...], preferred_element_type=F32) + b2_ref[...]
        log_sig = jnp.minimum(gx, 0.0) - jnp.log(1.0 + jnp.exp(-jnp.abs(gx)))
        la = log_sig * (LOG2_E / GLA_TAU)
        la_hi = la.astype(BF16)
        la_lo = (la - la_hi.astype(F32)).astype(BF16)
        la_cat = jnp.concatenate([la_hi, la_lo], axis=1)
        pbuf[POOL_HIST:, :] = proj(OFF_POOL, POOL_WIDTH)
        zbuf[CONV_HIST:, :] = proj(OFF_CC, CONV_WIDTH) * proj(OFF_CU, CONV_WIDTH)
        tril = jnp.where(ri >= ci, 1.0, 0.0).astype(BF16)
        for c in range(nc):
            rows = slice(c * CHUNK, (c + 1) * CHUNK)
            bb = jnp.dot(tril, la_cat[rows, :], preferred_element_type=F32)
            b_s[rows, :] = bb[:, :GLA_KW] + bb[:, GLA_KW:]
        q_s[...] = proj(OFF_Q, GLA_KW) * (GLA_DK ** -0.5)
        k_s[...] = proj(OFF_K, GLA_KW)
        pool_mixer()
        conv_mixer()
        lane_hi2 = lax.broadcasted_iota(jnp.int32, (1, 128), 1) >= GLA_DK
        for n, src in enumerate((b_s, q_s, k_s)):
            for c2 in range(2):
                x = src[:, 128 * c2:128 * c2 + 128]
                x_sw = pltpu.roll(x, GLA_DK, 1)
                rep_s[GLA_HEADS * n + 2 * c2] = jnp.where(lane_hi2, x_sw, x)
                rep_s[GLA_HEADS * n + 2 * c2 + 1] = jnp.where(lane_hi2, x, x_sw)

    b, q, k = b_s[...], q_s[...], k_s[...]
    b3 = b.reshape(nc, CHUNK, GLA_KW)
    b_last = b3[:, CHUNK - 1:CHUNK, :]
    qe_s[...] = (q.reshape(b3.shape) * jnp.exp2(b3)).reshape(ts, GLA_KW).astype(BF16)
    kd_s[...] = (k.reshape(b3.shape) * jnp.exp2(b_last - b3)).reshape(ts, GLA_KW).astype(BF16)
    decay = jnp.exp2(b_last)

    lane_hi = lax.broadcasted_iota(jnp.int32, (1, 1, 128), 2) >= GLA_DK
    t64 = lax.broadcasted_iota(jnp.int32, (1, CHUNK, 1), 1)
    blk16, blk4, pos4 = t64 // 16, (t64 // 4) % 4, t64 % 4
    in_hi4 = lax.broadcasted_iota(jnp.int32, (1, 8, 1), 1) >= 4
    neg = -1e30

    def build_level_operands(hd):
        rb, rq, rk = (rep_s[GLA_HEADS * n + hd] for n in range(3))
        rb64, rq64, rk64 = (x.reshape(nc, CHUNK, 128) for x in (rb, rq, rk))
        rb16 = rb.reshape(ts // 16, 16, 128)
        rb8 = rb.reshape(ts // 8, 8, 128)
        to64 = lambda x: x.reshape(nc, CHUNK, 128)
        end4 = jnp.where(in_hi4, rb8[:, 7:8, :], rb8[:, 3:4, :])
        k1 = rk64 * to64(jnp.exp2(rb16[:, 15:16, :] - rb16))
        k2 = rk64 * to64(jnp.exp2(end4 - rb8))
        nxt = jnp.where(lane_hi, pltpu.roll(rb8, 7, 1), rb8)
        for p in range(2):
            col = slice(128 * p, 128 * p + 128)
            j = jnp.where(lane_hi, 2 * p + 1, 2 * p)
            ref1 = jnp.where(lane_hi, rb64[:, 32 * p + 31:32 * p + 32, :],
                             rb64[:, 32 * p + 15:32 * p + 16, :])
            ref2 = jnp.where(lane_hi, rb16[:, 8 * p + 7:8 * p + 8, :], rb16[:, 8 * p + 3:8 * p + 4, :])
            ref3 = jnp.where(in_hi4, nxt[:, 4 + 2 * p:5 + 2 * p, :], nxt[:, 2 * p:2 * p + 1, :])
            q1 = rq64 * jnp.exp2(jnp.where(blk16 > j, rb64 - ref1, neg))
            q2 = rq64 * jnp.exp2(jnp.where(blk4 > j, to64(rb16 - ref2), neg))
            q3 = rq64 * jnp.exp2(jnp.where(pos4 >= j, to64(rb8 - ref3), neg))
            for n, (ql, kl) in enumerate(((q1, jnp.where(blk16 == j, k1, 0.0)),
                                          (q2, jnp.where(blk4 == j, k2, 0.0)),
                                          (q3, jnp.where(pos4 == j, rk64, 0.0)))):
                lq_s[3 * hd + n, :, col] = ql.reshape(ts, 128).astype(BF16)
                lk_s[3 * hd + n, :, col] = kl.reshape(ts, 128).astype(BF16)

    def proj_rows(r0, nr, off, width):
        return jnp.dot(a_s[r0:r0 + nr, :], win_ref[:, off:off + width],
                       preferred_element_type=F32)

    quarter = ts // 4
    build_level_operands(0)
    for r0 in range(0, ts, quarter):
        v_s[r0:r0 + quarter, :] = proj_rows(r0, quarter, OFF_V, GLA_WIDTH).astype(BF16)
    build_level_operands(1)
    for r0 in range(0, ts, quarter):
        og = proj_rows(r0, quarter, OFF_OG, GLA_WIDTH)
        gate_s[r0:r0 + quarter, :] = og * jax.nn.sigmoid(og)
    build_level_operands(2)
    build_level_operands(3)

    same16 = (ri // 16) == (ci // 16)
    same4 = (ri // 4) == (ci // 4)
    nt = (((1,), (1,)), ((), ()))
    tn = (((0,), (0,)), ((), ()))
    ksl = [slice(hd * GLA_DK, (hd + 1) * GLA_DK) for hd in range(GLA_HEADS)]
    vsl = [slice(hd * GLA_DV, (hd + 1) * GLA_DV) for hd in range(GLA_HEADS)]
    pairs = [(c, hd) for c in range(nc) for hd in range(GLA_HEADS)]
    group = 2 * GLA_HEADS

    def score_group(g0):
        grp = pairs[g0:g0 + group]
        rws = [slice(c * CHUNK, (c + 1) * CHUNK) for c, _ in grp]
        lvl = [[lax.dot_general(lq_s[3 * hd + n, r, :], lk_s[3 * hd + n, r, :], nt,
                                preferred_element_type=F32) for n in range(3)]
               for r, (c, hd) in zip(rws, grp)]
        us = [lax.dot_general(v_s[r, vsl[hd]], kd_s[r, ksl[hd]], tn, preferred_element_type=F32)
              for r, (c, hd) in zip(rws, grp)]
        ps = [(g1 + jnp.where(same16, g2, 0.0) + jnp.where(same4, g3, 0.0)).astype(BF16)
              for g1, g2, g3 in lvl]
        pvs = [jnp.dot(p, v_s[r, vsl[hd]], preferred_element_type=F32)
               for p, r, (c, hd) in zip(ps, rws, grp)]
        for pv, u, r, (c, hd) in zip(pvs, us, rws, grp):
            o_s[r, vsl[hd]] = pv
            u_s[c * GLA_HEADS + hd] = u

    for g0 in range(0, len(pairs), group):
        score_group(g0)

    sts = [state_s[hd] for hd in range(GLA_HEADS)]
    for c in range(nc):
        for hd in range(GLA_HEADS):
            sb_s[c * GLA_HEADS + hd] = sts[hd].astype(BF16)
        sts = [sts[hd] * decay[c][:, ksl[hd]] + u_s[c * GLA_HEADS + hd]
               for hd in range(GLA_HEADS)]
    for hd in range(GLA_HEADS):
        state_s[hd] = sts[hd]

    for g0 in range(0, len(pairs), group):
        grp = pairs[g0:g0 + group]
        inters = [lax.dot_general(qe_s[c * CHUNK:(c + 1) * CHUNK, ksl[hd]],
                                  sb_s[c * GLA_HEADS + hd], nt, preferred_element_type=F32)
                  for c, hd in grp]
        for inter, (c, hd) in zip(inters, grp):
            i_s[c * CHUNK:(c + 1) * CHUNK, vsl[hd]] = inter

    for hd in range(GLA_HEADS):
        vs = slice(hd * GLA_DV, (hd + 1) * GLA_DV)
        o = o_s[:, vs] + i_s[:, vs]
        o = o * lax.rsqrt(jnp.mean(o * o, axis=-1, keepdims=True) + EPS) * gnorm_ref[:, vs]
        y_s[:, POOL_WIDTH + CONV_WIDTH + hd * GLA_DV:POOL_WIDTH + CONV_WIDTH + (hd + 1) * GLA_DV] = (
            o * gate_s[:, vs]).astype(BF16)

    mix = jnp.dot(y_s[...], wout_ref[...], preferred_element_type=F32)
    o_ref[...] = h_ref[...] + _rms(mix, post_ref[...])

    @pl.when(is_meta)
    def _():
        pmeta[...] = pbuf[0:POOL_HIST, :]
        zmeta[...] = zbuf[0:CONV_HIST, :]
        smeta[...] = state_s[...]


def _mixer(h, tiles_per_seq, pre, post, win, wout, poolw, pscale, convw, w2, b2, gnorm):
    ts = TILE
    return pl.pallas_call(
        functools.partial(_mixer_kernel, tiles_per_seq=tiles_per_seq),
        grid=(h.shape[0] // ts,),
        in_specs=[
            pl.BlockSpec((ts, D_MODEL), lambda i: (i, 0)),
            _resident((1, D_MODEL)),
            _resident((1, D_MODEL)),
            _resident((D_MODEL, D_PROJ_PACKED)),
            _resident((D_MODEL, D_MODEL)),
            _resident((POOL_WIDTH, POOL_WIDTH)),
            _resident((1, POOL_WIDTH)),
            _resident((8, CONV_WIDTH)),
            _resident((GL_PAD, GLA_KW)),
            _resident((1, GLA_KW)),
            _resident((1, GLA_WIDTH)),
        ],
        out_specs=pl.BlockSpec((ts, D_MODEL), lambda i: (i, 0)),
        out_shape=jax.ShapeDtypeStruct(h.shape, F32),
        scratch_shapes=[
            pltpu.VMEM((ts, D_MODEL), BF16),
            pltpu.VMEM((POOL_HIST + ts, POOL_WIDTH), F32),
            pltpu.VMEM((CONV_HIST + ts, CONV_WIDTH), F32),
            pltpu.VMEM((ts, GLA_KW), F32),
            pltpu.VMEM((ts, GLA_KW), F32),
            pltpu.VMEM((ts, GLA_KW), F32),
            pltpu.VMEM((3 * GLA_HEADS, ts, 128), F32),
            pltpu.VMEM((ts, GLA_KW), BF16),
            pltpu.VMEM((ts, GLA_KW), BF16),
            pltpu.VMEM((3 * GLA_HEADS, ts, 4 * GLA_DK), BF16),
            pltpu.VMEM((3 * GLA_HEADS, ts, 4 * GLA_DK), BF16),
            pltpu.VMEM((ts, GLA_WIDTH), BF16),
            pltpu.VMEM((ts, GLA_WIDTH), F32),
            pltpu.VMEM((ts, GLA_WIDTH), F32),
            pltpu.VMEM((ts, GLA_WIDTH), F32),
            pltpu.VMEM((ts // CHUNK * GLA_HEADS, GLA_DV, GLA_DK), F32),
            pltpu.VMEM((ts // CHUNK * GLA_HEADS, GLA_DV, GLA_DK), BF16),
            pltpu.VMEM((ts, D_MODEL), BF16),
            pltpu.VMEM((GLA_HEADS, GLA_DV, GLA_DK), F32),
            pltpu.VMEM((POOL_HIST, POOL_WIDTH), F32),
            pltpu.VMEM((CONV_HIST, CONV_WIDTH), F32),
            pltpu.VMEM((GLA_HEADS, GLA_DV, GLA_DK), F32),
        ],
        compiler_params=pltpu.CompilerParams(
            dimension_semantics=("arbitrary",), vmem_limit_bytes=VMEM_LIMIT),
        name="mixer",
    )(h, pre, post, win, wout, poolw, pscale, convw, w2, b2, gnorm)


def _pack_w_in(w):
    pool, cb, cc, cu, q, k, v, gl, og = jnp.split(
        w, np.cumsum((256, 256, 256, 256, 256, 256, 512, 16, 512))[:-1].tolist(), axis=-1)
    gl = jnp.pad(gl, ((0, 0), (0, GL_PAD - GLA_GATE_RANK)))
    return jnp.concatenate([pool, cb, cc, cu, q, k, v, og, gl], axis=-1).astype(BF16)


def _block_diag(w):
    out = jnp.zeros((POOL_WIDTH, POOL_WIDTH), w.dtype)
    for g in range(len(POOL_WINDOWS)):
        sl = slice(g * POOL_GROUP, (g + 1) * POOL_GROUP)
        out = out.at[sl, sl].set(w[g])
    return out


def kernel(x, meta, ffn1_pre, ffn1_post, ffn1_wg, ffn1_wu, ffn1_wd, mix_pre, mix_post, w_in,
           pool_w, pool_scale, conv_w, gla_w2, gla_b2, gla_norm, w_out,
           ffn2_pre, ffn2_post, ffn2_wg, ffn2_wu, ffn2_wd):
    n_batch, seq, d = x.shape
    depth = w_in.shape[0]
    assert d == D_MODEL and seq % TILE == 0 and meta.shape[0] == N_META
    meta_tile = jnp.pad(meta.astype(F32), ((META_PAD, 0), (0, 0)))
    h = x.reshape(n_batch * seq, d)
    row = lambda v: v.reshape(1, -1).astype(F32)
    for l in range(depth):
        h = _ffn(h, meta_tile, row(ffn1_pre[l]), row(ffn1_post[l]), ffn1_wg, ffn1_wu, ffn1_wd,
                 mode="first" if l == 0 else "mid", layer=l)
        h = _mixer(
            h, seq // TILE, row(mix_pre[l]), row(mix_post[l]), _pack_w_in(w_in[l]),
            w_out[l].astype(BF16), _block_diag(pool_w[l]).astype(BF16), row(pool_scale[l]),
            jnp.pad(conv_w[l].astype(F32), ((0, 8 - CONV_K), (0, 0))),
            jnp.pad(gla_w2[l], ((0, GL_PAD - GLA_GATE_RANK), (0, 0))).astype(BF16),
            row(gla_b2[l]), row(gla_norm[l]))
        h = _ffn(h, meta_tile, row(ffn2_pre[l]), row(ffn2_post[l]), ffn2_wg, ffn2_wu, ffn2_wd,
                 mode="last" if l == depth - 1 else "mid", layer=l)
    return h.reshape(n_batch, seq, d)
```

```python
import functools

import jax
import jax.numpy as jnp
import numpy as np
from jax import lax
from jax.experimental import pallas as pl
from jax.experimental.pallas import tpu as pltpu

F32 = jnp.float32
BF16 = jnp.bfloat16

D_MODEL = 1024
D_FF = 2816
N_META = 16
EPS = 1e-6
FFN_RESID = 0.5

POOL_WIDTH = 256
POOL_WINDOWS = (2, 4, 8, 16)
POOL_GROUP = 64
CONV_WIDTH = 256
CONV_K = 3
GLA_HEADS = 4
GLA_DK = 64
GLA_DV = 128
GLA_KW = GLA_HEADS * GLA_DK
GLA_WIDTH = GLA_HEADS * GLA_DV
GLA_GATE_RANK = 16
GLA_TAU = 16.0
LOG2_E = 1.4426950408889634
SINGLE_REF_MAX_LOG2 = 100.0
CHUNK = 64


OFF_POOL, OFF_CB, OFF_CC, OFF_CU, OFF_Q, OFF_K = 0, 256, 512, 768, 1024, 1280
OFF_V, OFF_OG, OFF_GL = 1536, 2048, 2560
GL_PAD = 128
D_PROJ_PACKED = OFF_GL + GL_PAD

V7X_VMEM_BYTES = 64 * 1024 * 1024
VMEM_LIMIT = 56 * 1024 * 1024

TILE = 512
FF_CHUNK = 256
META_PAD = TILE - N_META
POOL_HIST = 16
CONV_HIST = 8


def _rms(x, g):
    return x * lax.rsqrt(jnp.mean(x * x, axis=-1, keepdims=True) + EPS) * g


def _load_ffn_weights(layer, wg_hbm, wu_hbm, wd_hbm, wg_ref, wu_ref, wd_ref, stage_cols,
                      stage_rows, sems):
    n_chunks = D_FF // FF_CHUNK

    def copies(j, slot):
        off = pl.multiple_of(j * FF_CHUNK, FF_CHUNK)
        return (
            pltpu.make_async_copy(wg_hbm.at[layer, :, pl.ds(off, FF_CHUNK)],
                                  stage_cols.at[0, slot], sems.at[0, slot]),
            pltpu.make_async_copy(wu_hbm.at[layer, :, pl.ds(off, FF_CHUNK)],
                                  stage_cols.at[1, slot], sems.at[1, slot]),
            pltpu.make_async_copy(wd_hbm.at[layer, pl.ds(off, FF_CHUNK), :],
                                  stage_rows.at[slot], sems.at[2, slot]),
        )

    for cp in copies(0, 0):
        cp.start()

    def body(j, carry):
        slot = lax.rem(j, 2)

        @pl.when(j + 1 < n_chunks)
        def _():
            for cp in copies(j + 1, 1 - slot):
                cp.start()

        for cp in copies(j, slot):
            cp.wait()
        off = pl.multiple_of(j * FF_CHUNK, FF_CHUNK)
        wg_ref[:, pl.ds(off, FF_CHUNK)] = stage_cols[0, slot].astype(BF16)
        wu_ref[:, pl.ds(off, FF_CHUNK)] = stage_cols[1, slot].astype(BF16)
        wd_ref[pl.ds(off, FF_CHUNK), :] = stage_rows[slot].astype(BF16)
        return carry

    lax.fori_loop(0, n_chunks, body, 0)


def _ffn_kernel(h_ref, meta_ref, pre_ref, post_ref, wg_hbm, wu_hbm, wd_hbm, o_ref,
                z_ref, wg_ref, wu_ref, wd_ref, stage_cols, stage_rows, sems, *,
                meta_at_step0, layer):
    @pl.when(pl.program_id(0) == 0)
    def _():
        _load_ffn_weights(layer, wg_hbm, wu_hbm, wd_hbm, wg_ref, wu_ref, wd_ref,
                          stage_cols, stage_rows, sems)

    h = h_ref[...]
    if meta_at_step0:
        h = jnp.where(pl.program_id(0) == 0, meta_ref[...], h)
    a = _rms(h, pre_ref[...]).astype(BF16)
    for j in range(D_FF // FF_CHUNK):
        sl = slice(j * FF_CHUNK, (j + 1) * FF_CHUNK)
        g = jnp.dot(a, wg_ref[:, sl], preferred_element_type=F32)
        u = jnp.dot(a, wu_ref[:, sl], preferred_element_type=F32)
        z_ref[:, sl] = (g * jax.nn.sigmoid(g) * u).astype(BF16)
    f = jnp.dot(z_ref[...], wd_ref[...], preferred_element_type=F32)
    o_ref[...] = h + FFN_RESID * _rms(f, post_ref[...])


def _resident(shape):
    return pl.BlockSpec(shape, lambda *_: (0,) * len(shape), pipeline_mode=pl.Buffered(1))


def _ffn(src, meta_tile, pre, post, wg, wu, wd, *, mode, layer):
    tile = TILE
    n_tiles = src.shape[0] // tile
    assert src.shape[0] % tile == 0
    if mode == "first":
        grid, src_map = n_tiles + 1, lambda i: (jnp.maximum(i - 1, 0), 0)
    elif mode == "mid":
        grid, src_map = n_tiles, lambda i: (i, 0)
    else:
        grid, src_map = n_tiles - 1, lambda i: (i + 1, 0)
    out_rows = grid * tile
    return pl.pallas_call(
        functools.partial(_ffn_kernel, meta_at_step0=(mode == "first"), layer=layer),
        grid=(grid,),
        in_specs=[
            pl.BlockSpec((tile, D_MODEL), src_map),
            _resident((TILE, D_MODEL)),
            _resident((1, D_MODEL)),
            _resident((1, D_MODEL)),
            pl.BlockSpec(memory_space=pl.ANY),
            pl.BlockSpec(memory_space=pl.ANY),
            pl.BlockSpec(memory_space=pl.ANY),
        ],
        out_specs=pl.BlockSpec((tile, D_MODEL), lambda i: (i, 0)),
        out_shape=jax.ShapeDtypeStruct((out_rows, D_MODEL), F32),
        scratch_shapes=[
            pltpu.VMEM((tile, D_FF), BF16),
            pltpu.VMEM((D_MODEL, D_FF), BF16),
            pltpu.VMEM((D_MODEL, D_FF), BF16),
            pltpu.VMEM((D_FF, D_MODEL), BF16),
            pltpu.VMEM((2, 2, D_MODEL, FF_CHUNK), F32),
            pltpu.VMEM((2, FF_CHUNK, D_MODEL), F32),
            pltpu.SemaphoreType.DMA((3, 2)),
        ],
        compiler_params=pltpu.CompilerParams(
            dimension_semantics=("arbitrary",), vmem_limit_bytes=VMEM_LIMIT),
        name="ffn_" + mode,
    )(src, meta_tile, pre, post, wg, wu, wd)


def _mixer_kernel(h_ref, pre_ref, post_ref, win_ref, wout_ref, poolw_ref, pscale_ref,
                  convw_ref, w2_ref, b2_ref, gnorm_ref, o_ref,
                  a_s, pbuf, zbuf, b_s, q_s, k_s, rep_s, qe_s, kd_s, lq_s, lk_s, v_s, gate_s, o_s, i_s, u_s, sb_s, y_s, state_s,
                  pmeta, zmeta, smeta, *, tiles_per_seq):
    step = pl.program_id(0)
    ts = TILE
    is_meta = step == 0

    @pl.when(is_meta)
    def _():
        pbuf[0:POOL_HIST, :] = jnp.zeros((POOL_HIST, POOL_WIDTH), F32)
        zbuf[0:CONV_HIST, :] = jnp.zeros((CONV_HIST, CONV_WIDTH), F32)
        state_s[...] = jnp.zeros_like(state_s)

    @pl.when(jnp.logical_and(step > 0, lax.rem(step - 1, tiles_per_seq) == 0))
    def _():
        pbuf[0:POOL_HIST, :] = pmeta[...]
        zbuf[0:CONV_HIST, :] = zmeta[...]
        state_s[...] = smeta[...]

    nc = ts // CHUNK
    ri = lax.broadcasted_iota(jnp.int32, (CHUNK, CHUNK), 0)
    ci = lax.broadcasted_iota(jnp.int32, (CHUNK, CHUNK), 1)

    def proj(off, width):
        return jnp.dot(a_s[...], win_ref[:, off:off + width], preferred_element_type=F32)

    def pool_mixer():
        x1 = pbuf[...]
        s2 = x1 + pltpu.roll(x1, 1, 0)
        s4 = s2 + pltpu.roll(s2, 2, 0)
        s8 = s4 + pltpu.roll(s4, 4, 0)
        s16 = s8 + pltpu.roll(s8, 8, 0)
        lane = lax.broadcasted_iota(jnp.int32, (ts, POOL_WIDTH), 1)
        row = lax.broadcasted_iota(jnp.int32, (ts, POOL_WIDTH), 0)
        win = jnp.where(lane < 64, 2, jnp.where(lane < 128, 4, jnp.where(lane < 192, 8, 16)))
        wsum = jnp.where(lane < 64, s2[POOL_HIST:], jnp.where(
            lane < 128, s4[POOL_HIST:], jnp.where(lane < 192, s8[POOL_HIST:], s16[POOL_HIST:])))
        pos1 = jnp.where(is_meta, row - (META_PAD - 1), max(POOL_WINDOWS))
        cnt = jnp.clip(pos1, 1, win).astype(F32)
        m = (wsum / cnt - x1[POOL_HIST:]).astype(BF16)
        y_pool = jnp.dot(m, poolw_ref[...], preferred_element_type=F32) * pscale_ref[...]
        y_s[:, 0:POOL_WIDTH] = y_pool.astype(BF16)
        pbuf[0:POOL_HIST, :] = x1[ts:ts + POOL_HIST]

    def conv_mixer():
        zz = zbuf[...]
        z1 = pltpu.roll(zz, 1, 0)
        z2 = pltpu.roll(zz, 2, 0)
        cw = convw_ref[...]
        yc = (cw[0:1, :] * z2[CONV_HIST:] + cw[1:2, :] * z1[CONV_HIST:]
              + cw[2:3, :] * zz[CONV_HIST:])
        y_s[:, POOL_WIDTH:POOL_WIDTH + CONV_WIDTH] = (proj(OFF_CB, CONV_WIDTH) * yc).astype(BF16)
        zbuf[0:CONV_HIST, :] = zz[ts:ts + CONV_HIST]

    @pl.when(step >= 0)
    def _():
        a_s[...] = _rms(h_ref[...], pre_ref[...]).astype(BF16)
        g_low = proj(OFF_GL, GL_PAD).astype(BF16)
        gx = jnp.dot(g_low, w2_ref[...], preferred_element_type=F32) + b2_ref[...]
        log_sig = jnp.minimum(gx, 0.0) - jnp.log(1.0 + jnp.exp(-jnp.abs(gx)))
        la = log_sig * (LOG2_E / GLA_TAU)
        la_hi = la.astype(BF16)
        la_lo = (la - la_hi.astype(F32)).astype(BF16)
        la_cat = jnp.concatenate([la_hi, la_lo], axis=1)
        pbuf[POOL_HIST:, :] = proj(OFF_POOL, POOL_WIDTH)
        zbuf[CONV_HIST:, :] = proj(OFF_CC, CONV_WIDTH) * proj(OFF_CU, CONV_WIDTH)
        tril = jnp.where(ri >= ci, 1.0, 0.0).astype(BF16)
        for c in range(nc):
            rows = slice(c * CHUNK, (c + 1) * CHUNK)
            bb = jnp.dot(tril, la_cat[rows, :], preferred_element_type=F32)
            b_s[rows, :] = bb[:, :GLA_KW] + bb[:, GLA_KW:]
        q_s[...] = proj(OFF_Q, GLA_KW) * (GLA_DK ** -0.5)
        k_s[...] = proj(OFF_K, GLA_KW)
        pool_mixer()
        conv_mixer()
        v_s[...] = proj(OFF_V, GLA_WIDTH).astype(BF16)
        og = proj(OFF_OG, GLA_WIDTH)
        gate_s[...] = og * jax.nn.sigmoid(og)

    b, q, k = b_s[...], q_s[...], k_s[...]
    b3 = b.reshape(nc, CHUNK, GLA_KW)
    b_last = b3[:, CHUNK - 1:CHUNK, :]
    qe_s[...] = (q.reshape(b3.shape) * jnp.exp2(b3)).reshape(ts, GLA_KW).astype(BF16)
    kd_s[...] = (k.reshape(b3.shape) * jnp.exp2(b_last - b3)).reshape(ts, GLA_KW).astype(BF16)
    decay = jnp.exp2(b_last)

    nt = (((1,), (1,)), ((), ()))
    tn = (((0,), (0,)), ((), ()))
    ksl = [slice(hd * GLA_DK, (hd + 1) * GLA_DK) for hd in range(GLA_HEADS)]
    vsl = [slice(hd * GLA_DV, (hd + 1) * GLA_DV) for hd in range(GLA_HEADS)]
    pairs = [(c, hd) for c in range(nc) for hd in range(GLA_HEADS)]
    group = 2 * GLA_HEADS
    chunk_rows = lambda c: slice(c * CHUNK, (c + 1) * CHUNK)

    single_ref_ok = jnp.min(b) > -SINGLE_REF_MAX_LOG2

    def state_increments(grp):
        return [lax.dot_general(v_s[chunk_rows(c), vsl[hd]], kd_s[chunk_rows(c), ksl[hd]], tn,
                                preferred_element_type=F32) for c, hd in grp]

    def finish_group(grp, ps, us):
        pvs = [jnp.dot(p, v_s[chunk_rows(c), vsl[hd]], preferred_element_type=F32)
               for p, (c, hd) in zip(ps, grp)]
        for pv, u, (c, hd) in zip(pvs, us, grp):
            o_s[chunk_rows(c), vsl[hd]] = pv
            u_s[c * GLA_HEADS + hd] = u

    @pl.when(single_ref_ok)
    def _():
        lk_s[0] = (k.reshape(b3.shape) * jnp.exp2(-b3)).reshape(ts, GLA_KW).astype(BF16)
        causal = ri >= ci
        for g0 in range(0, len(pairs), group):
            grp = pairs[g0:g0 + group]
            scs = [lax.dot_general(qe_s[chunk_rows(c), ksl[hd]], lk_s[0, chunk_rows(c), ksl[hd]],
                                   nt, preferred_element_type=F32) for c, hd in grp]
            us = state_increments(grp)
            finish_group(grp, [jnp.where(causal, sc, 0.0).astype(BF16) for sc in scs], us)

    lane_hi = lax.broadcasted_iota(jnp.int32, (1, 1, 128), 2) >= GLA_DK
    t64 = lax.broadcasted_iota(jnp.int32, (1, CHUNK, 1), 1)
    blk16, blk4, pos4 = t64 // 16, (t64 // 4) % 4, t64 % 4
    in_hi4 = lax.broadcasted_iota(jnp.int32, (1, 8, 1), 1) >= 4
    neg = -1e30

    def replicate_heads():
        for n, src in enumerate((b_s, q_s, k_s)):
            for c2 in range(2):
                x = src[:, 128 * c2:128 * c2 + 128]
                x_sw = pltpu.roll(x, GLA_DK, 1)
                rep_s[GLA_HEADS * n + 2 * c2] = jnp.where(lane_hi[0], x_sw, x)
                rep_s[GLA_HEADS * n + 2 * c2 + 1] = jnp.where(lane_hi[0], x, x_sw)

    def build_level_operands(hd):
        rb, rq, rk = (rep_s[GLA_HEADS * n + hd] for n in range(3))
        rb64, rq64, rk64 = (x.reshape(nc, CHUNK, 128) for x in (rb, rq, rk))
        rb16 = rb.reshape(ts // 16, 16, 128)
        rb8 = rb.reshape(ts // 8, 8, 128)
        to64 = lambda x: x.reshape(nc, CHUNK, 128)
        end4 = jnp.where(in_hi4, rb8[:, 7:8, :], rb8[:, 3:4, :])
        k1 = rk64 * to64(jnp.exp2(rb16[:, 15:16, :] - rb16))
        k2 = rk64 * to64(jnp.exp2(end4 - rb8))
        nxt = jnp.where(lane_hi, pltpu.roll(rb8, 7, 1), rb8)
        for p in range(2):
            col = slice(128 * p, 128 * p + 128)
            j = jnp.where(lane_hi, 2 * p + 1, 2 * p)
            ref1 = jnp.where(lane_hi, rb64[:, 32 * p + 31:32 * p + 32, :],
                             rb64[:, 32 * p + 15:32 * p + 16, :])
            ref2 = jnp.where(lane_hi, rb16[:, 8 * p + 7:8 * p + 8, :], rb16[:, 8 * p + 3:8 * p + 4, :])
            ref3 = jnp.where(in_hi4, nxt[:, 4 + 2 * p:5 + 2 * p, :], nxt[:, 2 * p:2 * p + 1, :])
            q1 = rq64 * jnp.exp2(jnp.where(blk16 > j, rb64 - ref1, neg))
            q2 = rq64 * jnp.exp2(jnp.where(blk4 > j, to64(rb16 - ref2), neg))
            q3 = rq64 * jnp.exp2(jnp.where(pos4 >= j, to64(rb8 - ref3), neg))
            for n, (ql, kl) in enumerate(((q1, jnp.where(blk16 == j, k1, 0.0)),
                                          (q2, jnp.where(blk4 == j, k2, 0.0)),
                                          (q3, jnp.where(pos4 == j, rk64, 0.0)))):
                lq_s[3 * hd + n, :, col] = ql.reshape(ts, 128).astype(BF16)
                lk_s[3 * hd + n, :, col] = kl.reshape(ts, 128).astype(BF16)

    @pl.when(jnp.logical_not(single_ref_ok))
    def _():
        replicate_heads()
        for hd in range(GLA_HEADS):
            build_level_operands(hd)
        same16 = (ri // 16) == (ci // 16)
        same4 = (ri // 4) == (ci // 4)
        for g0 in range(0, len(pairs), group):
            grp = pairs[g0:g0 + group]
            lvl = [[lax.dot_general(lq_s[3 * hd + n, chunk_rows(c), :],
                                    lk_s[3 * hd + n, chunk_rows(c), :], nt,
                                    preferred_element_type=F32) for n in range(3)]
                   for c, hd in grp]
            us = state_increments(grp)
            finish_group(grp, [(g1 + jnp.where(same16, g2, 0.0)
                                + jnp.where(same4, g3, 0.0)).astype(BF16) for g1, g2, g3 in lvl], us)


    sts = [state_s[hd] for hd in range(GLA_HEADS)]
    for c in range(nc):
        for hd in range(GLA_HEADS):
            sb_s[c * GLA_HEADS + hd] = sts[hd].astype(BF16)
        sts = [sts[hd] * decay[c][:, ksl[hd]] + u_s[c * GLA_HEADS + hd]
               for hd in range(GLA_HEADS)]
    for hd in range(GLA_HEADS):
        state_s[hd] = sts[hd]

    for g0 in range(0, len(pairs), group):
        grp = pairs[g0:g0 + group]
        inters = [lax.dot_general(qe_s[c * CHUNK:(c + 1) * CHUNK, ksl[hd]],
                                  sb_s[c * GLA_HEADS + hd], nt, preferred_element_type=F32)
                  for c, hd in grp]
        for inter, (c, hd) in zip(inters, grp):
            i_s[c * CHUNK:(c + 1) * CHUNK, vsl[hd]] = inter

    for hd in range(GLA_HEADS):
        vs = slice(hd * GLA_DV, (hd + 1) * GLA_DV)
        o = o_s[:, vs] + i_s[:, vs]
        o = o * lax.rsqrt(jnp.mean(o * o, axis=-1, keepdims=True) + EPS) * gnorm_ref[:, vs]
        y_s[:, POOL_WIDTH + CONV_WIDTH + hd * GLA_DV:POOL_WIDTH + CONV_WIDTH + (hd + 1) * GLA_DV] = (
            o * gate_s[:, vs]).astype(BF16)

    mix = jnp.dot(y_s[...], wout_ref[...], preferred_element_type=F32)
    o_ref[...] = h_ref[...] + _rms(mix, post_ref[...])

    @pl.when(is_meta)
    def _():
        pmeta[...] = pbuf[0:POOL_HIST, :]
        zmeta[...] = zbuf[0:CONV_HIST, :]
        smeta[...] = state_s[...]


def _mixer(h, tiles_per_seq, pre, post, win, wout, poolw, pscale, convw, w2, b2, gnorm):
    ts = TILE
    return pl.pallas_call(
        functools.partial(_mixer_kernel, tiles_per_seq=tiles_per_seq),
        grid=(h.shape[0] // ts,),
        in_specs=[
            pl.BlockSpec((ts, D_MODEL), lambda i: (i, 0)),
            _resident((1, D_MODEL)),
            _resident((1, D_MODEL)),
            _resident((D_MODEL, D_PROJ_PACKED)),
            _resident((D_MODEL, D_MODEL)),
            _resident((POOL_WIDTH, POOL_WIDTH)),
            _resident((1, POOL_WIDTH)),
            _resident((8, CONV_WIDTH)),
            _resident((GL_PAD, GLA_KW)),
            _resident((1, GLA_KW)),
            _resident((1, GLA_WIDTH)),
        ],
        out_specs=pl.BlockSpec((ts, D_MODEL), lambda i: (i, 0)),
        out_shape=jax.ShapeDtypeStruct(h.shape, F32),
        scratch_shapes=[
            pltpu.VMEM((ts, D_MODEL), BF16),
            pltpu.VMEM((POOL_HIST + ts, POOL_WIDTH), F32),
            pltpu.VMEM((CONV_HIST + ts, CONV_WIDTH), F32),
            pltpu.VMEM((ts, GLA_KW), F32),
            pltpu.VMEM((ts, GLA_KW), F32),
            pltpu.VMEM((ts, GLA_KW), F32),
            pltpu.VMEM((3 * GLA_HEADS, ts, 128), F32),
            pltpu.VMEM((ts, GLA_KW), BF16),
            pltpu.VMEM((ts, GLA_KW), BF16),
            pltpu.VMEM((3 * GLA_HEADS, ts, 4 * GLA_DK), BF16),
            pltpu.VMEM((3 * GLA_HEADS, ts, 4 * GLA_DK), BF16),
            pltpu.VMEM((ts, GLA_WIDTH), BF16),
            pltpu.VMEM((ts, GLA_WIDTH), F32),
            pltpu.VMEM((ts, GLA_WIDTH), F32),
            pltpu.VMEM((ts, GLA_WIDTH), F32),
            pltpu.VMEM((ts // CHUNK * GLA_HEADS, GLA_DV, GLA_DK), F32),
            pltpu.VMEM((ts // CHUNK * GLA_HEADS, GLA_DV, GLA_DK), BF16),
            pltpu.VMEM((ts, D_MODEL), BF16),
            pltpu.VMEM((GLA_HEADS, GLA_DV, GLA_DK), F32),
            pltpu.VMEM((POOL_HIST, POOL_WIDTH), F32),
            pltpu.VMEM((CONV_HIST, CONV_WIDTH), F32),
            pltpu.VMEM((GLA_HEADS, GLA_DV, GLA_DK), F32),
        ],
        compiler_params=pltpu.CompilerParams(
            dimension_semantics=("arbitrary",), vmem_limit_bytes=VMEM_LIMIT),
        name="mixer",
    )(h, pre, post, win, wout, poolw, pscale, convw, w2, b2, gnorm)


def _pack_w_in(w):
    pool, cb, cc, cu, q, k, v, gl, og = jnp.split(
        w, np.cumsum((256, 256, 256, 256, 256, 256, 512, 16, 512))[:-1].tolist(), axis=-1)
    gl = jnp.pad(gl, ((0, 0), (0, GL_PAD - GLA_GATE_RANK)))
    return jnp.concatenate([pool, cb, cc, cu, q, k, v, og, gl], axis=-1).astype(BF16)


def _block_diag(w):
    out = jnp.zeros((POOL_WIDTH, POOL_WIDTH), w.dtype)
    for g in range(len(POOL_WINDOWS)):
        sl = slice(g * POOL_GROUP, (g + 1) * POOL_GROUP)
        out = out.at[sl, sl].set(w[g])
    return out


def kernel(x, meta, ffn1_pre, ffn1_post, ffn1_wg, ffn1_wu, ffn1_wd, mix_pre, mix_post, w_in,
           pool_w, pool_scale, conv_w, gla_w2, gla_b2, gla_norm, w_out,
           ffn2_pre, ffn2_post, ffn2_wg, ffn2_wu, ffn2_wd):
    n_batch, seq, d = x.shape
    depth = w_in.shape[0]
    assert d == D_MODEL and seq % TILE == 0 and meta.shape[0] == N_META
    meta_tile = jnp.pad(meta.astype(F32), ((META_PAD, 0), (0, 0)))
    h = x.reshape(n_batch * seq, d)
    row = lambda v: v.reshape(1, -1).astype(F32)
    for l in range(depth):
        h = _ffn(h, meta_tile, row(ffn1_pre[l]), row(ffn1_post[l]), ffn1_wg, ffn1_wu, ffn1_wd,
                 mode="first" if l == 0 else "mid", layer=l)
        h = _mixer(
            h, seq // TILE, row(mix_pre[l]), row(mix_post[l]), _pack_w_in(w_in[l]),
            w_out[l].astype(BF16), _block_diag(pool_w[l]).astype(BF16), row(pool_scale[l]),
            jnp.pad(conv_w[l].astype(F32), ((0, 8 - CONV_K), (0, 0))),
            jnp.pad(gla_w2[l], ((0, GL_PAD - GLA_GATE_RANK), (0, 0))).astype(BF16),
            row(gla_b2[l]), row(gla_norm[l]))
        h = _ffn(h, meta_tile, row(ffn2_pre[l]), row(ffn2_post[l]), ffn2_wg, ffn2_wu, ffn2_wd,
                 mode="last" if l == depth - 1 else "mid", layer=l)
    return h.reshape(n_batch, seq, d)
```

```python
import functools

import jax
import jax.numpy as jnp
from jax import lax
from jax.experimental import pallas as pl
from jax.experimental.pallas import tpu as pltpu

F32 = jnp.float32
BF16 = jnp.bfloat16

D_MODEL = 1024
D_FF = 2816
N_META = 16
EPS = 1e-6
FFN_RESID = 0.5

POOL_WIDTH = 256
POOL_WINDOWS = (2, 4, 8, 16)
POOL_GROUP = 64
CONV_WIDTH = 256
CONV_K = 3
GLA_HEADS = 4
GLA_DK = 64
GLA_DV = 128
GLA_KW = GLA_HEADS * GLA_DK
GLA_WIDTH = GLA_HEADS * GLA_DV
GLA_GATE_RANK = 16
GLA_TAU = 16.0
LOG2_E = 1.4426950408889634
SINGLE_REF_MAX_LOG2 = 100.0
CHUNK = 64


OFF_POOL, OFF_CB, OFF_CC, OFF_CU, OFF_Q, OFF_K = 0, 256, 512, 768, 1024, 1280
OFF_V, OFF_OG, OFF_GL = 1536, 2048, 2560
GL_PAD = 128
D_PROJ_PACKED = OFF_GL + GL_PAD

V7X_VMEM_BYTES = 64 * 1024 * 1024
VMEM_LIMIT = 56 * 1024 * 1024

TILE = 512
FF_CHUNK = 256
META_PAD = TILE - N_META
POOL_HIST = 16
CONV_HIST = 8


def _rms(x, g):
    return x * lax.rsqrt(jnp.mean(x * x, axis=-1, keepdims=True) + EPS) * g


def _load_ffn_weights(layer, wg_hbm, wu_hbm, wd_hbm, wg_ref, wu_ref, wd_ref, stage_cols,
                      stage_rows, sems):
    n_chunks = D_FF // FF_CHUNK

    def copies(j, slot):
        off = pl.multiple_of(j * FF_CHUNK, FF_CHUNK)
        return (
            pltpu.make_async_copy(wg_hbm.at[layer, :, pl.ds(off, FF_CHUNK)],
                                  stage_cols.at[0, slot], sems.at[0, slot]),
            pltpu.make_async_copy(wu_hbm.at[layer, :, pl.ds(off, FF_CHUNK)],
                                  stage_cols.at[1, slot], sems.at[1, slot]),
            pltpu.make_async_copy(wd_hbm.at[layer, pl.ds(off, FF_CHUNK), :],
                                  stage_rows.at[slot], sems.at[2, slot]),
        )

    for cp in copies(0, 0):
        cp.start()

    def body(j, carry):
        slot = lax.rem(j, 2)

        @pl.when(j + 1 < n_chunks)
        def _():
            for cp in copies(j + 1, 1 - slot):
                cp.start()

        for cp in copies(j, slot):
            cp.wait()
        off = pl.multiple_of(j * FF_CHUNK, FF_CHUNK)
        wg_ref[:, pl.ds(off, FF_CHUNK)] = stage_cols[0, slot].astype(BF16)
        wu_ref[:, pl.ds(off, FF_CHUNK)] = stage_cols[1, slot].astype(BF16)
        wd_ref[pl.ds(off, FF_CHUNK), :] = stage_rows[slot].astype(BF16)
        return carry

    lax.fori_loop(0, n_chunks, body, 0)


def _ffn_kernel(h_ref, meta_ref, pre_ref, post_ref, wg_hbm, wu_hbm, wd_hbm, o_ref,
                z_ref, wg_ref, wu_ref, wd_ref, stage_cols, stage_rows, sems, *,
                meta_at_step0, layer):
    @pl.when(pl.program_id(0) == 0)
    def _():
        _load_ffn_weights(layer, wg_hbm, wu_hbm, wd_hbm, wg_ref, wu_ref, wd_ref,
                          stage_cols, stage_rows, sems)

    h = h_ref[...]
    if meta_at_step0:
        h = jnp.where(pl.program_id(0) == 0, meta_ref[...], h)
    a = _rms(h, pre_ref[...]).astype(BF16)
    for j in range(D_FF // FF_CHUNK):
        sl = slice(j * FF_CHUNK, (j + 1) * FF_CHUNK)
        g = jnp.dot(a, wg_ref[:, sl], preferred_element_type=F32)
        u = jnp.dot(a, wu_ref[:, sl], preferred_element_type=F32)
        z_ref[:, sl] = (g * jax.nn.sigmoid(g) * u).astype(BF16)
    f = jnp.dot(z_ref[...], wd_ref[...], preferred_element_type=F32)
    o_ref[...] = h + FFN_RESID * _rms(f, post_ref[...])


def _resident(shape):
    return pl.BlockSpec(shape, lambda *_: (0,) * len(shape), pipeline_mode=pl.Buffered(1))


def _ffn(src, meta_tile, pre, post, wg, wu, wd, *, mode, layer):
    tile = TILE
    n_tiles = src.shape[0] // tile
    assert src.shape[0] % tile == 0
    if mode == "first":
        grid, src_map = n_tiles + 1, lambda i: (jnp.maximum(i - 1, 0), 0)
    elif mode == "mid":
        grid, src_map = n_tiles, lambda i: (i, 0)
    else:
        grid, src_map = n_tiles - 1, lambda i: (i + 1, 0)
    out_rows = grid * tile
    return pl.pallas_call(
        functools.partial(_ffn_kernel, meta_at_step0=(mode == "first"), layer=layer),
        grid=(grid,),
        in_specs=[
            pl.BlockSpec((tile, D_MODEL), src_map),
            _resident((TILE, D_MODEL)),
            _resident((1, D_MODEL)),
            _resident((1, D_MODEL)),
            pl.BlockSpec(memory_space=pl.ANY),
            pl.BlockSpec(memory_space=pl.ANY),
            pl.BlockSpec(memory_space=pl.ANY),
        ],
        out_specs=pl.BlockSpec((tile, D_MODEL), lambda i: (i, 0)),
        out_shape=jax.ShapeDtypeStruct((out_rows, D_MODEL), F32),
        scratch_shapes=[
            pltpu.VMEM((tile, D_FF), BF16),
            pltpu.VMEM((D_MODEL, D_FF), BF16),
            pltpu.VMEM((D_MODEL, D_FF), BF16),
            pltpu.VMEM((D_FF, D_MODEL), BF16),
            pltpu.VMEM((2, 2, D_MODEL, FF_CHUNK), F32),
            pltpu.VMEM((2, FF_CHUNK, D_MODEL), F32),
            pltpu.SemaphoreType.DMA((3, 2)),
        ],
        compiler_params=pltpu.CompilerParams(
            dimension_semantics=("arbitrary",), vmem_limit_bytes=VMEM_LIMIT),
        name="ffn_" + mode,
    )(src, meta_tile, pre, post, wg, wu, wd)


def _load_columns(src, dst_ref, n_chunks, stage, sems):
    def copy(j, slot):
        off = pl.multiple_of(j * FF_CHUNK, FF_CHUNK)
        return pltpu.make_async_copy(src.at[:, pl.ds(off, FF_CHUNK)], stage.at[slot],
                                     sems.at[slot])

    copy(0, 0).start()

    def body(j, carry):
        slot = lax.rem(j, 2)

        @pl.when(j + 1 < n_chunks)
        def _():
            copy(j + 1, 1 - slot).start()

        copy(j, slot).wait()
        dst_ref[:, pl.ds(pl.multiple_of(j * FF_CHUNK, FF_CHUNK), FF_CHUNK)] = (
            stage[slot].astype(BF16))
        return carry

    lax.fori_loop(0, n_chunks, body, 0)


def _mixer_kernel(h_ref, pre_ref, post_ref, win_hbm, wtail_ref, wout_hbm, poolw_ref, pscale_ref,
                  convw_ref, w2_ref, b2_ref, gnorm_ref, o_ref,
                  a_s, pbuf, zbuf, b_s, q_s, k_s, rep_s, qe_s, kd_s, lq_s, lk_s, v_s, gate_s, o_s, i_s, u_s, sb_s, y_s, state_s,
                  pmeta, zmeta, smeta, win_ref, wout_ref, wstage, wsems, *, tiles_per_seq, layer):
    step = pl.program_id(0)
    ts = TILE
    is_meta = step == 0

    @pl.when(is_meta)
    def _():
        _load_columns(win_hbm.at[layer], win_ref, OFF_OG // FF_CHUNK, wstage, wsems)
        _load_columns(wout_hbm.at[layer], wout_ref, D_MODEL // FF_CHUNK, wstage, wsems)
        pbuf[0:POOL_HIST, :] = jnp.zeros((POOL_HIST, POOL_WIDTH), F32)
        zbuf[0:CONV_HIST, :] = jnp.zeros((CONV_HIST, CONV_WIDTH), F32)
        state_s[...] = jnp.zeros_like(state_s)

    @pl.when(jnp.logical_and(step > 0, lax.rem(step - 1, tiles_per_seq) == 0))
    def _():
        pbuf[0:POOL_HIST, :] = pmeta[...]
        zbuf[0:CONV_HIST, :] = zmeta[...]
        state_s[...] = smeta[...]

    nc = ts // CHUNK
    ri = lax.broadcasted_iota(jnp.int32, (CHUNK, CHUNK), 0)
    ci = lax.broadcasted_iota(jnp.int32, (CHUNK, CHUNK), 1)

    def proj(off, width):
        w = (win_ref[:, off:off + width] if off < OFF_OG
             else wtail_ref[:, off - OFF_OG:off - OFF_OG + width])
        return jnp.dot(a_s[...], w, preferred_element_type=F32)

    def pool_mixer():
        x1 = pbuf[...]
        s2 = x1 + pltpu.roll(x1, 1, 0)
        s4 = s2 + pltpu.roll(s2, 2, 0)
        s8 = s4 + pltpu.roll(s4, 4, 0)
        s16 = s8 + pltpu.roll(s8, 8, 0)
        lane = lax.broadcasted_iota(jnp.int32, (ts, POOL_WIDTH), 1)
        row = lax.broadcasted_iota(jnp.int32, (ts, POOL_WIDTH), 0)
        win = jnp.where(lane < 64, 2, jnp.where(lane < 128, 4, jnp.where(lane < 192, 8, 16)))
        wsum = jnp.where(lane < 64, s2[POOL_HIST:], jnp.where(
            lane < 128, s4[POOL_HIST:], jnp.where(lane < 192, s8[POOL_HIST:], s16[POOL_HIST:])))
        pos1 = jnp.where(is_meta, row - (META_PAD - 1), max(POOL_WINDOWS))
        cnt = jnp.clip(pos1, 1, win).astype(F32)
        m = (wsum / cnt - x1[POOL_HIST:]).astype(BF16)
        y_pool = jnp.dot(m, poolw_ref[...], preferred_element_type=F32) * pscale_ref[...]
        y_s[:, 0:POOL_WIDTH] = y_pool.astype(BF16)
        pbuf[0:POOL_HIST, :] = x1[ts:ts + POOL_HIST]

    def conv_mixer():
        zz = zbuf[...]
        z1 = pltpu.roll(zz, 1, 0)
        z2 = pltpu.roll(zz, 2, 0)
        cw = convw_ref[...]
        yc = (cw[0:1, :] * z2[CONV_HIST:] + cw[1:2, :] * z1[CONV_HIST:]
              + cw[2:3, :] * zz[CONV_HIST:])
        y_s[:, POOL_WIDTH:POOL_WIDTH + CONV_WIDTH] = (proj(OFF_CB, CONV_WIDTH) * yc).astype(BF16)
        zbuf[0:CONV_HIST, :] = zz[ts:ts + CONV_HIST]

    @pl.when(step >= 0)
    def _():
        a_s[...] = _rms(h_ref[...], pre_ref[...]).astype(BF16)
        g_low = proj(OFF_GL, GL_PAD).astype(BF16)
        gx = jnp.dot(g_low, w2_ref[...], preferred_element_type=F32) + b2_ref[...]
        log_sig = jnp.minimum(gx, 0.0) - jnp.log(1.0 + jnp.exp(-jnp.abs(gx)))
        la = log_sig * (LOG2_E / GLA_TAU)
        la_hi = la.astype(BF16)
        la_lo = (la - la_hi.astype(F32)).astype(BF16)
        la_cat = jnp.concatenate([la_hi, la_lo], axis=1)
        pbuf[POOL_HIST:, :] = proj(OFF_POOL, POOL_WIDTH)
        zbuf[CONV_HIST:, :] = proj(OFF_CC, CONV_WIDTH) * proj(OFF_CU, CONV_WIDTH)
        tril = jnp.where(ri >= ci, 1.0, 0.0).astype(BF16)
        for c in range(nc):
            rows = slice(c * CHUNK, (c + 1) * CHUNK)
            bb = jnp.dot(tril, la_cat[rows, :], preferred_element_type=F32)
            b_s[rows, :] = bb[:, :GLA_KW] + bb[:, GLA_KW:]
        q_s[...] = proj(OFF_Q, GLA_KW) * (GLA_DK ** -0.5)
        k_s[...] = proj(OFF_K, GLA_KW)
        pool_mixer()
        conv_mixer()
        v_s[...] = proj(OFF_V, GLA_WIDTH).astype(BF16)
        og = proj(OFF_OG, GLA_WIDTH)
        gate_s[...] = og * jax.nn.sigmoid(og)

    b, q, k = b_s[...], q_s[...], k_s[...]
    b3 = b.reshape(nc, CHUNK, GLA_KW)
    b_last = b3[:, CHUNK - 1:CHUNK, :]
    qe_s[...] = (q.reshape(b3.shape) * jnp.exp2(b3)).reshape(ts, GLA_KW).astype(BF16)
    kd_s[...] = (k.reshape(b3.shape) * jnp.exp2(b_last - b3)).reshape(ts, GLA_KW).astype(BF16)
    decay = jnp.exp2(b_last)

    nt = (((1,), (1,)), ((), ()))
    tn = (((0,), (0,)), ((), ()))
    ksl = [slice(hd * GLA_DK, (hd + 1) * GLA_DK) for hd in range(GLA_HEADS)]
    vsl = [slice(hd * GLA_DV, (hd + 1) * GLA_DV) for hd in range(GLA_HEADS)]
    pairs = [(c, hd) for c in range(nc) for hd in range(GLA_HEADS)]
    group = 2 * GLA_HEADS
    chunk_rows = lambda c: slice(c * CHUNK, (c + 1) * CHUNK)

    single_ref_ok = jnp.min(b) > -SINGLE_REF_MAX_LOG2

    def state_increments(grp):
        return [lax.dot_general(v_s[chunk_rows(c), vsl[hd]], kd_s[chunk_rows(c), ksl[hd]], tn,
                                preferred_element_type=F32) for c, hd in grp]

    def finish_group(grp, ps, us):
        pvs = [jnp.dot(p, v_s[chunk_rows(c), vsl[hd]], preferred_element_type=F32)
               for p, (c, hd) in zip(ps, grp)]
        for pv, u, (c, hd) in zip(pvs, us, grp):
            o_s[chunk_rows(c), vsl[hd]] = pv
            u_s[c * GLA_HEADS + hd] = u

    @pl.when(single_ref_ok)
    def _():
        lk_s[0] = (k.reshape(b3.shape) * jnp.exp2(-b3)).reshape(ts, GLA_KW).astype(BF16)
        causal = ri >= ci
        for g0 in range(0, len(pairs), group):
            grp = pairs[g0:g0 + group]
            scs = [lax.dot_general(qe_s[chunk_rows(c), ksl[hd]], lk_s[0, chunk_rows(c), ksl[hd]],
                                   nt, preferred_element_type=F32) for c, hd in grp]
            us = state_increments(grp)
            finish_group(grp, [jnp.where(causal, sc, 0.0).astype(BF16) for sc in scs], us)

    lane_hi = lax.broadcasted_iota(jnp.int32, (1, 1, 128), 2) >= GLA_DK
    t64 = lax.broadcasted_iota(jnp.int32, (1, CHUNK, 1), 1)
    blk16, blk4, pos4 = t64 // 16, (t64 // 4) % 4, t64 % 4
    in_hi4 = lax.broadcasted_iota(jnp.int32, (1, 8, 1), 1) >= 4
    neg = -1e30

    def replicate_heads():
        for n, src in enumerate((b_s, q_s, k_s)):
            for c2 in range(2):
                x = src[:, 128 * c2:128 * c2 + 128]
                x_sw = pltpu.roll(x, GLA_DK, 1)
                rep_s[GLA_HEADS * n + 2 * c2] = jnp.where(lane_hi[0], x_sw, x)
                rep_s[GLA_HEADS * n + 2 * c2 + 1] = jnp.where(lane_hi[0], x, x_sw)

    def build_level_operands(hd):
        rb, rq, rk = (rep_s[GLA_HEADS * n + hd] for n in range(3))
        rb64, rq64, rk64 = (x.reshape(nc, CHUNK, 128) for x in (rb, rq, rk))
        rb16 = rb.reshape(ts // 16, 16, 128)
        rb8 = rb.reshape(ts // 8, 8, 128)
        to64 = lambda x: x.reshape(nc, CHUNK, 128)
        end4 = jnp.where(in_hi4, rb8[:, 7:8, :], rb8[:, 3:4, :])
        k1 = rk64 * to64(jnp.exp2(rb16[:, 15:16, :] - rb16))
        k2 = rk64 * to64(jnp.exp2(end4 - rb8))
        nxt = jnp.where(lane_hi, pltpu.roll(rb8, 7, 1), rb8)
        for p in range(2):
            col = slice(128 * p, 128 * p + 128)
            j = jnp.where(lane_hi, 2 * p + 1, 2 * p)
            ref1 = jnp.where(lane_hi, rb64[:, 32 * p + 31:32 * p + 32, :],
                             rb64[:, 32 * p + 15:32 * p + 16, :])
            ref2 = jnp.where(lane_hi, rb16[:, 8 * p + 7:8 * p + 8, :], rb16[:, 8 * p + 3:8 * p + 4, :])
            ref3 = jnp.where(in_hi4, nxt[:, 4 + 2 * p:5 + 2 * p, :], nxt[:, 2 * p:2 * p + 1, :])
            q1 = rq64 * jnp.exp2(jnp.where(blk16 > j, rb64 - ref1, neg))
            q2 = rq64 * jnp.exp2(jnp.where(blk4 > j, to64(rb16 - ref2), neg))
            q3 = rq64 * jnp.exp2(jnp.where(pos4 >= j, to64(rb8 - ref3), neg))
            for n, (ql, kl) in enumerate(((q1, jnp.where(blk16 == j, k1, 0.0)),
                                          (q2, jnp.where(blk4 == j, k2, 0.0)),
                                          (q3, jnp.where(pos4 == j, rk64, 0.0)))):
                lq_s[3 * hd + n, :, col] = ql.reshape(ts, 128).astype(BF16)
                lk_s[3 * hd + n, :, col] = kl.reshape(ts, 128).astype(BF16)

    @pl.when(jnp.logical_not(single_ref_ok))
    def _():
        replicate_heads()
        for hd in range(GLA_HEADS):
            build_level_operands(hd)
        same16 = (ri // 16) == (ci // 16)
        same4 = (ri // 4) == (ci // 4)
        for g0 in range(0, len(pairs), group):
            grp = pairs[g0:g0 + group]
            lvl = [[lax.dot_general(lq_s[3 * hd + n, chunk_rows(c), :],
                                    lk_s[3 * hd + n, chunk_rows(c), :], nt,
                                    preferred_element_type=F32) for n in range(3)]
                   for c, hd in grp]
            us = state_increments(grp)
            finish_group(grp, [(g1 + jnp.where(same16, g2, 0.0)
                                + jnp.where(same4, g3, 0.0)).astype(BF16) for g1, g2, g3 in lvl], us)


    sts = [state_s[hd] for hd in range(GLA_HEADS)]
    for c in range(nc):
        for hd in range(GLA_HEADS):
            sb_s[c * GLA_HEADS + hd] = sts[hd].astype(BF16)
        sts = [sts[hd] * decay[c][:, ksl[hd]] + u_s[c * GLA_HEADS + hd]
               for hd in range(GLA_HEADS)]
    for hd in range(GLA_HEADS):
        state_s[hd] = sts[hd]

    for g0 in range(0, len(pairs), group):
        grp = pairs[g0:g0 + group]
        inters = [lax.dot_general(qe_s[c * CHUNK:(c + 1) * CHUNK, ksl[hd]],
                                  sb_s[c * GLA_HEADS + hd], nt, preferred_element_type=F32)
                  for c, hd in grp]
        for inter, (c, hd) in zip(inters, grp):
            i_s[c * CHUNK:(c + 1) * CHUNK, vsl[hd]] = inter

    for hd in range(GLA_HEADS):
        vs = slice(hd * GLA_DV, (hd + 1) * GLA_DV)
        o = o_s[:, vs] + i_s[:, vs]
        o = o * lax.rsqrt(jnp.mean(o * o, axis=-1, keepdims=True) + EPS) * gnorm_ref[:, vs]
        y_s[:, POOL_WIDTH + CONV_WIDTH + hd * GLA_DV:POOL_WIDTH + CONV_WIDTH + (hd + 1) * GLA_DV] = (
            o * gate_s[:, vs]).astype(BF16)

    mix = jnp.dot(y_s[...], wout_ref[...], preferred_element_type=F32)
    o_ref[...] = h_ref[...] + _rms(mix, post_ref[...])

    @pl.when(is_meta)
    def _():
        pmeta[...] = pbuf[0:POOL_HIST, :]
        zmeta[...] = zbuf[0:CONV_HIST, :]
        smeta[...] = state_s[...]


def _mixer(h, tiles_per_seq, pre, post, win, wtail, wout, poolw, pscale, convw, w2, b2, gnorm,
           *, layer):
    ts = TILE
    return pl.pallas_call(
        functools.partial(_mixer_kernel, tiles_per_seq=tiles_per_seq, layer=layer),
        grid=(h.shape[0] // ts,),
        in_specs=[
            pl.BlockSpec((ts, D_MODEL), lambda i: (i, 0)),
            _resident((1, D_MODEL)),
            _resident((1, D_MODEL)),
            pl.BlockSpec(memory_space=pl.ANY),
            _resident((D_MODEL, D_PROJ_PACKED - OFF_OG)),
            pl.BlockSpec(memory_space=pl.ANY),
            _resident((POOL_WIDTH, POOL_WIDTH)),
            _resident((1, POOL_WIDTH)),
            _resident((8, CONV_WIDTH)),
            _resident((GL_PAD, GLA_KW)),
            _resident((1, GLA_KW)),
            _resident((1, GLA_WIDTH)),
        ],
        out_specs=pl.BlockSpec((ts, D_MODEL), lambda i: (i, 0)),
        out_shape=jax.ShapeDtypeStruct(h.shape, F32),
        scratch_shapes=[
            pltpu.VMEM((ts, D_MODEL), BF16),
            pltpu.VMEM((POOL_HIST + ts, POOL_WIDTH), F32),
            pltpu.VMEM((CONV_HIST + ts, CONV_WIDTH), F32),
            pltpu.VMEM((ts, GLA_KW), F32),
            pltpu.VMEM((ts, GLA_KW), F32),
            pltpu.VMEM((ts, GLA_KW), F32),
            pltpu.VMEM((3 * GLA_HEADS, ts, 128), F32),
            pltpu.VMEM((ts, GLA_KW), BF16),
            pltpu.VMEM((ts, GLA_KW), BF16),
            pltpu.VMEM((3 * GLA_HEADS, ts, 4 * GLA_DK), BF16),
            pltpu.VMEM((3 * GLA_HEADS, ts, 4 * GLA_DK), BF16),
            pltpu.VMEM((ts, GLA_WIDTH), BF16),
            pltpu.VMEM((ts, GLA_WIDTH), F32),
            pltpu.VMEM((ts, GLA_WIDTH), F32),
            pltpu.VMEM((ts, GLA_WIDTH), F32),
            pltpu.VMEM((ts // CHUNK * GLA_HEADS, GLA_DV, GLA_DK), F32),
            pltpu.VMEM((ts // CHUNK * GLA_HEADS, GLA_DV, GLA_DK), BF16),
            pltpu.VMEM((ts, D_MODEL), BF16),
            pltpu.VMEM((GLA_HEADS, GLA_DV, GLA_DK), F32),
            pltpu.VMEM((POOL_HIST, POOL_WIDTH), F32),
            pltpu.VMEM((CONV_HIST, CONV_WIDTH), F32),
            pltpu.VMEM((GLA_HEADS, GLA_DV, GLA_DK), F32),
            pltpu.VMEM((D_MODEL, OFF_OG), BF16),
            pltpu.VMEM((D_MODEL, D_MODEL), BF16),
            pltpu.VMEM((2, D_MODEL, FF_CHUNK), F32),
            pltpu.SemaphoreType.DMA((2,)),
        ],
        compiler_params=pltpu.CompilerParams(
            dimension_semantics=("arbitrary",), vmem_limit_bytes=VMEM_LIMIT),
        name="mixer",
    )(h, pre, post, win, wtail, wout, poolw, pscale, convw, w2, b2, gnorm)


def _pack_w_in_tail(w):
    gl = w[:, OFF_OG:OFF_OG + GLA_GATE_RANK]
    og = w[:, OFF_OG + GLA_GATE_RANK:]
    gl = jnp.pad(gl, ((0, 0), (0, GL_PAD - GLA_GATE_RANK)))
    return jnp.concatenate([og, gl], axis=-1).astype(BF16)


def _block_diag(w):
    out = jnp.zeros((POOL_WIDTH, POOL_WIDTH), w.dtype)
    for g in range(len(POOL_WINDOWS)):
        sl = slice(g * POOL_GROUP, (g + 1) * POOL_GROUP)
        out = out.at[sl, sl].set(w[g])
    return out


def kernel(x, meta, ffn1_pre, ffn1_post, ffn1_wg, ffn1_wu, ffn1_wd, mix_pre, mix_post, w_in,
           pool_w, pool_scale, conv_w, gla_w2, gla_b2, gla_norm, w_out,
           ffn2_pre, ffn2_post, ffn2_wg, ffn2_wu, ffn2_wd):
    n_batch, seq, d = x.shape
    depth = w_in.shape[0]
    assert d == D_MODEL and seq % TILE == 0 and meta.shape[0] == N_META
    meta_tile = jnp.pad(meta.astype(F32), ((META_PAD, 0), (0, 0)))
    h = x.reshape(n_batch * seq, d)
    row = lambda v: v.reshape(1, -1).astype(F32)
    for l in range(depth):
        h = _ffn(h, meta_tile, row(ffn1_pre[l]), row(ffn1_post[l]), ffn1_wg, ffn1_wu, ffn1_wd,
                 mode="first" if l == 0 else "mid", layer=l)
        h = _mixer(
            h, seq // TILE, row(mix_pre[l]), row(mix_post[l]), w_in, _pack_w_in_tail(w_in[l]),
            w_out, _block_diag(pool_w[l]).astype(BF16), row(pool_scale[l]),
            jnp.pad(conv_w[l].astype(F32), ((0, 8 - CONV_K), (0, 0))),
            jnp.pad(gla_w2[l], ((0, GL_PAD - GLA_GATE_RANK), (0, 0))).astype(BF16),
            row(gla_b2[l]), row(gla_norm[l]), layer=l)
        h = _ffn(h, meta_tile, row(ffn2_pre[l]), row(ffn2_post[l]), ffn2_wg, ffn2_wu, ffn2_wd,
                 mode="last" if l == depth - 1 else "mid", layer=l)
    return h.reshape(n_batch, seq, d)
```

```python
import functools

import jax
import jax.numpy as jnp
import numpy as np
from jax import lax
from jax.experimental import pallas as pl
from jax.experimental.pallas import tpu as pltpu

F32 = jnp.float32
BF16 = jnp.bfloat16

D_MODEL = 1024
D_FF = 2816
N_META = 16
EPS = 1e-6
FFN_RESID = 0.5

POOL_WIDTH = 256
POOL_WINDOWS = (2, 4, 8, 16)
POOL_GROUP = 64
CONV_WIDTH = 256
CONV_K = 3
GLA_HEADS = 4
GLA_DK = 64
GLA_DV = 128
GLA_KW = GLA_HEADS * GLA_DK
GLA_WIDTH = GLA_HEADS * GLA_DV
GLA_GATE_RANK = 16
GLA_TAU = 16.0
LOG2_E = 1.4426950408889634
SINGLE_REF_MAX_LOG2 = 100.0
CHUNK = 64


OFF_POOL, OFF_CB, OFF_CC, OFF_CU, OFF_Q, OFF_K = 0, 256, 512, 768, 1024, 1280
OFF_V, OFF_OG, OFF_GL = 1536, 2048, 2560
GL_PAD = 128
D_PROJ_PACKED = OFF_GL + GL_PAD

V7X_VMEM_BYTES = 64 * 1024 * 1024
VMEM_LIMIT = 56 * 1024 * 1024

TILE = 512
FF_CHUNK = 256
META_PAD = TILE - N_META
META_ROWS = 64
POOL_HIST = 16
CONV_HIST = 8


def _rms(x, g):
    return x * lax.rsqrt(jnp.mean(x * x, axis=-1, keepdims=True) + EPS) * g


def _load_ffn_weights(layer, wg_hbm, wu_hbm, wd_hbm, wg_ref, wu_ref, wd_ref, stage_cols,
                      stage_rows, sems, on_chunk):
    n_chunks = D_FF // FF_CHUNK

    def copies(j, slot):
        off = pl.multiple_of(j * FF_CHUNK, FF_CHUNK)
        return (
            pltpu.make_async_copy(wg_hbm.at[layer, :, pl.ds(off, FF_CHUNK)],
                                  stage_cols.at[0, slot], sems.at[0, slot]),
            pltpu.make_async_copy(wu_hbm.at[layer, :, pl.ds(off, FF_CHUNK)],
                                  stage_cols.at[1, slot], sems.at[1, slot]),
            pltpu.make_async_copy(wd_hbm.at[layer, pl.ds(off, FF_CHUNK), :],
                                  stage_rows.at[slot], sems.at[2, slot]),
        )

    for cp in copies(0, 0):
        cp.start()

    def body(j, carry):
        slot = lax.rem(j, 2)

        @pl.when(j + 1 < n_chunks)
        def _():
            for cp in copies(j + 1, 1 - slot):
                cp.start()

        for cp in copies(j, slot):
            cp.wait()
        off = pl.multiple_of(j * FF_CHUNK, FF_CHUNK)
        wg_ref[:, pl.ds(off, FF_CHUNK)] = stage_cols[0, slot].astype(BF16)
        wu_ref[:, pl.ds(off, FF_CHUNK)] = stage_cols[1, slot].astype(BF16)
        wd_ref[pl.ds(off, FF_CHUNK), :] = stage_rows[slot].astype(BF16)
        on_chunk(off)
        return carry

    lax.fori_loop(0, n_chunks, body, 0)


def _ffn_kernel(h_ref, meta_ref, pre_ref, post_ref, wg_hbm, wu_hbm, wd_hbm, o_ref,
                a_ref, z_ref, wg_ref, wu_ref, wd_ref, stage_cols, stage_rows, sems, *,
                step0, layer):
    step = pl.program_id(0)

    def hidden_chunk(rows, cols):
        a = a_ref[rows, :]
        g = jnp.dot(a, wg_ref[:, cols], preferred_element_type=F32)
        u = jnp.dot(a, wu_ref[:, cols], preferred_element_type=F32)
        z_ref[rows, cols] = (g * jax.nn.sigmoid(g) * u).astype(BF16)

    def finish(rows, h):
        f = jnp.dot(z_ref[rows, :], wd_ref[...], preferred_element_type=F32)
        o_ref[rows, :] = h + FFN_RESID * _rms(f, post_ref[...])

    @pl.when(step == 0)
    def _():
        rows = slice(0, TILE) if step0 == "tokens" else slice(TILE - META_ROWS, TILE)
        h = (meta_ref if step0 == "meta_input" else h_ref)[rows, :]
        a_ref[rows, :] = _rms(h, pre_ref[...]).astype(BF16)
        _load_ffn_weights(layer, wg_hbm, wu_hbm, wd_hbm, wg_ref, wu_ref, wd_ref, stage_cols,
                          stage_rows, sems,
                          lambda off: hidden_chunk(rows, pl.ds(off, FF_CHUNK)))
        if step0 != "tokens":
            o_ref[0:TILE - META_ROWS, :] = jnp.zeros((TILE - META_ROWS, D_MODEL), F32)
        finish(rows, h)

    @pl.when(step > 0)
    def _():
        h = h_ref[...]
        a_ref[...] = _rms(h, pre_ref[...]).astype(BF16)
        for j in range(D_FF // FF_CHUNK):
            hidden_chunk(slice(0, TILE), slice(j * FF_CHUNK, (j + 1) * FF_CHUNK))
        finish(slice(0, TILE), h)


def _resident(shape):
    return pl.BlockSpec(shape, lambda *_: (0,) * len(shape), pipeline_mode=pl.Buffered(1))


def _ffn(src, meta_tile, pre, post, wg, wu, wd, *, mode, layer):
    tile = TILE
    n_tiles = src.shape[0] // tile
    assert src.shape[0] % tile == 0
    if mode == "first":
        grid, src_map = n_tiles + 1, lambda i: (jnp.maximum(i - 1, 0), 0)
    elif mode == "mid":
        grid, src_map = n_tiles, lambda i: (i, 0)
    else:
        grid, src_map = n_tiles - 1, lambda i: (i + 1, 0)
    out_rows = grid * tile
    return pl.pallas_call(
        functools.partial(
            _ffn_kernel, layer=layer,
            step0={"first": "meta_input", "mid": "meta_block", "last": "tokens"}[mode]),
        grid=(grid,),
        in_specs=[
            pl.BlockSpec((tile, D_MODEL), src_map),
            _resident((TILE, D_MODEL)),
            _resident((1, D_MODEL)),
            _resident((1, D_MODEL)),
            pl.BlockSpec(memory_space=pl.ANY),
            pl.BlockSpec(memory_space=pl.ANY),
            pl.BlockSpec(memory_space=pl.ANY),
        ],
        out_specs=pl.BlockSpec((tile, D_MODEL), lambda i: (i, 0)),
        out_shape=jax.ShapeDtypeStruct((out_rows, D_MODEL), F32),
        scratch_shapes=[
            pltpu.VMEM((tile, D_MODEL), BF16),
            pltpu.VMEM((tile, D_FF), BF16),
            pltpu.VMEM((D_MODEL, D_FF), BF16),
            pltpu.VMEM((D_MODEL, D_FF), BF16),
            pltpu.VMEM((D_FF, D_MODEL), BF16),
            pltpu.VMEM((2, 2, D_MODEL, FF_CHUNK), F32),
            pltpu.VMEM((2, FF_CHUNK, D_MODEL), F32),
            pltpu.SemaphoreType.DMA((3, 2)),
        ],
        compiler_params=pltpu.CompilerParams(
            dimension_semantics=("arbitrary",), vmem_limit_bytes=VMEM_LIMIT),
        name="ffn_" + mode,
    )(src, meta_tile, pre, post, wg, wu, wd)


def _mixer_kernel(h_ref, pre_ref, post_ref, win_ref, wout_ref, poolw_ref, pscale_ref,
                  convw_ref, w2_ref, b2_ref, gnorm_ref, o_ref,
                  a_s, pbuf, zbuf, b_s, q_s, k_s, rep_s, qe_s, kd_s, lq_s, lk_s, v_s, gate_s, o_s, i_s, u_s, sb_s, y_s, state_s,
                  pmeta, zmeta, smeta, *, tiles_per_seq):
    step = pl.program_id(0)
    ts = TILE
    is_meta = step == 0

    @pl.when(is_meta)
    def _():
        pbuf[0:POOL_HIST, :] = jnp.zeros((POOL_HIST, POOL_WIDTH), F32)
        zbuf[0:CONV_HIST, :] = jnp.zeros((CONV_HIST, CONV_WIDTH), F32)
        state_s[...] = jnp.zeros_like(state_s)

    @pl.when(jnp.logical_and(step > 0, lax.rem(step - 1, tiles_per_seq) == 0))
    def _():
        pbuf[0:POOL_HIST, :] = pmeta[...]
        zbuf[0:CONV_HIST, :] = zmeta[...]
        state_s[...] = smeta[...]

    nc = ts // CHUNK
    ri = lax.broadcasted_iota(jnp.int32, (CHUNK, CHUNK), 0)
    ci = lax.broadcasted_iota(jnp.int32, (CHUNK, CHUNK), 1)

    def proj(off, width):
        return jnp.dot(a_s[...], win_ref[:, off:off + width], preferred_element_type=F32)

    def pool_mixer():
        x1 = pbuf[...]
        s2 = x1 + pltpu.roll(x1, 1, 0)
        s4 = s2 + pltpu.roll(s2, 2, 0)
        s8 = s4 + pltpu.roll(s4, 4, 0)
        s16 = s8 + pltpu.roll(s8, 8, 0)
        lane = lax.broadcasted_iota(jnp.int32, (ts, POOL_WIDTH), 1)
        row = lax.broadcasted_iota(jnp.int32, (ts, POOL_WIDTH), 0)
        win = jnp.where(lane < 64, 2, jnp.where(lane < 128, 4, jnp.where(lane < 192, 8, 16)))
        wsum = jnp.where(lane < 64, s2[POOL_HIST:], jnp.where(
            lane < 128, s4[POOL_HIST:], jnp.where(lane < 192, s8[POOL_HIST:], s16[POOL_HIST:])))
        pos1 = jnp.where(is_meta, row - (META_PAD - 1), max(POOL_WINDOWS))
        cnt = jnp.clip(pos1, 1, win).astype(F32)
        m = (wsum / cnt - x1[POOL_HIST:]).astype(BF16)
        y_pool = jnp.dot(m, poolw_ref[...], preferred_element_type=F32) * pscale_ref[...]
        y_s[:, 0:POOL_WIDTH] = y_pool.astype(BF16)
        pbuf[0:POOL_HIST, :] = x1[ts:ts + POOL_HIST]

    def conv_mixer():
        zz = zbuf[...]
        z1 = pltpu.roll(zz, 1, 0)
        z2 = pltpu.roll(zz, 2, 0)
        cw = convw_ref[...]
        yc = (cw[0:1, :] * z2[CONV_HIST:] + cw[1:2, :] * z1[CONV_HIST:]
              + cw[2:3, :] * zz[CONV_HIST:])
        y_s[:, POOL_WIDTH:POOL_WIDTH + CONV_WIDTH] = (proj(OFF_CB, CONV_WIDTH) * yc).astype(BF16)
        zbuf[0:CONV_HIST, :] = zz[ts:ts + CONV_HIST]

    @pl.when(step >= 0)
    def _():
        a_s[...] = _rms(h_ref[...], pre_ref[...]).astype(BF16)
        g_low = proj(OFF_GL, GL_PAD).astype(BF16)
        gx = jnp.dot(g_low, w2_ref[...], preferred_element_type=F32) + b2_ref[...]
        log_sig = jnp.minimum(gx, 0.0) - jnp.log(1.0 + jnp.exp(-jnp.abs(gx)))
        la = log_sig * (LOG2_E / GLA_TAU)
        la_hi = la.astype(BF16)
        la_lo = (la - la_hi.astype(F32)).astype(BF16)
        la_cat = jnp.concatenate([la_hi, la_lo], axis=1)
        pbuf[POOL_HIST:, :] = proj(OFF_POOL, POOL_WIDTH)
        zbuf[CONV_HIST:, :] = proj(OFF_CC, CONV_WIDTH) * proj(OFF_CU, CONV_WIDTH)
        tril = jnp.where(ri >= ci, 1.0, 0.0).astype(BF16)
        for c in range(nc):
            rows = slice(c * CHUNK, (c + 1) * CHUNK)
            bb = jnp.dot(tril, la_cat[rows, :], preferred_element_type=F32)
            b_s[rows, :] = bb[:, :GLA_KW] + bb[:, GLA_KW:]
        q_s[...] = proj(OFF_Q, GLA_KW) * (GLA_DK ** -0.5)
        k_s[...] = proj(OFF_K, GLA_KW)
        pool_mixer()
        conv_mixer()
        v_s[...] = proj(OFF_V, GLA_WIDTH).astype(BF16)
        og = proj(OFF_OG, GLA_WIDTH)
        gate_s[...] = og * jax.nn.sigmoid(og)

    b, q, k = b_s[...], q_s[...], k_s[...]
    b3 = b.reshape(nc, CHUNK, GLA_KW)
    b_last = b3[:, CHUNK - 1:CHUNK, :]
    qe_s[...] = (q.reshape(b3.shape) * jnp.exp2(b3)).reshape(ts, GLA_KW).astype(BF16)
    kd_s[...] = (k.reshape(b3.shape) * jnp.exp2(b_last - b3)).reshape(ts, GLA_KW).astype(BF16)
    decay = jnp.exp2(b_last)

    nt = (((1,), (1,)), ((), ()))
    tn = (((0,), (0,)), ((), ()))
    ksl = [slice(hd * GLA_DK, (hd + 1) * GLA_DK) for hd in range(GLA_HEADS)]
    vsl = [slice(hd * GLA_DV, (hd + 1) * GLA_DV) for hd in range(GLA_HEADS)]
    pairs = [(c, hd) for c in range(nc) for hd in range(GLA_HEADS)]
    group = 2 * GLA_HEADS
    chunk_rows = lambda c: slice(c * CHUNK, (c + 1) * CHUNK)

    single_ref_ok = jnp.min(b) > -SINGLE_REF_MAX_LOG2

    def state_increments(grp):
        return [lax.dot_general(v_s[chunk_rows(c), vsl[hd]], kd_s[chunk_rows(c), ksl[hd]], tn,
                                preferred_element_type=F32) for c, hd in grp]

    def finish_group(grp, ps, us):
        pvs = [jnp.dot(p, v_s[chunk_rows(c), vsl[hd]], preferred_element_type=F32)
               for p, (c, hd) in zip(ps, grp)]
        for pv, u, (c, hd) in zip(pvs, us, grp):
            o_s[chunk_rows(c), vsl[hd]] = pv
            u_s[c * GLA_HEADS + hd] = u

    @pl.when(single_ref_ok)
    def _():
        lk_s[0] = (k.reshape(b3.shape) * jnp.exp2(-b3)).reshape(ts, GLA_KW).astype(BF16)
        causal = ri >= ci
        for g0 in range(0, len(pairs), group):
            grp = pairs[g0:g0 + group]
            scs = [lax.dot_general(qe_s[chunk_rows(c), ksl[hd]], lk_s[0, chunk_rows(c), ksl[hd]],
                                   nt, preferred_element_type=F32) for c, hd in grp]
            us = state_increments(grp)
            finish_group(grp, [jnp.where(causal, sc, 0.0).astype(BF16) for sc in scs], us)

    lane_hi = lax.broadcasted_iota(jnp.int32, (1, 1, 128), 2) >= GLA_DK
    t64 = lax.broadcasted_iota(jnp.int32, (1, CHUNK, 1), 1)
    blk16, blk4, pos4 = t64 // 16, (t64 // 4) % 4, t64 % 4
    in_hi4 = lax.broadcasted_iota(jnp.int32, (1, 8, 1), 1) >= 4
    neg = -1e30

    def replicate_heads():
        for n, src in enumerate((b_s, q_s, k_s)):
            for c2 in range(2):
                x = src[:, 128 * c2:128 * c2 + 128]
                x_sw = pltpu.roll(x, GLA_DK, 1)
                rep_s[GLA_HEADS * n + 2 * c2] = jnp.where(lane_hi[0], x_sw, x)
                rep_s[GLA_HEADS * n + 2 * c2 + 1] = jnp.where(lane_hi[0], x, x_sw)

    def build_level_operands(hd):
        rb, rq, rk = (rep_s[GLA_HEADS * n + hd] for n in range(3))
        rb64, rq64, rk64 = (x.reshape(nc, CHUNK, 128) for x in (rb, rq, rk))
        rb16 = rb.reshape(ts // 16, 16, 128)
        rb8 = rb.reshape(ts // 8, 8, 128)
        to64 = lambda x: x.reshape(nc, CHUNK, 128)
        end4 = jnp.where(in_hi4, rb8[:, 7:8, :], rb8[:, 3:4, :])
        k1 = rk64 * to64(jnp.exp2(rb16[:, 15:16, :] - rb16))
        k2 = rk64 * to64(jnp.exp2(end4 - rb8))
        nxt = jnp.where(lane_hi, pltpu.roll(rb8, 7, 1), rb8)
        for p in range(2):
            col = slice(128 * p, 128 * p + 128)
            j = jnp.where(lane_hi, 2 * p + 1, 2 * p)
            ref1 = jnp.where(lane_hi, rb64[:, 32 * p + 31:32 * p + 32, :],
                             rb64[:, 32 * p + 15:32 * p + 16, :])
            ref2 = jnp.where(lane_hi, rb16[:, 8 * p + 7:8 * p + 8, :], rb16[:, 8 * p + 3:8 * p + 4, :])
            ref3 = jnp.where(in_hi4, nxt[:, 4 + 2 * p:5 + 2 * p, :], nxt[:, 2 * p:2 * p + 1, :])
            q1 = rq64 * jnp.exp2(jnp.where(blk16 > j, rb64 - ref1, neg))
            q2 = rq64 * jnp.exp2(jnp.where(blk4 > j, to64(rb16 - ref2), neg))
            q3 = rq64 * jnp.exp2(jnp.where(pos4 >= j, to64(rb8 - ref3), neg))
            for n, (ql, kl) in enumerate(((q1, jnp.where(blk16 == j, k1, 0.0)),
                                          (q2, jnp.where(blk4 == j, k2, 0.0)),
                                          (q3, jnp.where(pos4 == j, rk64, 0.0)))):
                lq_s[3 * hd + n, :, col] = ql.reshape(ts, 128).astype(BF16)
                lk_s[3 * hd + n, :, col] = kl.reshape(ts, 128).astype(BF16)

    @pl.when(jnp.logical_not(single_ref_ok))
    def _():
        replicate_heads()
        for hd in range(GLA_HEADS):
            build_level_operands(hd)
        same16 = (ri // 16) == (ci // 16)
        same4 = (ri // 4) == (ci // 4)
        for g0 in range(0, len(pairs), group):
            grp = pairs[g0:g0 + group]
            lvl = [[lax.dot_general(lq_s[3 * hd + n, chunk_rows(c), :],
                                    lk_s[3 * hd + n, chunk_rows(c), :], nt,
                                    preferred_element_type=F32) for n in range(3)]
                   for c, hd in grp]
            us = state_increments(grp)
            finish_group(grp, [(g1 + jnp.where(same16, g2, 0.0)
                                + jnp.where(same4, g3, 0.0)).astype(BF16) for g1, g2, g3 in lvl], us)


    sts = [state_s[hd] for hd in range(GLA_HEADS)]
    for c in range(nc):
        for hd in range(GLA_HEADS):
            sb_s[c * GLA_HEADS + hd] = sts[hd].astype(BF16)
        sts = [sts[hd] * decay[c][:, ksl[hd]] + u_s[c * GLA_HEADS + hd]
               for hd in range(GLA_HEADS)]
    for hd in range(GLA_HEADS):
        state_s[hd] = sts[hd]

    for g0 in range(0, len(pairs), group):
        grp = pairs[g0:g0 + group]
        inters = [lax.dot_general(qe_s[c * CHUNK:(c + 1) * CHUNK, ksl[hd]],
                                  sb_s[c * GLA_HEADS + hd], nt, preferred_element_type=F32)
                  for c, hd in grp]
        for inter, (c, hd) in zip(inters, grp):
            i_s[c * CHUNK:(c + 1) * CHUNK, vsl[hd]] = inter

    for hd in range(GLA_HEADS):
        vs = slice(hd * GLA_DV, (hd + 1) * GLA_DV)
        o = o_s[:, vs] + i_s[:, vs]
        o = o * lax.rsqrt(jnp.mean(o * o, axis=-1, keepdims=True) + EPS) * gnorm_ref[:, vs]
        y_s[:, POOL_WIDTH + CONV_WIDTH + hd * GLA_DV:POOL_WIDTH + CONV_WIDTH + (hd + 1) * GLA_DV] = (
            o * gate_s[:, vs]).astype(BF16)

    mix = jnp.dot(y_s[...], wout_ref[...], preferred_element_type=F32)
    o_ref[...] = h_ref[...] + _rms(mix, post_ref[...])

    @pl.when(is_meta)
    def _():
        pmeta[...] = pbuf[0:POOL_HIST, :]
        zmeta[...] = zbuf[0:CONV_HIST, :]
        smeta[...] = state_s[...]


def _mixer(h, tiles_per_seq, pre, post, win, wout, poolw, pscale, convw, w2, b2, gnorm):
    ts = TILE
    return pl.pallas_call(
        functools.partial(_mixer_kernel, tiles_per_seq=tiles_per_seq),
        grid=(h.shape[0] // ts,),
        in_specs=[
            pl.BlockSpec((ts, D_MODEL), lambda i: (i, 0)),
            _resident((1, D_MODEL)),
            _resident((1, D_MODEL)),
            _resident((D_MODEL, D_PROJ_PACKED)),
            _resident((D_MODEL, D_MODEL)),
            _resident((POOL_WIDTH, POOL_WIDTH)),
            _resident((1, POOL_WIDTH)),
            _resident((8, CONV_WIDTH)),
            _resident((GL_PAD, GLA_KW)),
            _resident((1, GLA_KW)),
            _resident((1, GLA_WIDTH)),
        ],
        out_specs=pl.BlockSpec((ts, D_MODEL), lambda i: (i, 0)),
        out_shape=jax.ShapeDtypeStruct(h.shape, F32),
        scratch_shapes=[
            pltpu.VMEM((ts, D_MODEL), BF16),
            pltpu.VMEM((POOL_HIST + ts, POOL_WIDTH), F32),
            pltpu.VMEM((CONV_HIST + ts, CONV_WIDTH), F32),
            pltpu.VMEM((ts, GLA_KW), F32),
            pltpu.VMEM((ts, GLA_KW), F32),
            pltpu.VMEM((ts, GLA_KW), F32),
            pltpu.VMEM((3 * GLA_HEADS, ts, 128), F32),
            pltpu.VMEM((ts, GLA_KW), BF16),
            pltpu.VMEM((ts, GLA_KW), BF16),
            pltpu.VMEM((3 * GLA_HEADS, ts, 4 * GLA_DK), BF16),
            pltpu.VMEM((3 * GLA_HEADS, ts, 4 * GLA_DK), BF16),
            pltpu.VMEM((ts, GLA_WIDTH), BF16),
            pltpu.VMEM((ts, GLA_WIDTH), F32),
            pltpu.VMEM((ts, GLA_WIDTH), F32),
            pltpu.VMEM((ts, GLA_WIDTH), F32),
            pltpu.VMEM((ts // CHUNK * GLA_HEADS, GLA_DV, GLA_DK), F32),
            pltpu.VMEM((ts // CHUNK * GLA_HEADS, GLA_DV, GLA_DK), BF16),
            pltpu.VMEM((ts, D_MODEL), BF16),
            pltpu.VMEM((GLA_HEADS, GLA_DV, GLA_DK), F32),
            pltpu.VMEM((POOL_HIST, POOL_WIDTH), F32),
            pltpu.VMEM((CONV_HIST, CONV_WIDTH), F32),
            pltpu.VMEM((GLA_HEADS, GLA_DV, GLA_DK), F32),
        ],
        compiler_params=pltpu.CompilerParams(
            dimension_semantics=("arbitrary",), vmem_limit_bytes=VMEM_LIMIT),
        name="mixer",
    )(h, pre, post, win, wout, poolw, pscale, convw, w2, b2, gnorm)


def _pack_w_in(w):
    pool, cb, cc, cu, q, k, v, gl, og = jnp.split(
        w, np.cumsum((256, 256, 256, 256, 256, 256, 512, 16, 512))[:-1].tolist(), axis=-1)
    gl = jnp.pad(gl, ((0, 0), (0, GL_PAD - GLA_GATE_RANK)))
    return jnp.concatenate([pool, cb, cc, cu, q, k, v, og, gl], axis=-1).astype(BF16)


def _block_diag(w):
    out = jnp.zeros((POOL_WIDTH, POOL_WIDTH), w.dtype)
    for g in range(len(POOL_WINDOWS)):
        sl = slice(g * POOL_GROUP, (g + 1) * POOL_GROUP)
        out = out.at[sl, sl].set(w[g])
    return out


def kernel(x, meta, ffn1_pre, ffn1_post, ffn1_wg, ffn1_wu, ffn1_wd, mix_pre, mix_post, w_in,
           pool_w, pool_scale, conv_w, gla_w2, gla_b2, gla_norm, w_out,
           ffn2_pre, ffn2_post, ffn2_wg, ffn2_wu, ffn2_wd):
    n_batch, seq, d = x.shape
    depth = w_in.shape[0]
    assert d == D_MODEL and seq % TILE == 0 and meta.shape[0] == N_META
    meta_tile = jnp.pad(meta.astype(F32), ((META_PAD, 0), (0, 0)))
    h = x.reshape(n_batch * seq, d)
    row = lambda v: v.reshape(1, -1).astype(F32)
    for l in range(depth):
        h = _ffn(h, meta_tile, row(ffn1_pre[l]), row(ffn1_post[l]), ffn1_wg, ffn1_wu, ffn1_wd,
                 mode="first" if l == 0 else "mid", layer=l)
        h = _mixer(
            h, seq // TILE, row(mix_pre[l]), row(mix_post[l]), _pack_w_in(w_in[l]),
            w_out[l].astype(BF16), _block_diag(pool_w[l]).astype(BF16), row(pool_scale[l]),
            jnp.pad(conv_w[l].astype(F32), ((0, 8 - CONV_K), (0, 0))),
            jnp.pad(gla_w2[l], ((0, GL_PAD - GLA_GATE_RANK), (0, 0))).astype(BF16),
            row(gla_b2[l]), row(gla_norm[l]))
        h = _ffn(h, meta_tile, row(ffn2_pre[l]), row(ffn2_post[l]), ffn2_wg, ffn2_wu, ffn2_wd,
                 mode="last" if l == depth - 1 else "mid", layer=l)
    return h.reshape(n_batch, seq, d)
```

```python
import functools

import jax
import jax.numpy as jnp
import numpy as np
from jax import lax
from jax.experimental import pallas as pl
from jax.experimental.pallas import tpu as pltpu

F32 = jnp.float32
BF16 = jnp.bfloat16

D_MODEL = 1024
D_FF = 2816
N_META = 16
EPS = 1e-6
FFN_RESID = 0.5

POOL_WIDTH = 256
POOL_WINDOWS = (2, 4, 8, 16)
POOL_GROUP = 64
CONV_WIDTH = 256
CONV_K = 3
GLA_HEADS = 4
GLA_DK = 64
GLA_DV = 128
GLA_KW = GLA_HEADS * GLA_DK
GLA_WIDTH = GLA_HEADS * GLA_DV
GLA_GATE_RANK = 16
GLA_TAU = 16.0
LOG2_E = 1.4426950408889634
SINGLE_REF_MAX_LOG2 = 100.0
CHUNK = 64


OFF_POOL, OFF_CB, OFF_CC, OFF_CU, OFF_Q, OFF_K = 0, 256, 512, 768, 1024, 1280
OFF_V, OFF_OG, OFF_GL = 1536, 2048, 2560
GL_PAD = 128
D_PROJ_PACKED = OFF_GL + GL_PAD

V7X_VMEM_BYTES = 64 * 1024 * 1024
VMEM_LIMIT = 56 * 1024 * 1024

TILE = 512
FF_CHUNK = 256
META_PAD = TILE - N_META
META_ROWS = 64
POOL_HIST = 16
CONV_HIST = 8


def _rms(x, g):
    return x * lax.rsqrt(jnp.mean(x * x, axis=-1, keepdims=True) + EPS) * g


def _load_ffn_weights(layer, wg_hbm, wu_hbm, wd_hbm, wgu_ref, wd_ref, stage_cols,
                      stage_rows, sems, on_chunk):
    n_chunks = D_FF // FF_CHUNK

    def copies(j, slot):
        off = pl.multiple_of(j * FF_CHUNK, FF_CHUNK)
        return (
            pltpu.make_async_copy(wg_hbm.at[layer, :, pl.ds(off, FF_CHUNK)],
                                  stage_cols.at[0, slot], sems.at[0, slot]),
            pltpu.make_async_copy(wu_hbm.at[layer, :, pl.ds(off, FF_CHUNK)],
                                  stage_cols.at[1, slot], sems.at[1, slot]),
            pltpu.make_async_copy(wd_hbm.at[layer, pl.ds(off, FF_CHUNK), :],
                                  stage_rows.at[slot], sems.at[2, slot]),
        )

    for cp in copies(0, 0):
        cp.start()

    def body(j, carry):
        slot = lax.rem(j, 2)

        @pl.when(j + 1 < n_chunks)
        def _():
            for cp in copies(j + 1, 1 - slot):
                cp.start()

        for cp in copies(j, slot):
            cp.wait()
        off = pl.multiple_of(j * FF_CHUNK, FF_CHUNK)
        off2 = pl.multiple_of(2 * j * FF_CHUNK, 2 * FF_CHUNK)
        wgu_ref[:, pl.ds(off2, FF_CHUNK)] = stage_cols[0, slot].astype(BF16)
        wgu_ref[:, pl.ds(off2 + FF_CHUNK, FF_CHUNK)] = stage_cols[1, slot].astype(BF16)
        wd_ref[pl.ds(off, FF_CHUNK), :] = stage_rows[slot].astype(BF16)
        on_chunk(j)
        return carry

    lax.fori_loop(0, n_chunks, body, 0)


def _ffn_kernel(h_ref, meta_ref, pre_ref, post_ref, wg_hbm, wu_hbm, wd_hbm, o_ref,
                a_ref, z_ref, wgu_ref, wd_ref, stage_cols, stage_rows, sems, *,
                step0, layer):
    step = pl.program_id(0)

    def hidden_chunk(rows, j):
        if isinstance(j, int):
            cols, cols2 = pl.ds(j * FF_CHUNK, FF_CHUNK), pl.ds(2 * j * FF_CHUNK, 2 * FF_CHUNK)
        else:
            cols = pl.ds(pl.multiple_of(j * FF_CHUNK, FF_CHUNK), FF_CHUNK)
            cols2 = pl.ds(pl.multiple_of(2 * j * FF_CHUNK, 2 * FF_CHUNK), 2 * FF_CHUNK)
        gu = jnp.dot(a_ref[rows, :], wgu_ref[:, cols2], preferred_element_type=F32)
        g, u = gu[:, :FF_CHUNK], gu[:, FF_CHUNK:]
        z_ref[rows, cols] = (g * jax.nn.sigmoid(g) * u).astype(BF16)

    def finish(rows, h):
        f = jnp.dot(z_ref[rows, :], wd_ref[...], preferred_element_type=F32)
        o_ref[rows, :] = h + FFN_RESID * _rms(f, post_ref[...])

    @pl.when(step == 0)
    def _():
        rows = slice(0, TILE) if step0 == "tokens" else slice(TILE - META_ROWS, TILE)
        h = (meta_ref if step0 == "meta_input" else h_ref)[rows, :]
        a_ref[rows, :] = _rms(h, pre_ref[...]).astype(BF16)
        _load_ffn_weights(layer, wg_hbm, wu_hbm, wd_hbm, wgu_ref, wd_ref, stage_cols,
                          stage_rows, sems, lambda j: hidden_chunk(rows, j))
        if step0 != "tokens":
            o_ref[0:TILE - META_ROWS, :] = jnp.zeros((TILE - META_ROWS, D_MODEL), F32)
        finish(rows, h)

    @pl.when(step > 0)
    def _():
        h = h_ref[...]
        a_ref[...] = _rms(h, pre_ref[...]).astype(BF16)
        for j in range(D_FF // FF_CHUNK):
            hidden_chunk(slice(0, TILE), j)
        finish(slice(0, TILE), h)


def _resident(shape):
    return pl.BlockSpec(shape, lambda *_: (0,) * len(shape), pipeline_mode=pl.Buffered(1))


def _ffn(src, meta_tile, pre, post, wg, wu, wd, *, mode, layer):
    tile = TILE
    n_tiles = src.shape[0] // tile
    assert src.shape[0] % tile == 0
    if mode == "first":
        grid, src_map = n_tiles + 1, lambda i: (jnp.maximum(i - 1, 0), 0)
    elif mode == "mid":
        grid, src_map = n_tiles, lambda i: (i, 0)
    else:
        grid, src_map = n_tiles - 1, lambda i: (i + 1, 0)
    out_rows = grid * tile
    return pl.pallas_call(
        functools.partial(
            _ffn_kernel, layer=layer,
            step0={"first": "meta_input", "mid": "meta_block", "last": "tokens"}[mode]),
        grid=(grid,),
        in_specs=[
            pl.BlockSpec((tile, D_MODEL), src_map),
            _resident((TILE, D_MODEL)),
            _resident((1, D_MODEL)),
            _resident((1, D_MODEL)),
            pl.BlockSpec(memory_space=pl.ANY),
            pl.BlockSpec(memory_space=pl.ANY),
            pl.BlockSpec(memory_space=pl.ANY),
        ],
        out_specs=pl.BlockSpec((tile, D_MODEL), lambda i: (i, 0)),
        out_shape=jax.ShapeDtypeStruct((out_rows, D_MODEL), F32),
        scratch_shapes=[
            pltpu.VMEM((tile, D_MODEL), BF16),
            pltpu.VMEM((tile, D_FF), BF16),
            pltpu.VMEM((D_MODEL, 2 * D_FF), BF16),
            pltpu.VMEM((D_FF, D_MODEL), BF16),
            pltpu.VMEM((2, 2, D_MODEL, FF_CHUNK), F32),
            pltpu.VMEM((2, FF_CHUNK, D_MODEL), F32),
            pltpu.SemaphoreType.DMA((3, 2)),
        ],
        compiler_params=pltpu.CompilerParams(
            dimension_semantics=("arbitrary",), vmem_limit_bytes=VMEM_LIMIT),
        name="ffn_" + mode,
    )(src, meta_tile, pre, post, wg, wu, wd)


def _mixer_kernel(h_ref, pre_ref, post_ref, win_ref, wout_ref, poolw_ref, pscale_ref,
                  convw_ref, w2_ref, b2_ref, gnorm_ref, o_ref,
                  a_s, pbuf, zbuf, b_s, q_s, k_s, rep_s, qe_s, kd_s, lq_s, lk_s, v_s, gate_s, o_s, i_s, u_s, sb_s, y_s, state_s,
                  pmeta, zmeta, smeta, *, tiles_per_seq):
    step = pl.program_id(0)
    ts = TILE
    is_meta = step == 0

    @pl.when(is_meta)
    def _():
        pbuf[0:POOL_HIST, :] = jnp.zeros((POOL_HIST, POOL_WIDTH), F32)
        zbuf[0:CONV_HIST, :] = jnp.zeros((CONV_HIST, CONV_WIDTH), F32)
        state_s[...] = jnp.zeros_like(state_s)

    @pl.when(jnp.logical_and(step > 0, lax.rem(step - 1, tiles_per_seq) == 0))
    def _():
        pbuf[0:POOL_HIST, :] = pmeta[...]
        zbuf[0:CONV_HIST, :] = zmeta[...]
        state_s[...] = smeta[...]

    nc = ts // CHUNK
    ri = lax.broadcasted_iota(jnp.int32, (CHUNK, CHUNK), 0)
    ci = lax.broadcasted_iota(jnp.int32, (CHUNK, CHUNK), 1)

    def proj(off, width):
        return jnp.dot(a_s[...], win_ref[:, off:off + width], preferred_element_type=F32)

    def pool_mixer():
        x1 = pbuf[...]
        s2 = x1 + pltpu.roll(x1, 1, 0)
        s4 = s2 + pltpu.roll(s2, 2, 0)
        s8 = s4 + pltpu.roll(s4, 4, 0)
        s16 = s8 + pltpu.roll(s8, 8, 0)
        lane = lax.broadcasted_iota(jnp.int32, (ts, POOL_WIDTH), 1)
        row = lax.broadcasted_iota(jnp.int32, (ts, POOL_WIDTH), 0)
        win = jnp.where(lane < 64, 2, jnp.where(lane < 128, 4, jnp.where(lane < 192, 8, 16)))
        wsum = jnp.where(lane < 64, s2[POOL_HIST:], jnp.where(
            lane < 128, s4[POOL_HIST:], jnp.where(lane < 192, s8[POOL_HIST:], s16[POOL_HIST:])))
        pos1 = jnp.where(is_meta, row - (META_PAD - 1), max(POOL_WINDOWS))
        cnt = jnp.clip(pos1, 1, win).astype(F32)
        m = (wsum / cnt - x1[POOL_HIST:]).astype(BF16)
        y_pool = jnp.dot(m, poolw_ref[...], preferred_element_type=F32) * pscale_ref[...]
        y_s[:, 0:POOL_WIDTH] = y_pool.astype(BF16)
        pbuf[0:POOL_HIST, :] = x1[ts:ts + POOL_HIST]

    def conv_mixer():
        zz = zbuf[...]
        z1 = pltpu.roll(zz, 1, 0)
        z2 = pltpu.roll(zz, 2, 0)
        cw = convw_ref[...]
        yc = (cw[0:1, :] * z2[CONV_HIST:] + cw[1:2, :] * z1[CONV_HIST:]
              + cw[2:3, :] * zz[CONV_HIST:])
        y_s[:, POOL_WIDTH:POOL_WIDTH + CONV_WIDTH] = (proj(OFF_CB, CONV_WIDTH) * yc).astype(BF16)
        zbuf[0:CONV_HIST, :] = zz[ts:ts + CONV_HIST]

    @pl.when(step >= 0)
    def _():
        a_s[...] = _rms(h_ref[...], pre_ref[...]).astype(BF16)
        g_low = proj(OFF_GL, GL_PAD).astype(BF16)
        gx = jnp.dot(g_low, w2_ref[...], preferred_element_type=F32) + b2_ref[...]
        log_sig = jnp.minimum(gx, 0.0) - jnp.log(1.0 + jnp.exp(-jnp.abs(gx)))
        la = log_sig * (LOG2_E / GLA_TAU)
        la_hi = la.astype(BF16)
        la_lo = (la - la_hi.astype(F32)).astype(BF16)
        la_cat = jnp.concatenate([la_hi, la_lo], axis=1)
        pbuf[POOL_HIST:, :] = proj(OFF_POOL, POOL_WIDTH)
        zbuf[CONV_HIST:, :] = proj(OFF_CC, CONV_WIDTH) * proj(OFF_CU, CONV_WIDTH)
        tril = jnp.where(ri >= ci, 1.0, 0.0).astype(BF16)
        for c in range(nc):
            rows = slice(c * CHUNK, (c + 1) * CHUNK)
            bb = jnp.dot(tril, la_cat[rows, :], preferred_element_type=F32)
            b_s[rows, :] = bb[:, :GLA_KW] + bb[:, GLA_KW:]
        q_s[...] = proj(OFF_Q, GLA_KW) * (GLA_DK ** -0.5)
        k_s[...] = proj(OFF_K, GLA_KW)
        pool_mixer()
        conv_mixer()
        v_s[...] = proj(OFF_V, GLA_WIDTH).astype(BF16)
        og = proj(OFF_OG, GLA_WIDTH)
        gate_s[...] = og * jax.nn.sigmoid(og)

    b, q, k = b_s[...], q_s[...], k_s[...]
    b3 = b.reshape(nc, CHUNK, GLA_KW)
    b_last = b3[:, CHUNK - 1:CHUNK, :]
    qe_s[...] = (q.reshape(b3.shape) * jnp.exp2(b3)).reshape(ts, GLA_KW).astype(BF16)
    kd_s[...] = (k.reshape(b3.shape) * jnp.exp2(b_last - b3)).reshape(ts, GLA_KW).astype(BF16)
    decay = jnp.exp2(b_last)

    nt = (((1,), (1,)), ((), ()))
    tn = (((0,), (0,)), ((), ()))
    ksl = [slice(hd * GLA_DK, (hd + 1) * GLA_DK) for hd in range(GLA_HEADS)]
    vsl = [slice(hd * GLA_DV, (hd + 1) * GLA_DV) for hd in range(GLA_HEADS)]
    pairs = [(c, hd) for c in range(nc) for hd in range(GLA_HEADS)]
    group = 4 * GLA_HEADS
    chunk_rows = lambda c: slice(c * CHUNK, (c + 1) * CHUNK)

    single_ref_ok = jnp.min(b) > -SINGLE_REF_MAX_LOG2

    def state_increments(grp):
        return [lax.dot_general(v_s[chunk_rows(c), vsl[hd]], kd_s[chunk_rows(c), ksl[hd]], tn,
                                preferred_element_type=F32) for c, hd in grp]

    def finish_group(grp, ps, us):
        pvs = [jnp.dot(p, v_s[chunk_rows(c), vsl[hd]], preferred_element_type=F32)
               for p, (c, hd) in zip(ps, grp)]
        for pv, u, (c, hd) in zip(pvs, us, grp):
            o_s[chunk_rows(c), vsl[hd]] = pv
            u_s[c * GLA_HEADS + hd] = u

    @pl.when(single_ref_ok)
    def _():
        lk_s[0] = (k.reshape(b3.shape) * jnp.exp2(-b3)).reshape(ts, GLA_KW).astype(BF16)
        causal = ri >= ci
        for g0 in range(0, len(pairs), group):
            grp = pairs[g0:g0 + group]
            scs = [lax.dot_general(qe_s[chunk_rows(c), ksl[hd]], lk_s[0, chunk_rows(c), ksl[hd]],
                                   nt, preferred_element_type=F32) for c, hd in grp]
            us = state_increments(grp)
            finish_group(grp, [jnp.where(causal, sc, 0.0).astype(BF16) for sc in scs], us)

    lane_hi = lax.broadcasted_iota(jnp.int32, (1, 1, 128), 2) >= GLA_DK
    t64 = lax.broadcasted_iota(jnp.int32, (1, CHUNK, 1), 1)
    blk16, blk4, pos4 = t64 // 16, (t64 // 4) % 4, t64 % 4
    in_hi4 = lax.broadcasted_iota(jnp.int32, (1, 8, 1), 1) >= 4
    neg = -1e30

    def replicate_heads():
        for n, src in enumerate((b_s, q_s, k_s)):
            for c2 in range(2):
                x = src[:, 128 * c2:128 * c2 + 128]
                x_sw = pltpu.roll(x, GLA_DK, 1)
                rep_s[GLA_HEADS * n + 2 * c2] = jnp.where(lane_hi[0], x_sw, x)
                rep_s[GLA_HEADS * n + 2 * c2 + 1] = jnp.where(lane_hi[0], x, x_sw)

    def build_level_operands(hd):
        rb, rq, rk = (rep_s[GLA_HEADS * n + hd] for n in range(3))
        rb64, rq64, rk64 = (x.reshape(nc, CHUNK, 128) for x in (rb, rq, rk))
        rb16 = rb.reshape(ts // 16, 16, 128)
        rb8 = rb.reshape(ts // 8, 8, 128)
        to64 = lambda x: x.reshape(nc, CHUNK, 128)
        end4 = jnp.where(in_hi4, rb8[:, 7:8, :], rb8[:, 3:4, :])
        k1 = rk64 * to64(jnp.exp2(rb16[:, 15:16, :] - rb16))
        k2 = rk64 * to64(jnp.exp2(end4 - rb8))
        nxt = jnp.where(lane_hi, pltpu.roll(rb8, 7, 1), rb8)
        for p in range(2):
            col = slice(128 * p, 128 * p + 128)
            j = jnp.where(lane_hi, 2 * p + 1, 2 * p)
            ref1 = jnp.where(lane_hi, rb64[:, 32 * p + 31:32 * p + 32, :],
                             rb64[:, 32 * p + 15:32 * p + 16, :])
            ref2 = jnp.where(lane_hi, rb16[:, 8 * p + 7:8 * p + 8, :], rb16[:, 8 * p + 3:8 * p + 4, :])
            ref3 = jnp.where(in_hi4, nxt[:, 4 + 2 * p:5 + 2 * p, :], nxt[:, 2 * p:2 * p + 1, :])
            q1 = rq64 * jnp.exp2(jnp.where(blk16 > j, rb64 - ref1, neg))
            q2 = rq64 * jnp.exp2(jnp.where(blk4 > j, to64(rb16 - ref2), neg))
            q3 = rq64 * jnp.exp2(jnp.where(pos4 >= j, to64(rb8 - ref3), neg))
            for n, (ql, kl) in enumerate(((q1, jnp.where(blk16 == j, k1, 0.0)),
                                          (q2, jnp.where(blk4 == j, k2, 0.0)),
                                          (q3, jnp.where(pos4 == j, rk64, 0.0)))):
                lq_s[3 * hd + n, :, col] = ql.reshape(ts, 128).astype(BF16)
                lk_s[3 * hd + n, :, col] = kl.reshape(ts, 128).astype(BF16)

    @pl.when(jnp.logical_not(single_ref_ok))
    def _():
        replicate_heads()
        for hd in range(GLA_HEADS):
            build_level_operands(hd)
        same16 = (ri // 16) == (ci // 16)
        same4 = (ri // 4) == (ci // 4)
        for g0 in range(0, len(pairs), group):
            grp = pairs[g0:g0 + group]
            lvl = [[lax.dot_general(lq_s[3 * hd + n, chunk_rows(c), :],
                                    lk_s[3 * hd + n, chunk_rows(c), :], nt,
                                    preferred_element_type=F32) for n in range(3)]
                   for c, hd in grp]
            us = state_increments(grp)
            finish_group(grp, [(g1 + jnp.where(same16, g2, 0.0)
                                + jnp.where(same4, g3, 0.0)).astype(BF16) for g1, g2, g3 in lvl], us)


    sts = [state_s[hd] for hd in range(GLA_HEADS)]
    for c in range(nc):
        for hd in range(GLA_HEADS):
            sb_s[c * GLA_HEADS + hd] = sts[hd].astype(BF16)
        sts = [sts[hd] * decay[c][:, ksl[hd]] + u_s[c * GLA_HEADS + hd]
               for hd in range(GLA_HEADS)]
    for hd in range(GLA_HEADS):
        state_s[hd] = sts[hd]

    for g0 in range(0, len(pairs), group):
        grp = pairs[g0:g0 + group]
        inters = [lax.dot_general(qe_s[c * CHUNK:(c + 1) * CHUNK, ksl[hd]],
                                  sb_s[c * GLA_HEADS + hd], nt, preferred_element_type=F32)
                  for c, hd in grp]
        for inter, (c, hd) in zip(inters, grp):
            i_s[c * CHUNK:(c + 1) * CHUNK, vsl[hd]] = inter

    for hd in range(GLA_HEADS):
        vs = slice(hd * GLA_DV, (hd + 1) * GLA_DV)
        o = o_s[:, vs] + i_s[:, vs]
        o = o * lax.rsqrt(jnp.mean(o * o, axis=-1, keepdims=True) + EPS) * gnorm_ref[:, vs]
        y_s[:, POOL_WIDTH + CONV_WIDTH + hd * GLA_DV:POOL_WIDTH + CONV_WIDTH + (hd + 1) * GLA_DV] = (
            o * gate_s[:, vs]).astype(BF16)

    mix = jnp.dot(y_s[...], wout_ref[...], preferred_element_type=F32)
    o_ref[...] = h_ref[...] + _rms(mix, post_ref[...])

    @pl.when(is_meta)
    def _():
        pmeta[...] = pbuf[0:POOL_HIST, :]
        zmeta[...] = zbuf[0:CONV_HIST, :]
        smeta[...] = state_s[...]


def _mixer(h, tiles_per_seq, pre, post, win, wout, poolw, pscale, convw, w2, b2, gnorm):
    ts = TILE
    return pl.pallas_call(
        functools.partial(_mixer_kernel, tiles_per_seq=tiles_per_seq),
        grid=(h.shape[0] // ts,),
        in_specs=[
            pl.BlockSpec((ts, D_MODEL), lambda i: (i, 0)),
            _resident((1, D_MODEL)),
            _resident((1, D_MODEL)),
            _resident((D_MODEL, D_PROJ_PACKED)),
            _resident((D_MODEL, D_MODEL)),
            _resident((POOL_WIDTH, POOL_WIDTH)),
            _resident((1, POOL_WIDTH)),
            _resident((8, CONV_WIDTH)),
            _resident((GL_PAD, GLA_KW)),
            _resident((1, GLA_KW)),
            _resident((1, GLA_WIDTH)),
        ],
        out_specs=pl.BlockSpec((ts, D_MODEL), lambda i: (i, 0)),
        out_shape=jax.ShapeDtypeStruct(h.shape, F32),
        scratch_shapes=[
            pltpu.VMEM((ts, D_MODEL), BF16),
            pltpu.VMEM((POOL_HIST + ts, POOL_WIDTH), F32),
            pltpu.VMEM((CONV_HIST + ts, CONV_WIDTH), F32),
            pltpu.VMEM((ts, GLA_KW), F32),
            pltpu.VMEM((ts, GLA_KW), F32),
            pltpu.VMEM((ts, GLA_KW), F32),
            pltpu.VMEM((3 * GLA_HEADS, ts, 128), F32),
            pltpu.VMEM((ts, GLA_KW), BF16),
            pltpu.VMEM((ts, GLA_KW), BF16),
            pltpu.VMEM((3 * GLA_HEADS, ts, 4 * GLA_DK), BF16),
            pltpu.VMEM((3 * GLA_HEADS, ts, 4 * GLA_DK), BF16),
            pltpu.VMEM((ts, GLA_WIDTH), BF16),
            pltpu.VMEM((ts, GLA_WIDTH), F32),
            pltpu.VMEM((ts, GLA_WIDTH), F32),
            pltpu.VMEM((ts, GLA_WIDTH), F32),
            pltpu.VMEM((ts // CHUNK * GLA_HEADS, GLA_DV, GLA_DK), F32),
            pltpu.VMEM((ts // CHUNK * GLA_HEADS, GLA_DV, GLA_DK), BF16),
            pltpu.VMEM((ts, D_MODEL), BF16),
            pltpu.VMEM((GLA_HEADS, GLA_DV, GLA_DK), F32),
            pltpu.VMEM((POOL_HIST, POOL_WIDTH), F32),
            pltpu.VMEM((CONV_HIST, CONV_WIDTH), F32),
            pltpu.VMEM((GLA_HEADS, GLA_DV, GLA_DK), F32),
        ],
        compiler_params=pltpu.CompilerParams(
            dimension_semantics=("arbitrary",), vmem_limit_bytes=VMEM_LIMIT),
        name="mixer",
    )(h, pre, post, win, wout, poolw, pscale, convw, w2, b2, gnorm)


def _pack_w_in(w):
    pool, cb, cc, cu, q, k, v, gl, og = jnp.split(
        w, np.cumsum((256, 256, 256, 256, 256, 256, 512, 16, 512))[:-1].tolist(), axis=-1)
    gl = jnp.pad(gl, ((0, 0), (0, GL_PAD - GLA_GATE_RANK)))
    return jnp.concatenate([pool, cb, cc, cu, q, k, v, og, gl], axis=-1).astype(BF16)


def _block_diag(w):
    out = jnp.zeros((POOL_WIDTH, POOL_WIDTH), w.dtype)
    for g in range(len(POOL_WINDOWS)):
        sl = slice(g * POOL_GROUP, (g + 1) * POOL_GROUP)
        out = out.at[sl, sl].set(w[g])
    return out


def kernel(x, meta, ffn1_pre, ffn1_post, ffn1_wg, ffn1_wu, ffn1_wd, mix_pre, mix_post, w_in,
           pool_w, pool_scale, conv_w, gla_w2, gla_b2, gla_norm, w_out,
           ffn2_pre, ffn2_post, ffn2_wg, ffn2_wu, ffn2_wd):
    n_batch, seq, d = x.shape
    depth = w_in.shape[0]
    assert d == D_MODEL and seq % TILE == 0 and meta.shape[0] == N_META
    meta_tile = jnp.pad(meta.astype(F32), ((META_PAD, 0), (0, 0)))
    h = x.reshape(n_batch * seq, d)
    row = lambda v: v.reshape(1, -1).astype(F32)
    for l in range(depth):
        h = _ffn(h, meta_tile, row(ffn1_pre[l]), row(ffn1_post[l]), ffn1_wg, ffn1_wu, ffn1_wd,
                 mode="first" if l == 0 else "mid", layer=l)
        h = _mixer(
            h, seq // TILE, row(mix_pre[l]), row(mix_post[l]), _pack_w_in(w_in[l]),
            w_out[l].astype(BF16), _block_diag(pool_w[l]).astype(BF16), row(pool_scale[l]),
            jnp.pad(conv_w[l].astype(F32), ((0, 8 - CONV_K), (0, 0))),
            jnp.pad(gla_w2[l], ((0, GL_PAD - GLA_GATE_RANK), (0, 0))).astype(BF16),
            row(gla_b2[l]), row(gla_norm[l]))
        h = _ffn(h, meta_tile, row(ffn2_pre[l]), row(ffn2_post[l]), ffn2_wg, ffn2_wu, ffn2_wd,
                 mode="last" if l == depth - 1 else "mid", layer=l)
    return h.reshape(n_batch, seq, d)
```

```python
import functools

import jax
import jax.numpy as jnp
import numpy as np
from jax import lax
from jax.experimental import pallas as pl
from jax.experimental.pallas import tpu as pltpu

F32 = jnp.float32
BF16 = jnp.bfloat16

D_MODEL = 1024
D_FF = 2816
N_META = 16
EPS = 1e-6
FFN_RESID = 0.5

POOL_WIDTH = 256
POOL_WINDOWS = (2, 4, 8, 16)
POOL_GROUP = 64
CONV_WIDTH = 256
CONV_K = 3
GLA_HEADS = 4
GLA_DK = 64
GLA_DV = 128
GLA_KW = GLA_HEADS * GLA_DK
GLA_WIDTH = GLA_HEADS * GLA_DV
GLA_GATE_RANK = 16
GLA_TAU = 16.0
LOG2_E = 1.4426950408889634
SINGLE_REF_MAX_LOG2 = 100.0
CHUNK = 64


OFF_POOL, OFF_CB, OFF_CC, OFF_CU, OFF_Q, OFF_K = 0, 256, 512, 768, 1024, 1280
OFF_V, OFF_OG, OFF_GL = 1536, 2048, 2560
GL_PAD = 128
D_PROJ_PACKED = OFF_GL + GL_PAD

V7X_VMEM_BYTES = 64 * 1024 * 1024
VMEM_LIMIT = 56 * 1024 * 1024

TILE = 512
FF_CHUNK = 256
META_PAD = TILE - N_META
META_ROWS = 64
POOL_HIST = 16
CONV_HIST = 8


def _rms(x, g):
    return x * lax.rsqrt(jnp.mean(x * x, axis=-1, keepdims=True) + EPS) * g


def _load_ffn_weights(layer, wg_hbm, wu_hbm, wd_hbm, wg_ref, wu_ref, wd_ref, stage_cols,
                      stage_rows, sems, on_chunk):
    n_chunks = D_FF // FF_CHUNK

    def copies(j, slot):
        off = pl.multiple_of(j * FF_CHUNK, FF_CHUNK)
        return (
            pltpu.make_async_copy(wg_hbm.at[layer, :, pl.ds(off, FF_CHUNK)],
                                  stage_cols.at[0, slot], sems.at[0, slot]),
            pltpu.make_async_copy(wu_hbm.at[layer, :, pl.ds(off, FF_CHUNK)],
                                  stage_cols.at[1, slot], sems.at[1, slot]),
            pltpu.make_async_copy(wd_hbm.at[layer, pl.ds(off, FF_CHUNK), :],
                                  stage_rows.at[slot], sems.at[2, slot]),
        )

    for cp in copies(0, 0):
        cp.start()

    def body(j, carry):
        slot = lax.rem(j, 2)

        @pl.when(j + 1 < n_chunks)
        def _():
            for cp in copies(j + 1, 1 - slot):
                cp.start()

        for cp in copies(j, slot):
            cp.wait()
        off = pl.multiple_of(j * FF_CHUNK, FF_CHUNK)
        wg_ref[:, pl.ds(off, FF_CHUNK)] = stage_cols[0, slot].astype(BF16)
        wu_ref[:, pl.ds(off, FF_CHUNK)] = stage_cols[1, slot].astype(BF16)
        wd_ref[pl.ds(off, FF_CHUNK), :] = stage_rows[slot].astype(BF16)
        on_chunk(off)
        return carry

    lax.fori_loop(0, n_chunks, body, 0)


def _ffn_kernel(h_ref, meta_ref, pre_ref, post_ref, wg_hbm, wu_hbm, wd_hbm, o_ref,
                a_ref, z_ref, wg_ref, wu_ref, wd_ref, stage_cols, stage_rows, sems, *,
                step0, layer):
    step = pl.program_id(0)

    def hidden_chunk(rows, cols):
        a = a_ref[rows, :]
        g = jnp.dot(a, wg_ref[:, cols], preferred_element_type=F32)
        u = jnp.dot(a, wu_ref[:, cols], preferred_element_type=F32)
        z_ref[rows, cols] = (g * jax.nn.sigmoid(g) * u).astype(BF16)

    def finish(rows, h):
        f = jnp.dot(z_ref[rows, :], wd_ref[...], preferred_element_type=F32)
        o_ref[rows, :] = h + _rms(f, FFN_RESID * post_ref[...])

    @pl.when(step == 0)
    def _():
        rows = slice(0, TILE) if step0 == "tokens" else slice(TILE - META_ROWS, TILE)
        h = (meta_ref if step0 == "meta_input" else h_ref)[rows, :]
        a_ref[rows, :] = _rms(h, pre_ref[...]).astype(BF16)
        _load_ffn_weights(layer, wg_hbm, wu_hbm, wd_hbm, wg_ref, wu_ref, wd_ref, stage_cols,
                          stage_rows, sems,
                          lambda off: hidden_chunk(rows, pl.ds(off, FF_CHUNK)))
        if step0 != "tokens":
            o_ref[0:TILE - META_ROWS, :] = jnp.zeros((TILE - META_ROWS, D_MODEL), F32)
        finish(rows, h)

    @pl.when(step > 0)
    def _():
        h = h_ref[...]
        a_ref[...] = _rms(h, pre_ref[...]).astype(BF16)
        for j in range(D_FF // FF_CHUNK):
            hidden_chunk(slice(0, TILE), slice(j * FF_CHUNK, (j + 1) * FF_CHUNK))
        finish(slice(0, TILE), h)


def _resident(shape):
    return pl.BlockSpec(shape, lambda *_: (0,) * len(shape), pipeline_mode=pl.Buffered(1))


def _ffn(src, meta_tile, pre, post, wg, wu, wd, *, mode, layer):
    tile = TILE
    n_tiles = src.shape[0] // tile
    assert src.shape[0] % tile == 0
    if mode == "first":
        grid, src_map = n_tiles + 1, lambda i: (jnp.maximum(i - 1, 0), 0)
    elif mode == "mid":
        grid, src_map = n_tiles, lambda i: (i, 0)
    else:
        grid, src_map = n_tiles - 1, lambda i: (i + 1, 0)
    out_rows = grid * tile
    return pl.pallas_call(
        functools.partial(
            _ffn_kernel, layer=layer,
            step0={"first": "meta_input", "mid": "meta_block", "last": "tokens"}[mode]),
        grid=(grid,),
        in_specs=[
            pl.BlockSpec((tile, D_MODEL), src_map),
            _resident((TILE, D_MODEL)),
            _resident((1, D_MODEL)),
            _resident((1, D_MODEL)),
            pl.BlockSpec(memory_space=pl.ANY),
            pl.BlockSpec(memory_space=pl.ANY),
            pl.BlockSpec(memory_space=pl.ANY),
        ],
        out_specs=pl.BlockSpec((tile, D_MODEL), lambda i: (i, 0)),
        out_shape=jax.ShapeDtypeStruct((out_rows, D_MODEL), F32),
        scratch_shapes=[
            pltpu.VMEM((tile, D_MODEL), BF16),
            pltpu.VMEM((tile, D_FF), BF16),
            pltpu.VMEM((D_MODEL, D_FF), BF16),
            pltpu.VMEM((D_MODEL, D_FF), BF16),
            pltpu.VMEM((D_FF, D_MODEL), BF16),
            pltpu.VMEM((2, 2, D_MODEL, FF_CHUNK), F32),
            pltpu.VMEM((2, FF_CHUNK, D_MODEL), F32),
            pltpu.SemaphoreType.DMA((3, 2)),
        ],
        compiler_params=pltpu.CompilerParams(
            dimension_semantics=("arbitrary",), vmem_limit_bytes=VMEM_LIMIT),
        name="ffn_" + mode,
    )(src, meta_tile, pre, post, wg, wu, wd)


def _mixer_kernel(h_ref, pre_ref, post_ref, win_ref, wout_ref, poolw_ref, pscale_ref,
                  convw_ref, w2_ref, b2_ref, gnorm_ref, o_ref,
                  a_s, pbuf, zbuf, b_s, q_s, k_s, rep_s, qe_s, kd_s, lq_s, lk_s, v_s, gate_s, o_s, i_s, u_s, sb_s, y_s, state_s,
                  pmeta, zmeta, smeta, *, tiles_per_seq):
    step = pl.program_id(0)
    ts = TILE
    is_meta = step == 0

    @pl.when(is_meta)
    def _():
        pbuf[0:POOL_HIST, :] = jnp.zeros((POOL_HIST, POOL_WIDTH), F32)
        zbuf[0:CONV_HIST, :] = jnp.zeros((CONV_HIST, CONV_WIDTH), F32)
        state_s[...] = jnp.zeros_like(state_s)

    @pl.when(jnp.logical_and(step > 0, lax.rem(step - 1, tiles_per_seq) == 0))
    def _():
        pbuf[0:POOL_HIST, :] = pmeta[...]
        zbuf[0:CONV_HIST, :] = zmeta[...]
        state_s[...] = smeta[...]

    nc = ts // CHUNK
    ri = lax.broadcasted_iota(jnp.int32, (CHUNK, CHUNK), 0)
    ci = lax.broadcasted_iota(jnp.int32, (CHUNK, CHUNK), 1)

    def proj(off, width):
        return jnp.dot(a_s[...], win_ref[:, off:off + width], preferred_element_type=F32)

    def pool_mixer():
        x1 = pbuf[...]
        s2 = x1 + pltpu.roll(x1, 1, 0)
        s4 = s2 + pltpu.roll(s2, 2, 0)
        s8 = s4 + pltpu.roll(s4, 4, 0)
        s16 = s8 + pltpu.roll(s8, 8, 0)
        lane = lax.broadcasted_iota(jnp.int32, (ts, POOL_WIDTH), 1)
        row = lax.broadcasted_iota(jnp.int32, (ts, POOL_WIDTH), 0)
        win = jnp.where(lane < 64, 2, jnp.where(lane < 128, 4, jnp.where(lane < 192, 8, 16)))
        wsum = jnp.where(lane < 64, s2[POOL_HIST:], jnp.where(
            lane < 128, s4[POOL_HIST:], jnp.where(lane < 192, s8[POOL_HIST:], s16[POOL_HIST:])))
        pos1 = jnp.where(is_meta, row - (META_PAD - 1), max(POOL_WINDOWS))
        cnt = jnp.clip(pos1, 1, win).astype(F32)
        m = (wsum / cnt - x1[POOL_HIST:]).astype(BF16)
        y_pool = jnp.dot(m, poolw_ref[...], preferred_element_type=F32) * pscale_ref[...]
        y_s[:, 0:POOL_WIDTH] = y_pool.astype(BF16)
        pbuf[0:POOL_HIST, :] = x1[ts:ts + POOL_HIST]

    def conv_mixer():
        zz = zbuf[...]
        z1 = pltpu.roll(zz, 1, 0)
        z2 = pltpu.roll(zz, 2, 0)
        cw = convw_ref[...]
        yc = (cw[0:1, :] * z2[CONV_HIST:] + cw[1:2, :] * z1[CONV_HIST:]
              + cw[2:3, :] * zz[CONV_HIST:])
        y_s[:, POOL_WIDTH:POOL_WIDTH + CONV_WIDTH] = (proj(OFF_CB, CONV_WIDTH) * yc).astype(BF16)
        zbuf[0:CONV_HIST, :] = zz[ts:ts + CONV_HIST]

    @pl.when(step >= 0)
    def _():
        a_s[...] = _rms(h_ref[...], pre_ref[...]).astype(BF16)
        g_low = proj(OFF_GL, GL_PAD).astype(BF16)
        gx = jnp.dot(g_low, w2_ref[...], preferred_element_type=F32) + b2_ref[...]
        log_sig = jnp.minimum(gx, 0.0) - jnp.log(1.0 + jnp.exp(-jnp.abs(gx)))
        la = log_sig * (LOG2_E / GLA_TAU)
        la_hi = la.astype(BF16)
        la_lo = (la - la_hi.astype(F32)).astype(BF16)
        la_cat = jnp.concatenate([la_hi, la_lo], axis=1)
        pbuf[POOL_HIST:, :] = proj(OFF_POOL, POOL_WIDTH)
        zbuf[CONV_HIST:, :] = proj(OFF_CC, CONV_WIDTH) * proj(OFF_CU, CONV_WIDTH)
        tril = jnp.where(ri >= ci, 1.0, 0.0).astype(BF16)
        for c in range(nc):
            rows = slice(c * CHUNK, (c + 1) * CHUNK)
            bb = jnp.dot(tril, la_cat[rows, :], preferred_element_type=F32)
            b_s[rows, :] = bb[:, :GLA_KW] + bb[:, GLA_KW:]
        q_s[...] = proj(OFF_Q, GLA_KW) * (GLA_DK ** -0.5)
        k_s[...] = proj(OFF_K, GLA_KW)
        pool_mixer()
        conv_mixer()
        v_s[...] = proj(OFF_V, GLA_WIDTH).astype(BF16)
        og = proj(OFF_OG, GLA_WIDTH)
        gate_s[...] = og * jax.nn.sigmoid(og)

    b, q, k = b_s[...], q_s[...], k_s[...]
    b3 = b.reshape(nc, CHUNK, GLA_KW)
    b_last = b3[:, CHUNK - 1:CHUNK, :]
    qe_s[...] = (q.reshape(b3.shape) * jnp.exp2(b3)).reshape(ts, GLA_KW).astype(BF16)
    kd_s[...] = (k.reshape(b3.shape) * jnp.exp2(b_last - b3)).reshape(ts, GLA_KW).astype(BF16)
    decay = jnp.exp2(b_last)

    nt = (((1,), (1,)), ((), ()))
    tn = (((0,), (0,)), ((), ()))
    ksl = [slice(hd * GLA_DK, (hd + 1) * GLA_DK) for hd in range(GLA_HEADS)]
    vsl = [slice(hd * GLA_DV, (hd + 1) * GLA_DV) for hd in range(GLA_HEADS)]
    pairs = [(c, hd) for c in range(nc) for hd in range(GLA_HEADS)]
    group = 2 * GLA_HEADS
    chunk_rows = lambda c: slice(c * CHUNK, (c + 1) * CHUNK)

    single_ref_ok = jnp.min(b) > -SINGLE_REF_MAX_LOG2

    def state_increments(grp):
        return [lax.dot_general(v_s[chunk_rows(c), vsl[hd]], kd_s[chunk_rows(c), ksl[hd]], tn,
                                preferred_element_type=F32) for c, hd in grp]

    def finish_group(grp, ps, us):
        pvs = [jnp.dot(p, v_s[chunk_rows(c), vsl[hd]], preferred_element_type=F32)
               for p, (c, hd) in zip(ps, grp)]
        for pv, u, (c, hd) in zip(pvs, us, grp):
            o_s[chunk_rows(c), vsl[hd]] = pv
            u_s[c * GLA_HEADS + hd] = u

    @pl.when(single_ref_ok)
    def _():
        lk_s[0] = (k.reshape(b3.shape) * jnp.exp2(-b3)).reshape(ts, GLA_KW).astype(BF16)
        causal = ri >= ci
        for g0 in range(0, len(pairs), group):
            grp = pairs[g0:g0 + group]
            scs = [lax.dot_general(qe_s[chunk_rows(c), ksl[hd]], lk_s[0, chunk_rows(c), ksl[hd]],
                                   nt, preferred_element_type=F32) for c, hd in grp]
            us = state_increments(grp)
            finish_group(grp, [jnp.where(causal, sc, 0.0).astype(BF16) for sc in scs], us)

    lane_hi = lax.broadcasted_iota(jnp.int32, (1, 1, 128), 2) >= GLA_DK
    t64 = lax.broadcasted_iota(jnp.int32, (1, CHUNK, 1), 1)
    blk16, blk4, pos4 = t64 // 16, (t64 // 4) % 4, t64 % 4
    in_hi4 = lax.broadcasted_iota(jnp.int32, (1, 8, 1), 1) >= 4
    neg = -1e30

    def replicate_heads():
        for n, src in enumerate((b_s, q_s, k_s)):
            for c2 in range(2):
                x = src[:, 128 * c2:128 * c2 + 128]
                x_sw = pltpu.roll(x, GLA_DK, 1)
                rep_s[GLA_HEADS * n + 2 * c2] = jnp.where(lane_hi[0], x_sw, x)
                rep_s[GLA_HEADS * n + 2 * c2 + 1] = jnp.where(lane_hi[0], x, x_sw)

    def build_level_operands(hd):
        rb, rq, rk = (rep_s[GLA_HEADS * n + hd] for n in range(3))
        rb64, rq64, rk64 = (x.reshape(nc, CHUNK, 128) for x in (rb, rq, rk))
        rb16 = rb.reshape(ts // 16, 16, 128)
        rb8 = rb.reshape(ts // 8, 8, 128)
        to64 = lambda x: x.reshape(nc, CHUNK, 128)
        end4 = jnp.where(in_hi4, rb8[:, 7:8, :], rb8[:, 3:4, :])
        k1 = rk64 * to64(jnp.exp2(rb16[:, 15:16, :] - rb16))
        k2 = rk64 * to64(jnp.exp2(end4 - rb8))
        nxt = jnp.where(lane_hi, pltpu.roll(rb8, 7, 1), rb8)
        for p in range(2):
            col = slice(128 * p, 128 * p + 128)
            j = jnp.where(lane_hi, 2 * p + 1, 2 * p)
            ref1 = jnp.where(lane_hi, rb64[:, 32 * p + 31:32 * p + 32, :],
                             rb64[:, 32 * p + 15:32 * p + 16, :])
            ref2 = jnp.where(lane_hi, rb16[:, 8 * p + 7:8 * p + 8, :], rb16[:, 8 * p + 3:8 * p + 4, :])
            ref3 = jnp.where(in_hi4, nxt[:, 4 + 2 * p:5 + 2 * p, :], nxt[:, 2 * p:2 * p + 1, :])
            q1 = rq64 * jnp.exp2(jnp.where(blk16 > j, rb64 - ref1, neg))
            q2 = rq64 * jnp.exp2(jnp.where(blk4 > j, to64(rb16 - ref2), neg))
            q3 = rq64 * jnp.exp2(jnp.where(pos4 >= j, to64(rb8 - ref3), neg))
            for n, (ql, kl) in enumerate(((q1, jnp.where(blk16 == j, k1, 0.0)),
                                          (q2, jnp.where(blk4 == j, k2, 0.0)),
                                          (q3, jnp.where(pos4 == j, rk64, 0.0)))):
                lq_s[3 * hd + n, :, col] = ql.reshape(ts, 128).astype(BF16)
                lk_s[3 * hd + n, :, col] = kl.reshape(ts, 128).astype(BF16)

    @pl.when(jnp.logical_not(single_ref_ok))
    def _():
        replicate_heads()
        for hd in range(GLA_HEADS):
            build_level_operands(hd)
        same16 = (ri // 16) == (ci // 16)
        same4 = (ri // 4) == (ci // 4)
        for g0 in range(0, len(pairs), group):
            grp = pairs[g0:g0 + group]
            lvl = [[lax.dot_general(lq_s[3 * hd + n, chunk_rows(c), :],
                                    lk_s[3 * hd + n, chunk_rows(c), :], nt,
                                    preferred_element_type=F32) for n in range(3)]
                   for c, hd in grp]
            us = state_increments(grp)
            finish_group(grp, [(g1 + jnp.where(same16, g2, 0.0)
                                + jnp.where(same4, g3, 0.0)).astype(BF16) for g1, g2, g3 in lvl], us)


    sts = [state_s[hd] for hd in range(GLA_HEADS)]
    for c in range(nc):
        for hd in range(GLA_HEADS):
            sb_s[c * GLA_HEADS + hd] = sts[hd].astype(BF16)
        sts = [sts[hd] * decay[c][:, ksl[hd]] + u_s[c * GLA_HEADS + hd]
               for hd in range(GLA_HEADS)]
    for hd in range(GLA_HEADS):
        state_s[hd] = sts[hd]

    for g0 in range(0, len(pairs), group):
        grp = pairs[g0:g0 + group]
        inters = [lax.dot_general(qe_s[c * CHUNK:(c + 1) * CHUNK, ksl[hd]],
                                  sb_s[c * GLA_HEADS + hd], nt, preferred_element_type=F32)
                  for c, hd in grp]
        for inter, (c, hd) in zip(inters, grp):
            i_s[c * CHUNK:(c + 1) * CHUNK, vsl[hd]] = inter

    for hd in range(GLA_HEADS):
        vs = slice(hd * GLA_DV, (hd + 1) * GLA_DV)
        o = o_s[:, vs] + i_s[:, vs]
        o = o * lax.rsqrt(jnp.mean(o * o, axis=-1, keepdims=True) + EPS) * gnorm_ref[:, vs]
        y_s[:, POOL_WIDTH + CONV_WIDTH + hd * GLA_DV:POOL_WIDTH + CONV_WIDTH + (hd + 1) * GLA_DV] = (
            o * gate_s[:, vs]).astype(BF16)

    mix = jnp.dot(y_s[...], wout_ref[...], preferred_element_type=F32)
    o_ref[...] = h_ref[...] + _rms(mix, post_ref[...])

    @pl.when(is_meta)
    def _():
        pmeta[...] = pbuf[0:POOL_HIST, :]
        zmeta[...] = zbuf[0:CONV_HIST, :]
        smeta[...] = state_s[...]


def _mixer(h, tiles_per_seq, pre, post, win, wout, poolw, pscale, convw, w2, b2, gnorm):
    ts = TILE
    return pl.pallas_call(
        functools.partial(_mixer_kernel, tiles_per_seq=tiles_per_seq),
        grid=(h.shape[0] // ts,),
        in_specs=[
            pl.BlockSpec((ts, D_MODEL), lambda i: (i, 0)),
            _resident((1, D_MODEL)),
            _resident((1, D_MODEL)),
            _resident((D_MODEL, D_PROJ_PACKED)),
            _resident((D_MODEL, D_MODEL)),
            _resident((POOL_WIDTH, POOL_WIDTH)),
            _resident((1, POOL_WIDTH)),
            _resident((8, CONV_WIDTH)),
            _resident((GL_PAD, GLA_KW)),
            _resident((1, GLA_KW)),
            _resident((1, GLA_WIDTH)),
        ],
        out_specs=pl.BlockSpec((ts, D_MODEL), lambda i: (i, 0)),
        out_shape=jax.ShapeDtypeStruct(h.shape, F32),
        scratch_shapes=[
            pltpu.VMEM((ts, D_MODEL), BF16),
            pltpu.VMEM((POOL_HIST + ts, POOL_WIDTH), F32),
            pltpu.VMEM((CONV_HIST + ts, CONV_WIDTH), F32),
            pltpu.VMEM((ts, GLA_KW), F32),
            pltpu.VMEM((ts, GLA_KW), F32),
            pltpu.VMEM((ts, GLA_KW), F32),
            pltpu.VMEM((3 * GLA_HEADS, ts, 128), F32),
            pltpu.VMEM((ts, GLA_KW), BF16),
            pltpu.VMEM((ts, GLA_KW), BF16),
            pltpu.VMEM((3 * GLA_HEADS, ts, 4 * GLA_DK), BF16),
            pltpu.VMEM((3 * GLA_HEADS, ts, 4 * GLA_DK), BF16),
            pltpu.VMEM((ts, GLA_WIDTH), BF16),
            pltpu.VMEM((ts, GLA_WIDTH), F32),
            pltpu.VMEM((ts, GLA_WIDTH), F32),
            pltpu.VMEM((ts, GLA_WIDTH), F32),
            pltpu.VMEM((ts // CHUNK * GLA_HEADS, GLA_DV, GLA_DK), F32),
            pltpu.VMEM((ts // CHUNK * GLA_HEADS, GLA_DV, GLA_DK), BF16),
            pltpu.VMEM((ts, D_MODEL), BF16),
            pltpu.VMEM((GLA_HEADS, GLA_DV, GLA_DK), F32),
            pltpu.VMEM((POOL_HIST, POOL_WIDTH), F32),
            pltpu.VMEM((CONV_HIST, CONV_WIDTH), F32),
            pltpu.VMEM((GLA_HEADS, GLA_DV, GLA_DK), F32),
        ],
        compiler_params=pltpu.CompilerParams(
            dimension_semantics=("arbitrary",), vmem_limit_bytes=VMEM_LIMIT),
        name="mixer",
    )(h, pre, post, win, wout, poolw, pscale, convw, w2, b2, gnorm)


def _pack_w_in(w):
    pool, cb, cc, cu, q, k, v, gl, og = jnp.split(
        w, np.cumsum((256, 256, 256, 256, 256, 256, 512, 16, 512))[:-1].tolist(), axis=-1)
    gl = jnp.pad(gl, ((0, 0), (0, GL_PAD - GLA_GATE_RANK)))
    return jnp.concatenate([pool, cb, cc, cu, q, k, v, og, gl], axis=-1).astype(BF16)


def _block_diag(w):
    out = jnp.zeros((POOL_WIDTH, POOL_WIDTH), w.dtype)
    for g in range(len(POOL_WINDOWS)):
        sl = slice(g * POOL_GROUP, (g + 1) * POOL_GROUP)
        out = out.at[sl, sl].set(w[g])
    return out


def kernel(x, meta, ffn1_pre, ffn1_post, ffn1_wg, ffn1_wu, ffn1_wd, mix_pre, mix_post, w_in,
           pool_w, pool_scale, conv_w, gla_w2, gla_b2, gla_norm, w_out,
           ffn2_pre, ffn2_post, ffn2_wg, ffn2_wu, ffn2_wd):
    n_batch, seq, d = x.shape
    depth = w_in.shape[0]
    assert d == D_MODEL and seq % TILE == 0 and meta.shape[0] == N_META
    meta_tile = jnp.pad(meta.astype(F32), ((META_PAD, 0), (0, 0)))
    h = x.reshape(n_batch * seq, d)
    row = lambda v: v.reshape(1, -1).astype(F32)
    for l in range(depth):
        h = _ffn(h, meta_tile, row(ffn1_pre[l]), row(ffn1_post[l]), ffn1_wg, ffn1_wu, ffn1_wd,
                 mode="first" if l == 0 else "mid", layer=l)
        h = _mixer(
            h, seq // TILE, row(mix_pre[l]), row(mix_post[l]), _pack_w_in(w_in[l]),
            w_out[l].astype(BF16), _block_diag(pool_w[l]).astype(BF16), row(pool_scale[l]),
            jnp.pad(conv_w[l].astype(F32), ((0, 8 - CONV_K), (0, 0))),
            jnp.pad(gla_w2[l], ((0, GL_PAD - GLA_GATE_RANK), (0, 0))).astype(BF16),
            row(gla_b2[l]), row(gla_norm[l]))
        h = _ffn(h, meta_tile, row(ffn2_pre[l]), row(ffn2_post[l]), ffn2_wg, ffn2_wu, ffn2_wd,
                 mode="last" if l == depth - 1 else "mid", layer=l)
    return h.reshape(n_batch, seq, d)
```

```python
import functools

import jax
import jax.numpy as jnp
import numpy as np
from jax import lax
from jax.experimental import pallas as pl
from jax.experimental.pallas import tpu as pltpu

F32 = jnp.float32
BF16 = jnp.bfloat16

D_MODEL = 1024
D_FF = 2816
N_META = 16
EPS = 1e-6
FFN_RESID = 0.5

POOL_WIDTH = 256
POOL_WINDOWS = (2, 4, 8, 16)
POOL_GROUP = 64
CONV_WIDTH = 256
CONV_K = 3
GLA_HEADS = 4
GLA_DK = 64
GLA_DV = 128
GLA_KW = GLA_HEADS * GLA_DK
GLA_WIDTH = GLA_HEADS * GLA_DV
GLA_GATE_RANK = 16
GLA_TAU = 16.0
LOG2_E = 1.4426950408889634
SINGLE_REF_MAX_LOG2 = 100.0
CHUNK = 64


OFF_POOL, OFF_CB, OFF_CC, OFF_CU, OFF_Q, OFF_K = 0, 256, 512, 768, 1024, 1280
OFF_V, OFF_OG, OFF_GL = 1536, 2048, 2560
GL_PAD = 128
D_PROJ_PACKED = OFF_GL + GL_PAD

V7X_VMEM_BYTES = 64 * 1024 * 1024
VMEM_LIMIT = 56 * 1024 * 1024

TILE = 512
FF_CHUNK = 256
META_PAD = TILE - N_META
META_ROWS = 64
POOL_HIST = 16
CONV_HIST = 8


def _rms(x, g):
    return x * lax.rsqrt(jnp.mean(x * x, axis=-1, keepdims=True) + EPS) * g


def _load_ffn_weights(layer, wg_hbm, wu_hbm, wd_hbm, wg_ref, wu_ref, wd_ref, stage_cols,
                      stage_rows, sems, on_chunk):
    n_chunks = D_FF // FF_CHUNK

    def copies(j, slot):
        off = pl.multiple_of(j * FF_CHUNK, FF_CHUNK)
        return (
            pltpu.make_async_copy(wg_hbm.at[layer, :, pl.ds(off, FF_CHUNK)],
                                  stage_cols.at[0, slot], sems.at[0, slot]),
            pltpu.make_async_copy(wu_hbm.at[layer, :, pl.ds(off, FF_CHUNK)],
                                  stage_cols.at[1, slot], sems.at[1, slot]),
            pltpu.make_async_copy(wd_hbm.at[layer, pl.ds(off, FF_CHUNK), :],
                                  stage_rows.at[slot], sems.at[2, slot]),
        )

    for cp in copies(0, 0):
        cp.start()

    def body(j, carry):
        slot = lax.rem(j, 2)

        @pl.when(j + 1 < n_chunks)
        def _():
            for cp in copies(j + 1, 1 - slot):
                cp.start()

        for cp in copies(j, slot):
            cp.wait()
        off = pl.multiple_of(j * FF_CHUNK, FF_CHUNK)
        wg_ref[:, pl.ds(off, FF_CHUNK)] = stage_cols[0, slot].astype(BF16)
        wu_ref[:, pl.ds(off, FF_CHUNK)] = stage_cols[1, slot].astype(BF16)
        wd_ref[pl.ds(off, FF_CHUNK), :] = stage_rows[slot].astype(BF16)
        on_chunk(off)
        return carry

    lax.fori_loop(0, n_chunks, body, 0)


def _ffn_kernel(h_ref, meta_ref, pre_ref, post_ref, wg_hbm, wu_hbm, wd_hbm, o_ref,
                a_ref, z_ref, wg_ref, wu_ref, wd_ref, stage_cols, stage_rows, sems, *,
                step0, layer):
    step = pl.program_id(0)

    def hidden_chunk(rows, cols):
        a = a_ref[rows, :]
        g = jnp.dot(a, wg_ref[:, cols], preferred_element_type=F32)
        u = jnp.dot(a, wu_ref[:, cols], preferred_element_type=F32)
        z_ref[rows, cols] = (g * jax.nn.sigmoid(g) * u).astype(BF16)

    def finish(rows, h):
        f = jnp.dot(z_ref[rows, :], wd_ref[...], preferred_element_type=F32)
        o_ref[rows, :] = h + _rms(f, FFN_RESID * post_ref[...])

    @pl.when(step == 0)
    def _():
        rows = slice(0, TILE) if step0 == "tokens" else slice(TILE - META_ROWS, TILE)
        h = (meta_ref if step0 == "meta_input" else h_ref)[rows, :]
        a_ref[rows, :] = _rms(h, pre_ref[...]).astype(BF16)
        _load_ffn_weights(layer, wg_hbm, wu_hbm, wd_hbm, wg_ref, wu_ref, wd_ref, stage_cols,
                          stage_rows, sems,
                          lambda off: hidden_chunk(rows, pl.ds(off, FF_CHUNK)))
        if step0 != "tokens":
            o_ref[0:TILE - META_ROWS, :] = jnp.zeros((TILE - META_ROWS, D_MODEL), F32)
        finish(rows, h)

    @pl.when(step > 0)
    def _():
        h = h_ref[...]
        a_ref[...] = _rms(h, pre_ref[...]).astype(BF16)
        for j in range(D_FF // FF_CHUNK):
            hidden_chunk(slice(0, TILE), slice(j * FF_CHUNK, (j + 1) * FF_CHUNK))
        finish(slice(0, TILE), h)


def _resident(shape):
    return pl.BlockSpec(shape, lambda *_: (0,) * len(shape), pipeline_mode=pl.Buffered(1))


def _ffn(src, meta_tile, pre, post, wg, wu, wd, *, mode, layer):
    tile = TILE
    n_tiles = src.shape[0] // tile
    assert src.shape[0] % tile == 0
    if mode == "first":
        grid, src_map = n_tiles + 1, lambda i: (jnp.maximum(i - 1, 0), 0)
    elif mode == "mid":
        grid, src_map = n_tiles, lambda i: (i, 0)
    else:
        grid, src_map = n_tiles - 1, lambda i: (i + 1, 0)
    out_rows = grid * tile
    return pl.pallas_call(
        functools.partial(
            _ffn_kernel, layer=layer,
            step0={"first": "meta_input", "mid": "meta_block", "last": "tokens"}[mode]),
        grid=(grid,),
        in_specs=[
            pl.BlockSpec((tile, D_MODEL), src_map),
            _resident((TILE, D_MODEL)),
            _resident((1, D_MODEL)),
            _resident((1, D_MODEL)),
            pl.BlockSpec(memory_space=pl.ANY),
            pl.BlockSpec(memory_space=pl.ANY),
            pl.BlockSpec(memory_space=pl.ANY),
        ],
        out_specs=pl.BlockSpec((tile, D_MODEL), lambda i: (i, 0)),
        out_shape=jax.ShapeDtypeStruct((out_rows, D_MODEL), F32),
        scratch_shapes=[
            pltpu.VMEM((tile, D_MODEL), BF16),
            pltpu.VMEM((tile, D_FF), BF16),
            pltpu.VMEM((D_MODEL, D_FF), BF16),
            pltpu.VMEM((D_MODEL, D_FF), BF16),
            pltpu.VMEM((D_FF, D_MODEL), BF16),
            pltpu.VMEM((2, 2, D_MODEL, FF_CHUNK), F32),
            pltpu.VMEM((2, FF_CHUNK, D_MODEL), F32),
            pltpu.SemaphoreType.DMA((3, 2)),
        ],
        compiler_params=pltpu.CompilerParams(
            dimension_semantics=("arbitrary",), vmem_limit_bytes=VMEM_LIMIT),
        name="ffn_" + mode,
    )(src, meta_tile, pre, post, wg, wu, wd)


def _mixer_kernel(h_ref, pre_ref, post_ref, win_ref, wout_ref, poolw_ref, pscale_ref,
                  convw_ref, w2_ref, b2_ref, gnorm_ref, o_ref,
                  a_s, pbuf, zbuf, b_s, q_s, k_s, rep_s, qe_s, kd_s, lq_s, lk_s, v_s, gate_s, o_s, u_s, sb_s, y_s, state_s,
                  pmeta, zmeta, smeta, bmin_s, *, tiles_per_seq):
    step = pl.program_id(0)
    ts = TILE
    is_meta = step == 0

    @pl.when(is_meta)
    def _():
        pbuf[0:POOL_HIST, :] = jnp.zeros((POOL_HIST, POOL_WIDTH), F32)
        zbuf[0:CONV_HIST, :] = jnp.zeros((CONV_HIST, CONV_WIDTH), F32)
        state_s[...] = jnp.zeros_like(state_s)

    @pl.when(jnp.logical_and(step > 0, lax.rem(step - 1, tiles_per_seq) == 0))
    def _():
        pbuf[0:POOL_HIST, :] = pmeta[...]
        zbuf[0:CONV_HIST, :] = zmeta[...]
        state_s[...] = smeta[...]

    nc = ts // CHUNK
    ri = lax.broadcasted_iota(jnp.int32, (CHUNK, CHUNK), 0)
    ci = lax.broadcasted_iota(jnp.int32, (CHUNK, CHUNK), 1)

    def proj(off, width):
        return jnp.dot(a_s[...], win_ref[:, off:off + width], preferred_element_type=F32)

    def pool_mixer():
        x1 = pbuf[...]
        s2 = x1 + pltpu.roll(x1, 1, 0)
        s4 = s2 + pltpu.roll(s2, 2, 0)
        s8 = s4 + pltpu.roll(s4, 4, 0)
        s16 = s8 + pltpu.roll(s8, 8, 0)
        lane = lax.broadcasted_iota(jnp.int32, (ts, POOL_WIDTH), 1)
        row = lax.broadcasted_iota(jnp.int32, (ts, POOL_WIDTH), 0)
        win = jnp.where(lane < 64, 2, jnp.where(lane < 128, 4, jnp.where(lane < 192, 8, 16)))
        wsum = jnp.where(lane < 64, s2[POOL_HIST:], jnp.where(
            lane < 128, s4[POOL_HIST:], jnp.where(lane < 192, s8[POOL_HIST:], s16[POOL_HIST:])))
        pos1 = jnp.where(is_meta, row - (META_PAD - 1), max(POOL_WINDOWS))
        cnt = jnp.clip(pos1, 1, win).astype(F32)
        m = (wsum / cnt - x1[POOL_HIST:]).astype(BF16)
        y_pool = jnp.dot(m, poolw_ref[...], preferred_element_type=F32) * pscale_ref[...]
        y_s[:, 0:POOL_WIDTH] = y_pool.astype(BF16)
        pbuf[0:POOL_HIST, :] = x1[ts:ts + POOL_HIST]

    def conv_mixer():
        zz = zbuf[...]
        z1 = pltpu.roll(zz, 1, 0)
        z2 = pltpu.roll(zz, 2, 0)
        cw = convw_ref[...]
        yc = (cw[0:1, :] * z2[CONV_HIST:] + cw[1:2, :] * z1[CONV_HIST:]
              + cw[2:3, :] * zz[CONV_HIST:])
        y_s[:, POOL_WIDTH:POOL_WIDTH + CONV_WIDTH] = (proj(OFF_CB, CONV_WIDTH) * yc).astype(BF16)
        zbuf[0:CONV_HIST, :] = zz[ts:ts + CONV_HIST]

    @pl.when(step >= 0)
    def _():
        a_s[...] = _rms(h_ref[...], pre_ref[...]).astype(BF16)
        g_low = proj(OFF_GL, GL_PAD).astype(BF16)
        gx = jnp.dot(g_low, w2_ref[...], preferred_element_type=F32) + b2_ref[...]
        log_sig = jnp.minimum(gx, 0.0) - jnp.log(1.0 + jnp.exp(-jnp.abs(gx)))
        la = log_sig * (LOG2_E / GLA_TAU)
        la_hi = la.astype(BF16)
        la_lo = (la - la_hi.astype(F32)).astype(BF16)
        la_cat = jnp.concatenate([la_hi, la_lo], axis=1)
        pbuf[POOL_HIST:, :] = proj(OFF_POOL, POOL_WIDTH)
        zbuf[CONV_HIST:, :] = proj(OFF_CC, CONV_WIDTH) * proj(OFF_CU, CONV_WIDTH)
        tril = jnp.where(ri >= ci, 1.0, 0.0).astype(BF16)
        for c in range(nc):
            rows = slice(c * CHUNK, (c + 1) * CHUNK)
            bb = jnp.dot(tril, la_cat[rows, :], preferred_element_type=F32)
            b_s[rows, :] = bb[:, :GLA_KW] + bb[:, GLA_KW:]
        q_s[...] = proj(OFF_Q, GLA_KW) * (GLA_DK ** -0.5)
        k_s[...] = proj(OFF_K, GLA_KW)
        pool_mixer()
        conv_mixer()
        v_s[...] = proj(OFF_V, GLA_WIDTH).astype(BF16)
        og = proj(OFF_OG, GLA_WIDTH)
        gate_s[...] = og * jax.nn.sigmoid(og) * gnorm_ref[...]
        bmin_s[0] = jnp.min(b_s[...])

    b, q, k = b_s[...], q_s[...], k_s[...]
    b3 = b.reshape(nc, CHUNK, GLA_KW)
    b_last = b3[:, CHUNK - 1:CHUNK, :]
    qe_s[...] = (q.reshape(b3.shape) * jnp.exp2(b3)).reshape(ts, GLA_KW).astype(BF16)
    kd_s[...] = (k.reshape(b3.shape) * jnp.exp2(b_last - b3)).reshape(ts, GLA_KW).astype(BF16)
    decay = jnp.exp2(b_last)

    nt = (((1,), (1,)), ((), ()))
    tn = (((0,), (0,)), ((), ()))
    ksl = [slice(hd * GLA_DK, (hd + 1) * GLA_DK) for hd in range(GLA_HEADS)]
    vsl = [slice(hd * GLA_DV, (hd + 1) * GLA_DV) for hd in range(GLA_HEADS)]
    pairs = [(c, hd) for c in range(nc) for hd in range(GLA_HEADS)]
    group = 2 * GLA_HEADS
    chunk_rows = lambda c: slice(c * CHUNK, (c + 1) * CHUNK)

    single_ref_ok = bmin_s[0] > -SINGLE_REF_MAX_LOG2

    def state_increments(grp):
        return [lax.dot_general(v_s[chunk_rows(c), vsl[hd]], kd_s[chunk_rows(c), ksl[hd]], tn,
                                preferred_element_type=F32) for c, hd in grp]

    def finish_group(grp, ps, us):
        pvs = [jnp.dot(p, v_s[chunk_rows(c), vsl[hd]], preferred_element_type=F32)
               for p, (c, hd) in zip(ps, grp)]
        for pv, u, (c, hd) in zip(pvs, us, grp):
            o_s[chunk_rows(c), vsl[hd]] = pv
            u_s[c * GLA_HEADS + hd] = u

    @pl.when(single_ref_ok)
    def _():
        lk_s[0] = (k.reshape(b3.shape) * jnp.exp2(-b3)).reshape(ts, GLA_KW).astype(BF16)
        causal = ri >= ci
        for g0 in range(0, len(pairs), group):
            grp = pairs[g0:g0 + group]
            scs = [lax.dot_general(qe_s[chunk_rows(c), ksl[hd]], lk_s[0, chunk_rows(c), ksl[hd]],
                                   nt, preferred_element_type=F32) for c, hd in grp]
            us = state_increments(grp)
            finish_group(grp, [jnp.where(causal, sc, 0.0).astype(BF16) for sc in scs], us)

    lane_hi = lax.broadcasted_iota(jnp.int32, (1, 1, 128), 2) >= GLA_DK
    t64 = lax.broadcasted_iota(jnp.int32, (1, CHUNK, 1), 1)
    blk16, blk4, pos4 = t64 // 16, (t64 // 4) % 4, t64 % 4
    in_hi4 = lax.broadcasted_iota(jnp.int32, (1, 8, 1), 1) >= 4
    neg = -1e30

    def replicate_heads():
        for n, src in enumerate((b_s, q_s, k_s)):
            for c2 in range(2):
                x = src[:, 128 * c2:128 * c2 + 128]
                x_sw = pltpu.roll(x, GLA_DK, 1)
                rep_s[GLA_HEADS * n + 2 * c2] = jnp.where(lane_hi[0], x_sw, x)
                rep_s[GLA_HEADS * n + 2 * c2 + 1] = jnp.where(lane_hi[0], x, x_sw)

    def build_level_operands(hd):
        rb, rq, rk = (rep_s[GLA_HEADS * n + hd] for n in range(3))
        rb64, rq64, rk64 = (x.reshape(nc, CHUNK, 128) for x in (rb, rq, rk))
        rb16 = rb.reshape(ts // 16, 16, 128)
        rb8 = rb.reshape(ts // 8, 8, 128)
        to64 = lambda x: x.reshape(nc, CHUNK, 128)
        end4 = jnp.where(in_hi4, rb8[:, 7:8, :], rb8[:, 3:4, :])
        k1 = rk64 * to64(jnp.exp2(rb16[:, 15:16, :] - rb16))
        k2 = rk64 * to64(jnp.exp2(end4 - rb8))
        nxt = jnp.where(lane_hi, pltpu.roll(rb8, 7, 1), rb8)
        for p in range(2):
            col = slice(128 * p, 128 * p + 128)
            j = jnp.where(lane_hi, 2 * p + 1, 2 * p)
            ref1 = jnp.where(lane_hi, rb64[:, 32 * p + 31:32 * p + 32, :],
                             rb64[:, 32 * p + 15:32 * p + 16, :])
            ref2 = jnp.where(lane_hi, rb16[:, 8 * p + 7:8 * p + 8, :], rb16[:, 8 * p + 3:8 * p + 4, :])
            ref3 = jnp.where(in_hi4, nxt[:, 4 + 2 * p:5 + 2 * p, :], nxt[:, 2 * p:2 * p + 1, :])
            q1 = rq64 * jnp.exp2(jnp.where(blk16 > j, rb64 - ref1, neg))
            q2 = rq64 * jnp.exp2(jnp.where(blk4 > j, to64(rb16 - ref2), neg))
            q3 = rq64 * jnp.exp2(jnp.where(pos4 >= j, to64(rb8 - ref3), neg))
            for n, (ql, kl) in enumerate(((q1, jnp.where(blk16 == j, k1, 0.0)),
                                          (q2, jnp.where(blk4 == j, k2, 0.0)),
                                          (q3, jnp.where(pos4 == j, rk64, 0.0)))):
                lq_s[3 * hd + n, :, col] = ql.reshape(ts, 128).astype(BF16)
                lk_s[3 * hd + n, :, col] = kl.reshape(ts, 128).astype(BF16)

    @pl.when(jnp.logical_not(single_ref_ok))
    def _():
        replicate_heads()
        for hd in range(GLA_HEADS):
            build_level_operands(hd)
        same16 = (ri // 16) == (ci // 16)
        same4 = (ri // 4) == (ci // 4)
        for g0 in range(0, len(pairs), group):
            grp = pairs[g0:g0 + group]
            lvl = [[lax.dot_general(lq_s[3 * hd + n, chunk_rows(c), :],
                                    lk_s[3 * hd + n, chunk_rows(c), :], nt,
                                    preferred_element_type=F32) for n in range(3)]
                   for c, hd in grp]
            us = state_increments(grp)
            finish_group(grp, [(g1 + jnp.where(same16, g2, 0.0)
                                + jnp.where(same4, g3, 0.0)).astype(BF16) for g1, g2, g3 in lvl], us)


    sts = [state_s[hd] for hd in range(GLA_HEADS)]
    for c in range(nc):
        for hd in range(GLA_HEADS):
            sb_s[c * GLA_HEADS + hd] = sts[hd].astype(BF16)
        sts = [sts[hd] * decay[c][:, ksl[hd]] + u_s[c * GLA_HEADS + hd]
               for hd in range(GLA_HEADS)]
    for hd in range(GLA_HEADS):
        state_s[hd] = sts[hd]

    for g0 in range(0, len(pairs), group):
        grp = pairs[g0:g0 + group]
        inters = [lax.dot_general(qe_s[c * CHUNK:(c + 1) * CHUNK, ksl[hd]],
                                  sb_s[c * GLA_HEADS + hd], nt, preferred_element_type=F32)
                  for c, hd in grp]
        for inter, (c, hd) in zip(inters, grp):
            o_s[chunk_rows(c), vsl[hd]] = o_s[chunk_rows(c), vsl[hd]] + inter

    for hd in range(GLA_HEADS):
        vs = slice(hd * GLA_DV, (hd + 1) * GLA_DV)
        o = o_s[:, vs]
        o = o * lax.rsqrt(jnp.mean(o * o, axis=-1, keepdims=True) + EPS)
        y_s[:, POOL_WIDTH + CONV_WIDTH + hd * GLA_DV:POOL_WIDTH + CONV_WIDTH + (hd + 1) * GLA_DV] = (
            o * gate_s[:, vs]).astype(BF16)

    mix = jnp.dot(y_s[...], wout_ref[...], preferred_element_type=F32)
    o_ref[...] = h_ref[...] + _rms(mix, post_ref[...])

    @pl.when(is_meta)
    def _():
        pmeta[...] = pbuf[0:POOL_HIST, :]
        zmeta[...] = zbuf[0:CONV_HIST, :]
        smeta[...] = state_s[...]


def _mixer(h, tiles_per_seq, pre, post, win, wout, poolw, pscale, convw, w2, b2, gnorm):
    ts = TILE
    return pl.pallas_call(
        functools.partial(_mixer_kernel, tiles_per_seq=tiles_per_seq),
        grid=(h.shape[0] // ts,),
        in_specs=[
            pl.BlockSpec((ts, D_MODEL), lambda i: (i, 0)),
            _resident((1, D_MODEL)),
            _resident((1, D_MODEL)),
            _resident((D_MODEL, D_PROJ_PACKED)),
            _resident((D_MODEL, D_MODEL)),
            _resident((POOL_WIDTH, POOL_WIDTH)),
            _resident((1, POOL_WIDTH)),
            _resident((8, CONV_WIDTH)),
            _resident((GL_PAD, GLA_KW)),
            _resident((1, GLA_KW)),
            _resident((1, GLA_WIDTH)),
        ],
        out_specs=pl.BlockSpec((ts, D_MODEL), lambda i: (i, 0)),
        out_shape=jax.ShapeDtypeStruct(h.shape, F32),
        scratch_shapes=[
            pltpu.VMEM((ts, D_MODEL), BF16),
            pltpu.VMEM((POOL_HIST + ts, POOL_WIDTH), F32),
            pltpu.VMEM((CONV_HIST + ts, CONV_WIDTH), F32),
            pltpu.VMEM((ts, GLA_KW), F32),
            pltpu.VMEM((ts, GLA_KW), F32),
            pltpu.VMEM((ts, GLA_KW), F32),
            pltpu.VMEM((3 * GLA_HEADS, ts, 128), F32),
            pltpu.VMEM((ts, GLA_KW), BF16),
            pltpu.VMEM((ts, GLA_KW), BF16),
            pltpu.VMEM((3 * GLA_HEADS, ts, 4 * GLA_DK), BF16),
            pltpu.VMEM((3 * GLA_HEADS, ts, 4 * GLA_DK), BF16),
            pltpu.VMEM((ts, GLA_WIDTH), BF16),
            pltpu.VMEM((ts, GLA_WIDTH), F32),
            pltpu.VMEM((ts, GLA_WIDTH), F32),
            pltpu.VMEM((ts // CHUNK * GLA_HEADS, GLA_DV, GLA_DK), F32),
            pltpu.VMEM((ts // CHUNK * GLA_HEADS, GLA_DV, GLA_DK), BF16),
            pltpu.VMEM((ts, D_MODEL), BF16),
            pltpu.VMEM((GLA_HEADS, GLA_DV, GLA_DK), F32),
            pltpu.VMEM((POOL_HIST, POOL_WIDTH), F32),
            pltpu.VMEM((CONV_HIST, CONV_WIDTH), F32),
            pltpu.VMEM((GLA_HEADS, GLA_DV, GLA_DK), F32),
            pltpu.SMEM((1,), F32),
        ],
        compiler_params=pltpu.CompilerParams(
            dimension_semantics=("arbitrary",), vmem_limit_bytes=VMEM_LIMIT),
        name="mixer",
    )(h, pre, post, win, wout, poolw, pscale, convw, w2, b2, gnorm)


def _pack_w_in(w):
    pool, cb, cc, cu, q, k, v, gl, og = jnp.split(
        w, np.cumsum((256, 256, 256, 256, 256, 256, 512, 16, 512))[:-1].tolist(), axis=-1)
    gl = jnp.pad(gl, ((0, 0), (0, GL_PAD - GLA_GATE_RANK)))
    return jnp.concatenate([pool, cb, cc, cu, q, k, v, og, gl], axis=-1).astype(BF16)


def _block_diag(w):
    out = jnp.zeros((POOL_WIDTH, POOL_WIDTH), w.dtype)
    for g in range(len(POOL_WINDOWS)):
        sl = slice(g * POOL_GROUP, (g + 1) * POOL_GROUP)
        out = out.at[sl, sl].set(w[g])
    return out


def kernel(x, meta, ffn1_pre, ffn1_post, ffn1_wg, ffn1_wu, ffn1_wd, mix_pre, mix_post, w_in,
           pool_w, pool_scale, conv_w, gla_w2, gla_b2, gla_norm, w_out,
           ffn2_pre, ffn2_post, ffn2_wg, ffn2_wu, ffn2_wd):
    n_batch, seq, d = x.shape
    depth = w_in.shape[0]
    assert d == D_MODEL and seq % TILE == 0 and meta.shape[0] == N_META
    meta_tile = jnp.pad(meta.astype(F32), ((META_PAD, 0), (0, 0)))
    h = x.reshape(n_batch * seq, d)
    row = lambda v: v.reshape(1, -1).astype(F32)
    for l in range(depth):
        h = _ffn(h, meta_tile, row(ffn1_pre[l]), row(ffn1_post[l]), ffn1_wg, ffn1_wu, ffn1_wd,
                 mode="first" if l == 0 else "mid", layer=l)
        h = _mixer(
            h, seq // TILE, row(mix_pre[l]), row(mix_post[l]), _pack_w_in(w_in[l]),
            w_out[l].astype(BF16), _block_diag(pool_w[l]).astype(BF16), row(pool_scale[l]),
            jnp.pad(conv_w[l].astype(F32), ((0, 8 - CONV_K), (0, 0))),
            jnp.pad(gla_w2[l], ((0, GL_PAD - GLA_GATE_RANK), (0, 0))).astype(BF16),
            row(gla_b2[l]), row(gla_norm[l]))
        h = _ffn(h, meta_tile, row(ffn2_pre[l]), row(ffn2_post[l]), ffn2_wg, ffn2_wu, ffn2_wd,
                 mode="last" if l == depth - 1 else "mid", layer=l)
    return h.reshape(n_batch, seq, d)
```

```python
import functools

import jax
import jax.numpy as jnp
import numpy as np
from jax import lax
from jax.experimental import pallas as pl
from jax.experimental.pallas import tpu as pltpu

F32 = jnp.float32
BF16 = jnp.bfloat16

D_MODEL = 1024
D_FF = 2816
N_META = 16
EPS = 1e-6
FFN_RESID = 0.5

POOL_WIDTH = 256
POOL_WINDOWS = (2, 4, 8, 16)
POOL_GROUP = 64
CONV_WIDTH = 256
CONV_K = 3
GLA_HEADS = 4
GLA_DK = 64
GLA_DV = 128
GLA_KW = GLA_HEADS * GLA_DK
GLA_WIDTH = GLA_HEADS * GLA_DV
GLA_GATE_RANK = 16
GLA_TAU = 16.0
LOG2_E = 1.4426950408889634
SINGLE_REF_MAX_LOG2 = 100.0
CHUNK = 64


OFF_POOL, OFF_CB, OFF_CC, OFF_CU, OFF_Q, OFF_K = 0, 256, 512, 768, 1024, 1280
OFF_V, OFF_OG, OFF_GL = 1536, 2048, 2560
GL_PAD = 128
D_PROJ_PACKED = OFF_GL + GL_PAD

V7X_VMEM_BYTES = 64 * 1024 * 1024
VMEM_LIMIT = 56 * 1024 * 1024

TILE = 512
FF_CHUNK = 256
META_PAD = TILE - N_META
META_ROWS = 64
POOL_HIST = 16
CONV_HIST = 8


def _inv_rms(x):
    return lax.rsqrt(jnp.mean(x * x, axis=-1, keepdims=True) + EPS)


def _rms(x, g):
    return x * _inv_rms(x) * g


def _load_ffn_weights(layer, wg_hbm, wu_hbm, wd_hbm, wg_ref, wu_ref, wd_ref, stage_cols,
                      stage_rows, sems, on_chunk):
    n_chunks = D_FF // FF_CHUNK

    def copies(j, slot):
        off = pl.multiple_of(j * FF_CHUNK, FF_CHUNK)
        return (
            pltpu.make_async_copy(wg_hbm.at[layer, :, pl.ds(off, FF_CHUNK)],
                                  stage_cols.at[0, slot], sems.at[0, slot]),
            pltpu.make_async_copy(wu_hbm.at[layer, :, pl.ds(off, FF_CHUNK)],
                                  stage_cols.at[1, slot], sems.at[1, slot]),
            pltpu.make_async_copy(wd_hbm.at[layer, pl.ds(off, FF_CHUNK), :],
                                  stage_rows.at[slot], sems.at[2, slot]),
        )

    for cp in copies(0, 0):
        cp.start()

    def body(j, carry):
        slot = lax.rem(j, 2)

        @pl.when(j + 1 < n_chunks)
        def _():
            for cp in copies(j + 1, 1 - slot):
                cp.start()

        for cp in copies(j, slot):
            cp.wait()
        off = pl.multiple_of(j * FF_CHUNK, FF_CHUNK)
        wg_ref[:, pl.ds(off, FF_CHUNK)] = stage_cols[0, slot].astype(BF16)
        wu_ref[:, pl.ds(off, FF_CHUNK)] = stage_cols[1, slot].astype(BF16)
        wd_ref[pl.ds(off, FF_CHUNK), :] = stage_rows[slot].astype(BF16)
        on_chunk(off)
        return carry

    lax.fori_loop(0, n_chunks, body, 0)


def _ffn_kernel(h_ref, meta_ref, pre_ref, post_ref, wg_hbm, wu_hbm, wd_hbm, o_ref,
                a_ref, z_ref, wg_ref, wu_ref, wd_ref, stage_cols, stage_rows, sems, *,
                step0, layer):
    step = pl.program_id(0)

    def start(rows, h):
        a_ref[rows, :] = (h * pre_ref[...]).astype(BF16)
        return _inv_rms(h)

    def hidden_chunk(rows, cols, inv_rms):
        a = a_ref[rows, :]
        g = inv_rms * jnp.dot(a, wg_ref[:, cols], preferred_element_type=F32)
        u = inv_rms * jnp.dot(a, wu_ref[:, cols], preferred_element_type=F32)
        z_ref[rows, cols] = (g * jax.nn.sigmoid(g) * u).astype(BF16)

    def finish(rows, h):
        f = jnp.dot(z_ref[rows, :], wd_ref[...], preferred_element_type=F32)
        o_ref[rows, :] = h + _rms(f, FFN_RESID * post_ref[...])

    @pl.when(step == 0)
    def _():
        rows = slice(0, TILE) if step0 == "tokens" else slice(TILE - META_ROWS, TILE)
        h = (meta_ref if step0 == "meta_input" else h_ref)[rows, :]
        inv_rms = start(rows, h)
        _load_ffn_weights(layer, wg_hbm, wu_hbm, wd_hbm, wg_ref, wu_ref, wd_ref, stage_cols,
                          stage_rows, sems,
                          lambda off: hidden_chunk(rows, pl.ds(off, FF_CHUNK), inv_rms))
        if step0 != "tokens":
            o_ref[0:TILE - META_ROWS, :] = jnp.zeros((TILE - META_ROWS, D_MODEL), F32)
        finish(rows, h)

    @pl.when(step > 0)
    def _():
        h = h_ref[...]
        inv_rms = start(slice(0, TILE), h)
        for j in range(D_FF // FF_CHUNK):
            hidden_chunk(slice(0, TILE), slice(j * FF_CHUNK, (j + 1) * FF_CHUNK), inv_rms)
        finish(slice(0, TILE), h)


def _resident(shape):
    return pl.BlockSpec(shape, lambda *_: (0,) * len(shape), pipeline_mode=pl.Buffered(1))


def _ffn(src, meta_tile, pre, post, wg, wu, wd, *, mode, layer):
    tile = TILE
    n_tiles = src.shape[0] // tile
    assert src.shape[0] % tile == 0
    if mode == "first":
        grid, src_map = n_tiles + 1, lambda i: (jnp.maximum(i - 1, 0), 0)
    elif mode == "mid":
        grid, src_map = n_tiles, lambda i: (i, 0)
    else:
        grid, src_map = n_tiles - 1, lambda i: (i + 1, 0)
    out_rows = grid * tile
    return pl.pallas_call(
        functools.partial(
            _ffn_kernel, layer=layer,
            step0={"first": "meta_input", "mid": "meta_block", "last": "tokens"}[mode]),
        grid=(grid,),
        in_specs=[
            pl.BlockSpec((tile, D_MODEL), src_map),
            _resident((TILE, D_MODEL)),
            _resident((1, D_MODEL)),
            _resident((1, D_MODEL)),
            pl.BlockSpec(memory_space=pl.ANY),
            pl.BlockSpec(memory_space=pl.ANY),
            pl.BlockSpec(memory_space=pl.ANY),
        ],
        out_specs=pl.BlockSpec((tile, D_MODEL), lambda i: (i, 0)),
        out_shape=jax.ShapeDtypeStruct((out_rows, D_MODEL), F32),
        scratch_shapes=[
            pltpu.VMEM((tile, D_MODEL), BF16),
            pltpu.VMEM((tile, D_FF), BF16),
            pltpu.VMEM((D_MODEL, D_FF), BF16),
            pltpu.VMEM((D_MODEL, D_FF), BF16),
            pltpu.VMEM((D_FF, D_MODEL), BF16),
            pltpu.VMEM((2, 2, D_MODEL, FF_CHUNK), F32),
            pltpu.VMEM((2, FF_CHUNK, D_MODEL), F32),
            pltpu.SemaphoreType.DMA((3, 2)),
        ],
        compiler_params=pltpu.CompilerParams(
            dimension_semantics=("arbitrary",), vmem_limit_bytes=VMEM_LIMIT),
        name="ffn_" + mode,
    )(src, meta_tile, pre, post, wg, wu, wd)


def _mixer_kernel(h_ref, pre_ref, post_ref, win_ref, wout_ref, poolw_ref, pscale_ref,
                  convw_ref, w2_ref, b2_ref, gnorm_ref, o_ref,
                  a_s, pbuf, zbuf, b_s, q_s, k_s, rep_s, qe_s, kd_s, lq_s, lk_s, v_s, gate_s, o_s, u_s, sb_s, y_s, state_s,
                  pmeta, zmeta, smeta, bmin_s, *, tiles_per_seq):
    step = pl.program_id(0)
    ts = TILE
    is_meta = step == 0

    @pl.when(is_meta)
    def _():
        pbuf[0:POOL_HIST, :] = jnp.zeros((POOL_HIST, POOL_WIDTH), F32)
        zbuf[0:CONV_HIST, :] = jnp.zeros((CONV_HIST, CONV_WIDTH), F32)
        state_s[...] = jnp.zeros_like(state_s)

    @pl.when(jnp.logical_and(step > 0, lax.rem(step - 1, tiles_per_seq) == 0))
    def _():
        pbuf[0:POOL_HIST, :] = pmeta[...]
        zbuf[0:CONV_HIST, :] = zmeta[...]
        state_s[...] = smeta[...]

    nc = ts // CHUNK
    ri = lax.broadcasted_iota(jnp.int32, (CHUNK, CHUNK), 0)
    ci = lax.broadcasted_iota(jnp.int32, (CHUNK, CHUNK), 1)

    def proj(off, width):
        return jnp.dot(a_s[...], win_ref[:, off:off + width], preferred_element_type=F32)

    def pool_mixer():
        x1 = pbuf[...]
        s2 = x1 + pltpu.roll(x1, 1, 0)
        s4 = s2 + pltpu.roll(s2, 2, 0)
        s8 = s4 + pltpu.roll(s4, 4, 0)
        s16 = s8 + pltpu.roll(s8, 8, 0)
        lane = lax.broadcasted_iota(jnp.int32, (ts, POOL_WIDTH), 1)
        row = lax.broadcasted_iota(jnp.int32, (ts, POOL_WIDTH), 0)
        win = jnp.where(lane < 64, 2, jnp.where(lane < 128, 4, jnp.where(lane < 192, 8, 16)))
        wsum = jnp.where(lane < 64, s2[POOL_HIST:], jnp.where(
            lane < 128, s4[POOL_HIST:], jnp.where(lane < 192, s8[POOL_HIST:], s16[POOL_HIST:])))
        pos1 = jnp.where(is_meta, row - (META_PAD - 1), max(POOL_WINDOWS))
        cnt = jnp.clip(pos1, 1, win).astype(F32)
        m = (wsum / cnt - x1[POOL_HIST:]).astype(BF16)
        y_pool = jnp.dot(m, poolw_ref[...], preferred_element_type=F32) * pscale_ref[...]
        y_s[:, 0:POOL_WIDTH] = y_pool.astype(BF16)
        pbuf[0:POOL_HIST, :] = x1[ts:ts + POOL_HIST]

    def conv_mixer():
        zz = zbuf[...]
        z1 = pltpu.roll(zz, 1, 0)
        z2 = pltpu.roll(zz, 2, 0)
        cw = convw_ref[...]
        yc = (cw[0:1, :] * z2[CONV_HIST:] + cw[1:2, :] * z1[CONV_HIST:]
              + cw[2:3, :] * zz[CONV_HIST:])
        y_s[:, POOL_WIDTH:POOL_WIDTH + CONV_WIDTH] = (proj(OFF_CB, CONV_WIDTH) * yc).astype(BF16)
        zbuf[0:CONV_HIST, :] = zz[ts:ts + CONV_HIST]

    @pl.when(step >= 0)
    def _():
        a_s[...] = _rms(h_ref[...], pre_ref[...]).astype(BF16)
        g_low = proj(OFF_GL, GL_PAD).astype(BF16)
        gx = jnp.dot(g_low, w2_ref[...], preferred_element_type=F32) + b2_ref[...]
        log_sig = jnp.minimum(gx, 0.0) - jnp.log(1.0 + jnp.exp(-jnp.abs(gx)))
        la = log_sig * (LOG2_E / GLA_TAU)
        la_hi = la.astype(BF16)
        la_lo = (la - la_hi.astype(F32)).astype(BF16)
        la_cat = jnp.concatenate([la_hi, la_lo], axis=1)
        pbuf[POOL_HIST:, :] = proj(OFF_POOL, POOL_WIDTH)
        zbuf[CONV_HIST:, :] = proj(OFF_CC, CONV_WIDTH) * proj(OFF_CU, CONV_WIDTH)
        tril = jnp.where(ri >= ci, 1.0, 0.0).astype(BF16)
        for c in range(nc):
            rows = slice(c * CHUNK, (c + 1) * CHUNK)
            bb = jnp.dot(tril, la_cat[rows, :], preferred_element_type=F32)
            b_s[rows, :] = bb[:, :GLA_KW] + bb[:, GLA_KW:]
        q_s[...] = proj(OFF_Q, GLA_KW) * (GLA_DK ** -0.5)
        k_s[...] = proj(OFF_K, GLA_KW)
        pool_mixer()
        conv_mixer()
        v_s[...] = proj(OFF_V, GLA_WIDTH).astype(BF16)
        og = proj(OFF_OG, GLA_WIDTH)
        gate_s[...] = og * jax.nn.sigmoid(og) * gnorm_ref[...]
        bmin_s[0] = jnp.min(b_s[...])

    b, q, k = b_s[...], q_s[...], k_s[...]
    b3 = b.reshape(nc, CHUNK, GLA_KW)
    b_last = b3[:, CHUNK - 1:CHUNK, :]
    qe_s[...] = (q.reshape(b3.shape) * jnp.exp2(b3)).reshape(ts, GLA_KW).astype(BF16)
    kd_s[...] = (k.reshape(b3.shape) * jnp.exp2(b_last - b3)).reshape(ts, GLA_KW).astype(BF16)
    decay = jnp.exp2(b_last)

    nt = (((1,), (1,)), ((), ()))
    tn = (((0,), (0,)), ((), ()))
    ksl = [slice(hd * GLA_DK, (hd + 1) * GLA_DK) for hd in range(GLA_HEADS)]
    vsl = [slice(hd * GLA_DV, (hd + 1) * GLA_DV) for hd in range(GLA_HEADS)]
    pairs = [(c, hd) for c in range(nc) for hd in range(GLA_HEADS)]
    group = 2 * GLA_HEADS
    chunk_rows = lambda c: slice(c * CHUNK, (c + 1) * CHUNK)

    single_ref_ok = bmin_s[0] > -SINGLE_REF_MAX_LOG2

    def state_increments(grp):
        return [lax.dot_general(v_s[chunk_rows(c), vsl[hd]], kd_s[chunk_rows(c), ksl[hd]], tn,
                                preferred_element_type=F32) for c, hd in grp]

    def finish_group(grp, ps, us):
        pvs = [jnp.dot(p, v_s[chunk_rows(c), vsl[hd]], preferred_element_type=F32)
               for p, (c, hd) in zip(ps, grp)]
        for pv, u, (c, hd) in zip(pvs, us, grp):
            o_s[chunk_rows(c), vsl[hd]] = pv
            u_s[c * GLA_HEADS + hd] = u

    @pl.when(single_ref_ok)
    def _():
        lk_s[0] = (k.reshape(b3.shape) * jnp.exp2(-b3)).reshape(ts, GLA_KW).astype(BF16)
        causal = ri >= ci
        for g0 in range(0, len(pairs), group):
            grp = pairs[g0:g0 + group]
            scs = [lax.dot_general(qe_s[chunk_rows(c), ksl[hd]], lk_s[0, chunk_rows(c), ksl[hd]],
                                   nt, preferred_element_type=F32) for c, hd in grp]
            us = state_increments(grp)
            finish_group(grp, [jnp.where(causal, sc, 0.0).astype(BF16) for sc in scs], us)

    lane_hi = lax.broadcasted_iota(jnp.int32, (1, 1, 128), 2) >= GLA_DK
    t64 = lax.broadcasted_iota(jnp.int32, (1, CHUNK, 1), 1)
    blk16, blk4, pos4 = t64 // 16, (t64 // 4) % 4, t64 % 4
    in_hi4 = lax.broadcasted_iota(jnp.int32, (1, 8, 1), 1) >= 4
    neg = -1e30

    def replicate_heads():
        for n, src in enumerate((b_s, q_s, k_s)):
            for c2 in range(2):
                x = src[:, 128 * c2:128 * c2 + 128]
                x_sw = pltpu.roll(x, GLA_DK, 1)
                rep_s[GLA_HEADS * n + 2 * c2] = jnp.where(lane_hi[0], x_sw, x)
                rep_s[GLA_HEADS * n + 2 * c2 + 1] = jnp.where(lane_hi[0], x, x_sw)

    def build_level_operands(hd):
        rb, rq, rk = (rep_s[GLA_HEADS * n + hd] for n in range(3))
        rb64, rq64, rk64 = (x.reshape(nc, CHUNK, 128) for x in (rb, rq, rk))
        rb16 = rb.reshape(ts // 16, 16, 128)
        rb8 = rb.reshape(ts // 8, 8, 128)
        to64 = lambda x: x.reshape(nc, CHUNK, 128)
        end4 = jnp.where(in_hi4, rb8[:, 7:8, :], rb8[:, 3:4, :])
        k1 = rk64 * to64(jnp.exp2(rb16[:, 15:16, :] - rb16))
        k2 = rk64 * to64(jnp.exp2(end4 - rb8))
        nxt = jnp.where(lane_hi, pltpu.roll(rb8, 7, 1), rb8)
        for p in range(2):
            col = slice(128 * p, 128 * p + 128)
            j = jnp.where(lane_hi, 2 * p + 1, 2 * p)
            ref1 = jnp.where(lane_hi, rb64[:, 32 * p + 31:32 * p + 32, :],
                             rb64[:, 32 * p + 15:32 * p + 16, :])
            ref2 = jnp.where(lane_hi, rb16[:, 8 * p + 7:8 * p + 8, :], rb16[:, 8 * p + 3:8 * p + 4, :])
            ref3 = jnp.where(in_hi4, nxt[:, 4 + 2 * p:5 + 2 * p, :], nxt[:, 2 * p:2 * p + 1, :])
            q1 = rq64 * jnp.exp2(jnp.where(blk16 > j, rb64 - ref1, neg))
            q2 = rq64 * jnp.exp2(jnp.where(blk4 > j, to64(rb16 - ref2), neg))
            q3 = rq64 * jnp.exp2(jnp.where(pos4 >= j, to64(rb8 - ref3), neg))
            for n, (ql, kl) in enumerate(((q1, jnp.where(blk16 == j, k1, 0.0)),
                                          (q2, jnp.where(blk4 == j, k2, 0.0)),
                                          (q3, jnp.where(pos4 == j, rk64, 0.0)))):
                lq_s[3 * hd + n, :, col] = ql.reshape(ts, 128).astype(BF16)
                lk_s[3 * hd + n, :, col] = kl.reshape(ts, 128).astype(BF16)

    @pl.when(jnp.logical_not(single_ref_ok))
    def _():
        replicate_heads()
        for hd in range(GLA_HEADS):
            build_level_operands(hd)
        same16 = (ri // 16) == (ci // 16)
        same4 = (ri // 4) == (ci // 4)
        for g0 in range(0, len(pairs), group):
            grp = pairs[g0:g0 + group]
            lvl = [[lax.dot_general(lq_s[3 * hd + n, chunk_rows(c), :],
                                    lk_s[3 * hd + n, chunk_rows(c), :], nt,
                                    preferred_element_type=F32) for n in range(3)]
                   for c, hd in grp]
            us = state_increments(grp)
            finish_group(grp, [(g1 + jnp.where(same16, g2, 0.0)
                                + jnp.where(same4, g3, 0.0)).astype(BF16) for g1, g2, g3 in lvl], us)


    sts = [state_s[hd] for hd in range(GLA_HEADS)]
    for c in range(nc):
        for hd in range(GLA_HEADS):
            sb_s[c * GLA_HEADS + hd] = sts[hd].astype(BF16)
        sts = [sts[hd] * decay[c][:, ksl[hd]] + u_s[c * GLA_HEADS + hd]
               for hd in range(GLA_HEADS)]
    for hd in range(GLA_HEADS):
        state_s[hd] = sts[hd]

    for g0 in range(0, len(pairs), group):
        grp = pairs[g0:g0 + group]
        inters = [lax.dot_general(qe_s[c * CHUNK:(c + 1) * CHUNK, ksl[hd]],
                                  sb_s[c * GLA_HEADS + hd], nt, preferred_element_type=F32)
                  for c, hd in grp]
        for inter, (c, hd) in zip(inters, grp):
            o_s[chunk_rows(c), vsl[hd]] = o_s[chunk_rows(c), vsl[hd]] + inter

    for hd in range(GLA_HEADS):
        vs = slice(hd * GLA_DV, (hd + 1) * GLA_DV)
        o = o_s[:, vs]
        o = o * lax.rsqrt(jnp.mean(o * o, axis=-1, keepdims=True) + EPS)
        y_s[:, POOL_WIDTH + CONV_WIDTH + hd * GLA_DV:POOL_WIDTH + CONV_WIDTH + (hd + 1) * GLA_DV] = (
            o * gate_s[:, vs]).astype(BF16)

    mix = jnp.dot(y_s[...], wout_ref[...], preferred_element_type=F32)
    o_ref[...] = h_ref[...] + _rms(mix, post_ref[...])

    @pl.when(is_meta)
    def _():
        pmeta[...] = pbuf[0:POOL_HIST, :]
        zmeta[...] = zbuf[0:CONV_HIST, :]
        smeta[...] = state_s[...]


def _mixer(h, tiles_per_seq, pre, post, win, wout, poolw, pscale, convw, w2, b2, gnorm):
    ts = TILE
    return pl.pallas_call(
        functools.partial(_mixer_kernel, tiles_per_seq=tiles_per_seq),
        grid=(h.shape[0] // ts,),
        in_specs=[
            pl.BlockSpec((ts, D_MODEL), lambda i: (i, 0)),
            _resident((1, D_MODEL)),
            _resident((1, D_MODEL)),
            _resident((D_MODEL, D_PROJ_PACKED)),
            _resident((D_MODEL, D_MODEL)),
            _resident((POOL_WIDTH, POOL_WIDTH)),
            _resident((1, POOL_WIDTH)),
            _resident((8, CONV_WIDTH)),
            _resident((GL_PAD, GLA_KW)),
            _resident((1, GLA_KW)),
            _resident((1, GLA_WIDTH)),
        ],
        out_specs=pl.BlockSpec((ts, D_MODEL), lambda i: (i, 0)),
        out_shape=jax.ShapeDtypeStruct(h.shape, F32),
        scratch_shapes=[
            pltpu.VMEM((ts, D_MODEL), BF16),
            pltpu.VMEM((POOL_HIST + ts, POOL_WIDTH), F32),
            pltpu.VMEM((CONV_HIST + ts, CONV_WIDTH), F32),
            pltpu.VMEM((ts, GLA_KW), F32),
            pltpu.VMEM((ts, GLA_KW), F32),
            pltpu.VMEM((ts, GLA_KW), F32),
            pltpu.VMEM((3 * GLA_HEADS, ts, 128), F32),
            pltpu.VMEM((ts, GLA_KW), BF16),
            pltpu.VMEM((ts, GLA_KW), BF16),
            pltpu.VMEM((3 * GLA_HEADS, ts, 4 * GLA_DK), BF16),
            pltpu.VMEM((3 * GLA_HEADS, ts, 4 * GLA_DK), BF16),
            pltpu.VMEM((ts, GLA_WIDTH), BF16),
            pltpu.VMEM((ts, GLA_WIDTH), F32),
            pltpu.VMEM((ts, GLA_WIDTH), F32),
            pltpu.VMEM((ts // CHUNK * GLA_HEADS, GLA_DV, GLA_DK), F32),
            pltpu.VMEM((ts // CHUNK * GLA_HEADS, GLA_DV, GLA_DK), BF16),
            pltpu.VMEM((ts, D_MODEL), BF16),
            pltpu.VMEM((GLA_HEADS, GLA_DV, GLA_DK), F32),
            pltpu.VMEM((POOL_HIST, POOL_WIDTH), F32),
            pltpu.VMEM((CONV_HIST, CONV_WIDTH), F32),
            pltpu.VMEM((GLA_HEADS, GLA_DV, GLA_DK), F32),
            pltpu.SMEM((1,), F32),
        ],
        compiler_params=pltpu.CompilerParams(
            dimension_semantics=("arbitrary",), vmem_limit_bytes=VMEM_LIMIT),
        name="mixer",
    )(h, pre, post, win, wout, poolw, pscale, convw, w2, b2, gnorm)


def _pack_w_in(w):
    pool, cb, cc, cu, q, k, v, gl, og = jnp.split(
        w, np.cumsum((256, 256, 256, 256, 256, 256, 512, 16, 512))[:-1].tolist(), axis=-1)
    gl = jnp.pad(gl, ((0, 0), (0, GL_PAD - GLA_GATE_RANK)))
    return jnp.concatenate([pool, cb, cc, cu, q, k, v, og, gl], axis=-1).astype(BF16)


def _block_diag(w):
    out = jnp.zeros((POOL_WIDTH, POOL_WIDTH), w.dtype)
    for g in range(len(POOL_WINDOWS)):
        sl = slice(g * POOL_GROUP, (g + 1) * POOL_GROUP)
        out = out.at[sl, sl].set(w[g])
    return out


def kernel(x, meta, ffn1_pre, ffn1_post, ffn1_wg, ffn1_wu, ffn1_wd, mix_pre, mix_post, w_in,
           pool_w, pool_scale, conv_w, gla_w2, gla_b2, gla_norm, w_out,
           ffn2_pre, ffn2_post, ffn2_wg, ffn2_wu, ffn2_wd):
    n_batch, seq, d = x.shape
    depth = w_in.shape[0]
    assert d == D_MODEL and seq % TILE == 0 and meta.shape[0] == N_META
    meta_tile = jnp.pad(meta.astype(F32), ((META_PAD, 0), (0, 0)))
    h = x.reshape(n_batch * seq, d)
    row = lambda v: v.reshape(1, -1).astype(F32)
    for l in range(depth):
        h = _ffn(h, meta_tile, row(ffn1_pre[l]), row(ffn1_post[l]), ffn1_wg, ffn1_wu, ffn1_wd,
                 mode="first" if l == 0 else "mid", layer=l)
        h = _mixer(
            h, seq // TILE, row(mix_pre[l]), row(mix_post[l]), _pack_w_in(w_in[l]),
            w_out[l].astype(BF16), _block_diag(pool_w[l]).astype(BF16), row(pool_scale[l]),
            jnp.pad(conv_w[l].astype(F32), ((0, 8 - CONV_K), (0, 0))),
            jnp.pad(gla_w2[l], ((0, GL_PAD - GLA_GATE_RANK), (0, 0))).astype(BF16),
            row(gla_b2[l]), row(gla_norm[l]))
        h = _ffn(h, meta_tile, row(ffn2_pre[l]), row(ffn2_post[l]), ffn2_wg, ffn2_wu, ffn2_wd,
                 mode="last" if l == depth - 1 else "mid", layer=l)
    return h.reshape(n_batch, seq, d)
```

```python
import functools

import jax
import jax.numpy as jnp
import numpy as np
from jax import lax
from jax.experimental import pallas as pl
from jax.experimental.pallas import tpu as pltpu

F32 = jnp.float32
BF16 = jnp.bfloat16

D_MODEL = 1024
D_FF = 2816
N_META = 16
EPS = 1e-6
FFN_RESID = 0.5

POOL_WIDTH = 256
POOL_WINDOWS = (2, 4, 8, 16)
POOL_GROUP = 64
CONV_WIDTH = 256
CONV_K = 3
GLA_HEADS = 4
GLA_DK = 64
GLA_DV = 128
GLA_KW = GLA_HEADS * GLA_DK
GLA_WIDTH = GLA_HEADS * GLA_DV
GLA_GATE_RANK = 16
GLA_TAU = 16.0
LOG2_E = 1.4426950408889634
SINGLE_REF_MAX_LOG2 = 100.0
CHUNK = 64


OFF_POOL, OFF_CB, OFF_CC, OFF_CU, OFF_Q, OFF_K = 0, 256, 512, 768, 1024, 1280
OFF_V, OFF_OG, OFF_GL = 1536, 2048, 2560
GL_PAD = 128
D_PROJ_PACKED = OFF_GL + GL_PAD

V7X_VMEM_BYTES = 64 * 1024 * 1024
VMEM_LIMIT = 56 * 1024 * 1024

TILE = 512
FF_CHUNK = 256
META_PAD = TILE - N_META
META_ROWS = 64
POOL_HIST = 16
CONV_HIST = 8


def _rms(x, g):
    return x * lax.rsqrt(jnp.mean(x * x, axis=-1, keepdims=True) + EPS) * g


def _load_ffn_weights(layer, wg_hbm, wu_hbm, wd_hbm, wg_ref, wu_ref, wd_ref, stage_cols,
                      stage_rows, sems, on_chunk):
    n_chunks = D_FF // FF_CHUNK

    def copies(j, slot):
        off = pl.multiple_of(j * FF_CHUNK, FF_CHUNK)
        return (
            pltpu.make_async_copy(wg_hbm.at[layer, :, pl.ds(off, FF_CHUNK)],
                                  stage_cols.at[0, slot], sems.at[0, slot]),
            pltpu.make_async_copy(wu_hbm.at[layer, :, pl.ds(off, FF_CHUNK)],
                                  stage_cols.at[1, slot], sems.at[1, slot]),
            pltpu.make_async_copy(wd_hbm.at[layer, pl.ds(off, FF_CHUNK), :],
                                  stage_rows.at[slot], sems.at[2, slot]),
        )

    for cp in copies(0, 0):
        cp.start()

    def body(j, carry):
        slot = lax.rem(j, 2)

        @pl.when(j + 1 < n_chunks)
        def _():
            for cp in copies(j + 1, 1 - slot):
                cp.start()

        for cp in copies(j, slot):
            cp.wait()
        off = pl.multiple_of(j * FF_CHUNK, FF_CHUNK)
        wg_ref[:, pl.ds(off, FF_CHUNK)] = (0.5 * stage_cols[0, slot]).astype(BF16)
        wu_ref[:, pl.ds(off, FF_CHUNK)] = stage_cols[1, slot].astype(BF16)
        wd_ref[pl.ds(off, FF_CHUNK), :] = stage_rows[slot].astype(BF16)
        on_chunk(off)
        return carry

    lax.fori_loop(0, n_chunks, body, 0)


def _ffn_kernel(h_ref, meta_ref, pre_ref, post_ref, wg_hbm, wu_hbm, wd_hbm, o_ref,
                a_ref, z_ref, wg_ref, wu_ref, wd_ref, stage_cols, stage_rows, sems, *,
                step0, layer):
    step = pl.program_id(0)

    def hidden_chunk(rows, cols):
        a = a_ref[rows, :]
        half_g = jnp.dot(a, wg_ref[:, cols], preferred_element_type=F32)
        u = jnp.dot(a, wu_ref[:, cols], preferred_element_type=F32)
        z_ref[rows, cols] = (half_g * (1.0 + jnp.tanh(half_g)) * u).astype(BF16)

    def finish(rows, h):
        f = jnp.dot(z_ref[rows, :], wd_ref[...], preferred_element_type=F32)
        o_ref[rows, :] = h + _rms(f, FFN_RESID * post_ref[...])

    @pl.when(step == 0)
    def _():
        rows = slice(0, TILE) if step0 == "tokens" else slice(TILE - META_ROWS, TILE)
        h = (meta_ref if step0 == "meta_input" else h_ref)[rows, :]
        a_ref[rows, :] = _rms(h, pre_ref[...]).astype(BF16)
        _load_ffn_weights(layer, wg_hbm, wu_hbm, wd_hbm, wg_ref, wu_ref, wd_ref, stage_cols,
                          stage_rows, sems,
                          lambda off: hidden_chunk(rows, pl.ds(off, FF_CHUNK)))
        if step0 != "tokens":
            o_ref[0:TILE - META_ROWS, :] = jnp.zeros((TILE - META_ROWS, D_MODEL), F32)
        finish(rows, h)

    @pl.when(step > 0)
    def _():
        h = h_ref[...]
        a_ref[...] = _rms(h, pre_ref[...]).astype(BF16)
        for j in range(D_FF // FF_CHUNK):
            hidden_chunk(slice(0, TILE), slice(j * FF_CHUNK, (j + 1) * FF_CHUNK))
        finish(slice(0, TILE), h)


def _resident(shape):
    return pl.BlockSpec(shape, lambda *_: (0,) * len(shape), pipeline_mode=pl.Buffered(1))


def _ffn(src, meta_tile, pre, post, wg, wu, wd, *, mode, layer):
    tile = TILE
    n_tiles = src.shape[0] // tile
    assert src.shape[0] % tile == 0
    if mode == "first":
        grid, src_map = n_tiles + 1, lambda i: (jnp.maximum(i - 1, 0), 0)
    elif mode == "mid":
        grid, src_map = n_tiles, lambda i: (i, 0)
    else:
        grid, src_map = n_tiles - 1, lambda i: (i + 1, 0)
    out_rows = grid * tile
    return pl.pallas_call(
        functools.partial(
            _ffn_kernel, layer=layer,
            step0={"first": "meta_input", "mid": "meta_block", "last": "tokens"}[mode]),
        grid=(grid,),
        in_specs=[
            pl.BlockSpec((tile, D_MODEL), src_map),
            _resident((TILE, D_MODEL)),
            _resident((1, D_MODEL)),
            _resident((1, D_MODEL)),
            pl.BlockSpec(memory_space=pl.ANY),
            pl.BlockSpec(memory_space=pl.ANY),
            pl.BlockSpec(memory_space=pl.ANY),
        ],
        out_specs=pl.BlockSpec((tile, D_MODEL), lambda i: (i, 0)),
        out_shape=jax.ShapeDtypeStruct((out_rows, D_MODEL), F32),
        scratch_shapes=[
            pltpu.VMEM((tile, D_MODEL), BF16),
            pltpu.VMEM((tile, D_FF), BF16),
            pltpu.VMEM((D_MODEL, D_FF), BF16),
            pltpu.VMEM((D_MODEL, D_FF), BF16),
            pltpu.VMEM((D_FF, D_MODEL), BF16),
            pltpu.VMEM((2, 2, D_MODEL, FF_CHUNK), F32),
            pltpu.VMEM((2, FF_CHUNK, D_MODEL), F32),
            pltpu.SemaphoreType.DMA((3, 2)),
        ],
        compiler_params=pltpu.CompilerParams(
            dimension_semantics=("arbitrary",), vmem_limit_bytes=VMEM_LIMIT),
        name="ffn_" + mode,
    )(src, meta_tile, pre, post, wg, wu, wd)


def _mixer_kernel(h_ref, pre_ref, post_ref, win_ref, wout_ref, poolw_ref, pscale_ref,
                  convw_ref, w2_ref, b2_ref, gnorm_ref, o_ref,
                  a_s, pbuf, zbuf, b_s, q_s, k_s, rep_s, qe_s, kd_s, lq_s, lk_s, v_s, gate_s, o_s, u_s, sb_s, y_s, state_s,
                  pmeta, zmeta, smeta, bmin_s, *, tiles_per_seq):
    step = pl.program_id(0)
    ts = TILE
    is_meta = step == 0

    @pl.when(is_meta)
    def _():
        pbuf[0:POOL_HIST, :] = jnp.zeros((POOL_HIST, POOL_WIDTH), F32)
        zbuf[0:CONV_HIST, :] = jnp.zeros((CONV_HIST, CONV_WIDTH), F32)
        state_s[...] = jnp.zeros_like(state_s)

    @pl.when(jnp.logical_and(step > 0, lax.rem(step - 1, tiles_per_seq) == 0))
    def _():
        pbuf[0:POOL_HIST, :] = pmeta[...]
        zbuf[0:CONV_HIST, :] = zmeta[...]
        state_s[...] = smeta[...]

    nc = ts // CHUNK
    ri = lax.broadcasted_iota(jnp.int32, (CHUNK, CHUNK), 0)
    ci = lax.broadcasted_iota(jnp.int32, (CHUNK, CHUNK), 1)

    def proj(off, width):
        return jnp.dot(a_s[...], win_ref[:, off:off + width], preferred_element_type=F32)

    def pool_mixer():
        x1 = pbuf[...]
        s2 = x1 + pltpu.roll(x1, 1, 0)
        s4 = s2 + pltpu.roll(s2, 2, 0)
        s8 = s4 + pltpu.roll(s4, 4, 0)
        s16 = s8 + pltpu.roll(s8, 8, 0)
        lane = lax.broadcasted_iota(jnp.int32, (ts, POOL_WIDTH), 1)
        row = lax.broadcasted_iota(jnp.int32, (ts, POOL_WIDTH), 0)
        win = jnp.where(lane < 64, 2, jnp.where(lane < 128, 4, jnp.where(lane < 192, 8, 16)))
        wsum = jnp.where(lane < 64, s2[POOL_HIST:], jnp.where(
            lane < 128, s4[POOL_HIST:], jnp.where(lane < 192, s8[POOL_HIST:], s16[POOL_HIST:])))
        pos1 = jnp.where(is_meta, row - (META_PAD - 1), max(POOL_WINDOWS))
        cnt = jnp.clip(pos1, 1, win).astype(F32)
        m = (wsum / cnt - x1[POOL_HIST:]).astype(BF16)
        y_pool = jnp.dot(m, poolw_ref[...], preferred_element_type=F32) * pscale_ref[...]
        y_s[:, 0:POOL_WIDTH] = y_pool.astype(BF16)
        pbuf[0:POOL_HIST, :] = x1[ts:ts + POOL_HIST]

    def conv_mixer():
        zz = zbuf[...]
        z1 = pltpu.roll(zz, 1, 0)
        z2 = pltpu.roll(zz, 2, 0)
        cw = convw_ref[...]
        yc = (cw[0:1, :] * z2[CONV_HIST:] + cw[1:2, :] * z1[CONV_HIST:]
              + cw[2:3, :] * zz[CONV_HIST:])
        y_s[:, POOL_WIDTH:POOL_WIDTH + CONV_WIDTH] = (proj(OFF_CB, CONV_WIDTH) * yc).astype(BF16)
        zbuf[0:CONV_HIST, :] = zz[ts:ts + CONV_HIST]

    @pl.when(step >= 0)
    def _():
        a_s[...] = _rms(h_ref[...], pre_ref[...]).astype(BF16)
        g_low = proj(OFF_GL, GL_PAD).astype(BF16)
        gx = jnp.dot(g_low, w2_ref[...], preferred_element_type=F32) + b2_ref[...]
        log_sig = jnp.minimum(gx, 0.0) - jnp.log(1.0 + jnp.exp(-jnp.abs(gx)))
        la = log_sig * (LOG2_E / GLA_TAU)
        la_hi = la.astype(BF16)
        la_lo = (la - la_hi.astype(F32)).astype(BF16)
        la_cat = jnp.concatenate([la_hi, la_lo], axis=1)
        pbuf[POOL_HIST:, :] = proj(OFF_POOL, POOL_WIDTH)
        zbuf[CONV_HIST:, :] = proj(OFF_CC, CONV_WIDTH) * proj(OFF_CU, CONV_WIDTH)
        tril = jnp.where(ri >= ci, 1.0, 0.0).astype(BF16)
        for c in range(nc):
            rows = slice(c * CHUNK, (c + 1) * CHUNK)
            bb = jnp.dot(tril, la_cat[rows, :], preferred_element_type=F32)
            b_s[rows, :] = bb[:, :GLA_KW] + bb[:, GLA_KW:]
        q_s[...] = proj(OFF_Q, GLA_KW) * (GLA_DK ** -0.5)
        k_s[...] = proj(OFF_K, GLA_KW)
        pool_mixer()
        conv_mixer()
        v_s[...] = proj(OFF_V, GLA_WIDTH).astype(BF16)
        og = proj(OFF_OG, GLA_WIDTH)
        gate_s[...] = og * jax.nn.sigmoid(og) * gnorm_ref[...]
        bmin_s[0] = jnp.min(b_s[...])

    b, q, k = b_s[...], q_s[...], k_s[...]
    b3 = b.reshape(nc, CHUNK, GLA_KW)
    b_last = b3[:, CHUNK - 1:CHUNK, :]
    qe_s[...] = (q.reshape(b3.shape) * jnp.exp2(b3)).reshape(ts, GLA_KW).astype(BF16)
    kd_s[...] = (k.reshape(b3.shape) * jnp.exp2(b_last - b3)).reshape(ts, GLA_KW).astype(BF16)
    decay = jnp.exp2(b_last)

    nt = (((1,), (1,)), ((), ()))
    tn = (((0,), (0,)), ((), ()))
    ksl = [slice(hd * GLA_DK, (hd + 1) * GLA_DK) for hd in range(GLA_HEADS)]
    vsl = [slice(hd * GLA_DV, (hd + 1) * GLA_DV) for hd in range(GLA_HEADS)]
    pairs = [(c, hd) for c in range(nc) for hd in range(GLA_HEADS)]
    group = 2 * GLA_HEADS
    chunk_rows = lambda c: slice(c * CHUNK, (c + 1) * CHUNK)

    single_ref_ok = bmin_s[0] > -SINGLE_REF_MAX_LOG2

    def state_increments(grp):
        return [lax.dot_general(v_s[chunk_rows(c), vsl[hd]], kd_s[chunk_rows(c), ksl[hd]], tn,
                                preferred_element_type=F32) for c, hd in grp]

    def finish_group(grp, ps, us):
        pvs = [jnp.dot(p, v_s[chunk_rows(c), vsl[hd]], preferred_element_type=F32)
               for p, (c, hd) in zip(ps, grp)]
        for pv, u, (c, hd) in zip(pvs, us, grp):
            o_s[chunk_rows(c), vsl[hd]] = pv
            u_s[c * GLA_HEADS + hd] = u

    @pl.when(single_ref_ok)
    def _():
        lk_s[0] = (k.reshape(b3.shape) * jnp.exp2(-b3)).reshape(ts, GLA_KW).astype(BF16)
        causal = ri >= ci
        for g0 in range(0, len(pairs), group):
            grp = pairs[g0:g0 + group]
            scs = [lax.dot_general(qe_s[chunk_rows(c), ksl[hd]], lk_s[0, chunk_rows(c), ksl[hd]],
                                   nt, preferred_element_type=F32) for c, hd in grp]
            us = state_increments(grp)
            finish_group(grp, [jnp.where(causal, sc, 0.0).astype(BF16) for sc in scs], us)

    lane_hi = lax.broadcasted_iota(jnp.int32, (1, 1, 128), 2) >= GLA_DK
    t64 = lax.broadcasted_iota(jnp.int32, (1, CHUNK, 1), 1)
    blk16, blk4, pos4 = t64 // 16, (t64 // 4) % 4, t64 % 4
    in_hi4 = lax.broadcasted_iota(jnp.int32, (1, 8, 1), 1) >= 4
    neg = -1e30

    def replicate_heads():
        for n, src in enumerate((b_s, q_s, k_s)):
            for c2 in range(2):
                x = src[:, 128 * c2:128 * c2 + 128]
                x_sw = pltpu.roll(x, GLA_DK, 1)
                rep_s[GLA_HEADS * n + 2 * c2] = jnp.where(lane_hi[0], x_sw, x)
                rep_s[GLA_HEADS * n + 2 * c2 + 1] = jnp.where(lane_hi[0], x, x_sw)

    def build_level_operands(hd):
        rb, rq, rk = (rep_s[GLA_HEADS * n + hd] for n in range(3))
        rb64, rq64, rk64 = (x.reshape(nc, CHUNK, 128) for x in (rb, rq, rk))
        rb16 = rb.reshape(ts // 16, 16, 128)
        rb8 = rb.reshape(ts // 8, 8, 128)
        to64 = lambda x: x.reshape(nc, CHUNK, 128)
        end4 = jnp.where(in_hi4, rb8[:, 7:8, :], rb8[:, 3:4, :])
        k1 = rk64 * to64(jnp.exp2(rb16[:, 15:16, :] - rb16))
        k2 = rk64 * to64(jnp.exp2(end4 - rb8))
        nxt = jnp.where(lane_hi, pltpu.roll(rb8, 7, 1), rb8)
        for p in range(2):
            col = slice(128 * p, 128 * p + 128)
            j = jnp.where(lane_hi, 2 * p + 1, 2 * p)
            ref1 = jnp.where(lane_hi, rb64[:, 32 * p + 31:32 * p + 32, :],
                             rb64[:, 32 * p + 15:32 * p + 16, :])
            ref2 = jnp.where(lane_hi, rb16[:, 8 * p + 7:8 * p + 8, :], rb16[:, 8 * p + 3:8 * p + 4, :])
            ref3 = jnp.where(in_hi4, nxt[:, 4 + 2 * p:5 + 2 * p, :], nxt[:, 2 * p:2 * p + 1, :])
            q1 = rq64 * jnp.exp2(jnp.where(blk16 > j, rb64 - ref1, neg))
            q2 = rq64 * jnp.exp2(jnp.where(blk4 > j, to64(rb16 - ref2), neg))
            q3 = rq64 * jnp.exp2(jnp.where(pos4 >= j, to64(rb8 - ref3), neg))
            for n, (ql, kl) in enumerate(((q1, jnp.where(blk16 == j, k1, 0.0)),
                                          (q2, jnp.where(blk4 == j, k2, 0.0)),
                                          (q3, jnp.where(pos4 == j, rk64, 0.0)))):
                lq_s[3 * hd + n, :, col] = ql.reshape(ts, 128).astype(BF16)
                lk_s[3 * hd + n, :, col] = kl.reshape(ts, 128).astype(BF16)

    @pl.when(jnp.logical_not(single_ref_ok))
    def _():
        replicate_heads()
        for hd in range(GLA_HEADS):
            build_level_operands(hd)
        same16 = (ri // 16) == (ci // 16)
        same4 = (ri // 4) == (ci // 4)
        for g0 in range(0, len(pairs), group):
            grp = pairs[g0:g0 + group]
            lvl = [[lax.dot_general(lq_s[3 * hd + n, chunk_rows(c), :],
                                    lk_s[3 * hd + n, chunk_rows(c), :], nt,
                                    preferred_element_type=F32) for n in range(3)]
                   for c, hd in grp]
            us = state_increments(grp)
            finish_group(grp, [(g1 + jnp.where(same16, g2, 0.0)
                                + jnp.where(same4, g3, 0.0)).astype(BF16) for g1, g2, g3 in lvl], us)


    sts = [state_s[hd] for hd in range(GLA_HEADS)]
    for c in range(nc):
        for hd in range(GLA_HEADS):
            sb_s[c * GLA_HEADS + hd] = sts[hd].astype(BF16)
        sts = [sts[hd] * decay[c][:, ksl[hd]] + u_s[c * GLA_HEADS + hd]
               for hd in range(GLA_HEADS)]
    for hd in range(GLA_HEADS):
        state_s[hd] = sts[hd]

    for g0 in range(0, len(pairs), group):
        grp = pairs[g0:g0 + group]
        inters = [lax.dot_general(qe_s[c * CHUNK:(c + 1) * CHUNK, ksl[hd]],
                                  sb_s[c * GLA_HEADS + hd], nt, preferred_element_type=F32)
                  for c, hd in grp]
        for inter, (c, hd) in zip(inters, grp):
            o_s[chunk_rows(c), vsl[hd]] = o_s[chunk_rows(c), vsl[hd]] + inter

    for hd in range(GLA_HEADS):
        vs = slice(hd * GLA_DV, (hd + 1) * GLA_DV)
        o = o_s[:, vs]
        o = o * lax.rsqrt(jnp.mean(o * o, axis=-1, keepdims=True) + EPS)
        y_s[:, POOL_WIDTH + CONV_WIDTH + hd * GLA_DV:POOL_WIDTH + CONV_WIDTH + (hd + 1) * GLA_DV] = (
            o * gate_s[:, vs]).astype(BF16)

    mix = jnp.dot(y_s[...], wout_ref[...], preferred_element_type=F32)
    o_ref[...] = h_ref[...] + _rms(mix, post_ref[...])

    @pl.when(is_meta)
    def _():
        pmeta[...] = pbuf[0:POOL_HIST, :]
        zmeta[...] = zbuf[0:CONV_HIST, :]
        smeta[...] = state_s[...]


def _mixer(h, tiles_per_seq, pre, post, win, wout, poolw, pscale, convw, w2, b2, gnorm):
    ts = TILE
    return pl.pallas_call(
        functools.partial(_mixer_kernel, tiles_per_seq=tiles_per_seq),
        grid=(h.shape[0] // ts,),
        in_specs=[
            pl.BlockSpec((ts, D_MODEL), lambda i: (i, 0)),
            _resident((1, D_MODEL)),
            _resident((1, D_MODEL)),
            _resident((D_MODEL, D_PROJ_PACKED)),
            _resident((D_MODEL, D_MODEL)),
            _resident((POOL_WIDTH, POOL_WIDTH)),
            _resident((1, POOL_WIDTH)),
            _resident((8, CONV_WIDTH)),
            _resident((GL_PAD, GLA_KW)),
            _resident((1, GLA_KW)),
            _resident((1, GLA_WIDTH)),
        ],
        out_specs=pl.BlockSpec((ts, D_MODEL), lambda i: (i, 0)),
        out_shape=jax.ShapeDtypeStruct(h.shape, F32),
        scratch_shapes=[
            pltpu.VMEM((ts, D_MODEL), BF16),
            pltpu.VMEM((POOL_HIST + ts, POOL_WIDTH), F32),
            pltpu.VMEM((CONV_HIST + ts, CONV_WIDTH), F32),
            pltpu.VMEM((ts, GLA_KW), F32),
            pltpu.VMEM((ts, GLA_KW), F32),
            pltpu.VMEM((ts, GLA_KW), F32),
            pltpu.VMEM((3 * GLA_HEADS, ts, 128), F32),
            pltpu.VMEM((ts, GLA_KW), BF16),
            pltpu.VMEM((ts, GLA_KW), BF16),
            pltpu.VMEM((3 * GLA_HEADS, ts, 4 * GLA_DK), BF16),
            pltpu.VMEM((3 * GLA_HEADS, ts, 4 * GLA_DK), BF16),
            pltpu.VMEM((ts, GLA_WIDTH), BF16),
            pltpu.VMEM((ts, GLA_WIDTH), F32),
            pltpu.VMEM((ts, GLA_WIDTH), F32),
            pltpu.VMEM((ts // CHUNK * GLA_HEADS, GLA_DV, GLA_DK), F32),
            pltpu.VMEM((ts // CHUNK * GLA_HEADS, GLA_DV, GLA_DK), BF16),
            pltpu.VMEM((ts, D_MODEL), BF16),
            pltpu.VMEM((GLA_HEADS, GLA_DV, GLA_DK), F32),
            pltpu.VMEM((POOL_HIST, POOL_WIDTH), F32),
            pltpu.VMEM((CONV_HIST, CONV_WIDTH), F32),
            pltpu.VMEM((GLA_HEADS, GLA_DV, GLA_DK), F32),
            pltpu.SMEM((1,), F32),
        ],
        compiler_params=pltpu.CompilerParams(
            dimension_semantics=("arbitrary",), vmem_limit_bytes=VMEM_LIMIT),
        name="mixer",
    )(h, pre, post, win, wout, poolw, pscale, convw, w2, b2, gnorm)


def _pack_w_in(w):
    pool, cb, cc, cu, q, k, v, gl, og = jnp.split(
        w, np.cumsum((256, 256, 256, 256, 256, 256, 512, 16, 512))[:-1].tolist(), axis=-1)
    gl = jnp.pad(gl, ((0, 0), (0, GL_PAD - GLA_GATE_RANK)))
    return jnp.concatenate([pool, cb, cc, cu, q, k, v, og, gl], axis=-1).astype(BF16)


def _block_diag(w):
    out = jnp.zeros((POOL_WIDTH, POOL_WIDTH), w.dtype)
    for g in range(len(POOL_WINDOWS)):
        sl = slice(g * POOL_GROUP, (g + 1) * POOL_GROUP)
        out = out.at[sl, sl].set(w[g])
    return out


def kernel(x, meta, ffn1_pre, ffn1_post, ffn1_wg, ffn1_wu, ffn1_wd, mix_pre, mix_post, w_in,
           pool_w, pool_scale, conv_w, gla_w2, gla_b2, gla_norm, w_out,
           ffn2_pre, ffn2_post, ffn2_wg, ffn2_wu, ffn2_wd):
    n_batch, seq, d = x.shape
    depth = w_in.shape[0]
    assert d == D_MODEL and seq % TILE == 0 and meta.shape[0] == N_META
    meta_tile = jnp.pad(meta.astype(F32), ((META_PAD, 0), (0, 0)))
    h = x.reshape(n_batch * seq, d)
    row = lambda v: v.reshape(1, -1).astype(F32)
    for l in range(depth):
        h = _ffn(h, meta_tile, row(ffn1_pre[l]), row(ffn1_post[l]), ffn1_wg, ffn1_wu, ffn1_wd,
                 mode="first" if l == 0 else "mid", layer=l)
        h = _mixer(
            h, seq // TILE, row(mix_pre[l]), row(mix_post[l]), _pack_w_in(w_in[l]),
            w_out[l].astype(BF16), _block_diag(pool_w[l]).astype(BF16), row(pool_scale[l]),
            jnp.pad(conv_w[l].astype(F32), ((0, 8 - CONV_K), (0, 0))),
            jnp.pad(gla_w2[l], ((0, GL_PAD - GLA_GATE_RANK), (0, 0))).astype(BF16),
            row(gla_b2[l]), row(gla_norm[l]))
        h = _ffn(h, meta_tile, row(ffn2_pre[l]), row(ffn2_post[l]), ffn2_wg, ffn2_wu, ffn2_wd,
                 mode="last" if l == depth - 1 else "mid", layer=l)
    return h.reshape(n_batch, seq, d)
```

```python
import functools

import jax
import jax.numpy as jnp
import numpy as np
from jax import lax
from jax.experimental import pallas as pl
from jax.experimental.pallas import tpu as pltpu

F32 = jnp.float32
BF16 = jnp.bfloat16

D_MODEL = 1024
D_FF = 2816
N_META = 16
EPS = 1e-6
FFN_RESID = 0.5

POOL_WIDTH = 256
POOL_WINDOWS = (2, 4, 8, 16)
POOL_GROUP = 64
CONV_WIDTH = 256
CONV_K = 3
GLA_HEADS = 4
GLA_DK = 64
GLA_DV = 128
GLA_KW = GLA_HEADS * GLA_DK
GLA_WIDTH = GLA_HEADS * GLA_DV
GLA_GATE_RANK = 16
GLA_TAU = 16.0
LOG2_E = 1.4426950408889634
SINGLE_REF_MAX_LOG2 = 100.0
CHUNK = 64


OFF_POOL, OFF_CB, OFF_CC, OFF_CU, OFF_Q, OFF_K = 0, 256, 512, 768, 1024, 1280
OFF_V, OFF_OG, OFF_GL = 1536, 2048, 2560
GL_PAD = 128
D_PROJ_PACKED = OFF_GL + GL_PAD

V7X_VMEM_BYTES = 64 * 1024 * 1024
VMEM_LIMIT = 56 * 1024 * 1024

TILE = 512
FF_CHUNK = 256
META_PAD = TILE - N_META
META_ROWS = 64
POOL_HIST = 16
CONV_HIST = 8


def _rms(x, g):
    return x * lax.rsqrt(jnp.mean(x * x, axis=-1, keepdims=True) + EPS) * g


def _load_ffn_weights(layer, wg_hbm, wu_hbm, wd_hbm, wg_ref, wu_ref, wd_ref, stage_cols,
                      stage_rows, sems, on_chunk):
    n_chunks = D_FF // FF_CHUNK

    def copies(j, slot):
        off = pl.multiple_of(j * FF_CHUNK, FF_CHUNK)
        return (
            pltpu.make_async_copy(wg_hbm.at[layer, :, pl.ds(off, FF_CHUNK)],
                                  stage_cols.at[0, slot], sems.at[0, slot]),
            pltpu.make_async_copy(wu_hbm.at[layer, :, pl.ds(off, FF_CHUNK)],
                                  stage_cols.at[1, slot], sems.at[1, slot]),
            pltpu.make_async_copy(wd_hbm.at[layer, pl.ds(off, FF_CHUNK), :],
                                  stage_rows.at[slot], sems.at[2, slot]),
        )

    for cp in copies(0, 0):
        cp.start()

    def body(j, carry):
        slot = lax.rem(j, 2)

        @pl.when(j + 1 < n_chunks)
        def _():
            for cp in copies(j + 1, 1 - slot):
                cp.start()

        for cp in copies(j, slot):
            cp.wait()
        off = pl.multiple_of(j * FF_CHUNK, FF_CHUNK)
        wg_ref[:, pl.ds(off, FF_CHUNK)] = (0.5 * stage_cols[0, slot]).astype(BF16)
        wu_ref[:, pl.ds(off, FF_CHUNK)] = stage_cols[1, slot].astype(BF16)
        wd_ref[pl.ds(off, FF_CHUNK), :] = stage_rows[slot].astype(BF16)
        on_chunk(off)
        return carry

    lax.fori_loop(0, n_chunks, body, 0)


def _ffn_kernel(h_ref, meta_ref, pre_ref, post_ref, wg_hbm, wu_hbm, wd_hbm, o_ref,
                a_ref, z_ref, wg_ref, wu_ref, wd_ref, stage_cols, stage_rows, sems, *,
                step0, layer):
    step = pl.program_id(0)

    def hidden_chunk(rows, cols):
        a = a_ref[rows, :]
        half_g = jnp.dot(a, wg_ref[:, cols], preferred_element_type=F32)
        u = jnp.dot(a, wu_ref[:, cols], preferred_element_type=F32)
        z_ref[rows, cols] = (half_g * (1.0 + jnp.tanh(half_g)) * u).astype(BF16)

    def finish(rows, h):
        f = jnp.dot(z_ref[rows, :], wd_ref[...], preferred_element_type=F32)
        o_ref[rows, :] = h + _rms(f, FFN_RESID * post_ref[...])

    @pl.when(step == 0)
    def _():
        rows = slice(0, TILE) if step0 == "tokens" else slice(TILE - META_ROWS, TILE)
        h = (meta_ref if step0 == "meta_input" else h_ref)[rows, :]
        a_ref[rows, :] = _rms(h, pre_ref[...]).astype(BF16)
        _load_ffn_weights(layer, wg_hbm, wu_hbm, wd_hbm, wg_ref, wu_ref, wd_ref, stage_cols,
                          stage_rows, sems,
                          lambda off: hidden_chunk(rows, pl.ds(off, FF_CHUNK)))
        if step0 != "tokens":
            o_ref[0:TILE - META_ROWS, :] = jnp.zeros((TILE - META_ROWS, D_MODEL), F32)
        finish(rows, h)

    @pl.when(step > 0)
    def _():
        h = h_ref[...]
        a_ref[...] = _rms(h, pre_ref[...]).astype(BF16)
        for j in range(D_FF // FF_CHUNK):
            hidden_chunk(slice(0, TILE), slice(j * FF_CHUNK, (j + 1) * FF_CHUNK))
        finish(slice(0, TILE), h)


def _resident(shape, layer=None):
    if layer is None:
        return pl.BlockSpec(shape, lambda *_: (0,) * len(shape), pipeline_mode=pl.Buffered(1))
    return pl.BlockSpec((None,) + tuple(shape), lambda *_: (layer,) + (0,) * len(shape),
                        pipeline_mode=pl.Buffered(1))


def _ffn(src, meta_tile, pre, post, wg, wu, wd, *, mode, layer):
    tile = TILE
    n_tiles = src.shape[0] // tile
    assert src.shape[0] % tile == 0
    if mode == "first":
        grid, src_map = n_tiles + 1, lambda i: (jnp.maximum(i - 1, 0), 0)
    elif mode == "mid":
        grid, src_map = n_tiles, lambda i: (i, 0)
    else:
        grid, src_map = n_tiles - 1, lambda i: (i + 1, 0)
    out_rows = grid * tile
    return pl.pallas_call(
        functools.partial(
            _ffn_kernel, layer=layer,
            step0={"first": "meta_input", "mid": "meta_block", "last": "tokens"}[mode]),
        grid=(grid,),
        in_specs=[
            pl.BlockSpec((tile, D_MODEL), src_map),
            _resident((TILE, D_MODEL)),
            _resident((1, D_MODEL), layer),
            _resident((1, D_MODEL), layer),
            pl.BlockSpec(memory_space=pl.ANY),
            pl.BlockSpec(memory_space=pl.ANY),
            pl.BlockSpec(memory_space=pl.ANY),
        ],
        out_specs=pl.BlockSpec((tile, D_MODEL), lambda i: (i, 0)),
        out_shape=jax.ShapeDtypeStruct((out_rows, D_MODEL), F32),
        scratch_shapes=[
            pltpu.VMEM((tile, D_MODEL), BF16),
            pltpu.VMEM((tile, D_FF), BF16),
            pltpu.VMEM((D_MODEL, D_FF), BF16),
            pltpu.VMEM((D_MODEL, D_FF), BF16),
            pltpu.VMEM((D_FF, D_MODEL), BF16),
            pltpu.VMEM((2, 2, D_MODEL, FF_CHUNK), F32),
            pltpu.VMEM((2, FF_CHUNK, D_MODEL), F32),
            pltpu.SemaphoreType.DMA((3, 2)),
        ],
        compiler_params=pltpu.CompilerParams(
            dimension_semantics=("arbitrary",), vmem_limit_bytes=VMEM_LIMIT),
        name="ffn_" + mode,
    )(src, meta_tile, pre, post, wg, wu, wd)


def _mixer_kernel(h_ref, pre_ref, post_ref, win_ref, wout_ref, poolw_ref, pscale_ref,
                  convw_ref, w2_ref, b2_ref, gnorm_ref, o_ref,
                  a_s, pbuf, zbuf, b_s, q_s, k_s, rep_s, qe_s, kd_s, lq_s, lk_s, v_s, gate_s, o_s, u_s, sb_s, y_s, state_s,
                  pmeta, zmeta, smeta, bmin_s, *, tiles_per_seq):
    step = pl.program_id(0)
    ts = TILE
    is_meta = step == 0

    @pl.when(is_meta)
    def _():
        pbuf[0:POOL_HIST, :] = jnp.zeros((POOL_HIST, POOL_WIDTH), F32)
        zbuf[0:CONV_HIST, :] = jnp.zeros((CONV_HIST, CONV_WIDTH), F32)
        state_s[...] = jnp.zeros_like(state_s)

    @pl.when(jnp.logical_and(step > 0, lax.rem(step - 1, tiles_per_seq) == 0))
    def _():
        pbuf[0:POOL_HIST, :] = pmeta[...]
        zbuf[0:CONV_HIST, :] = zmeta[...]
        state_s[...] = smeta[...]

    nc = ts // CHUNK
    ri = lax.broadcasted_iota(jnp.int32, (CHUNK, CHUNK), 0)
    ci = lax.broadcasted_iota(jnp.int32, (CHUNK, CHUNK), 1)

    def proj(off, width):
        return jnp.dot(a_s[...], win_ref[:, off:off + width], preferred_element_type=F32)

    def pool_mixer():
        x1 = pbuf[...]
        s2 = x1 + pltpu.roll(x1, 1, 0)
        s4 = s2 + pltpu.roll(s2, 2, 0)
        s8 = s4 + pltpu.roll(s4, 4, 0)
        s16 = s8 + pltpu.roll(s8, 8, 0)
        lane = lax.broadcasted_iota(jnp.int32, (ts, POOL_WIDTH), 1)
        row = lax.broadcasted_iota(jnp.int32, (ts, POOL_WIDTH), 0)
        win = jnp.where(lane < 64, 2, jnp.where(lane < 128, 4, jnp.where(lane < 192, 8, 16)))
        wsum = jnp.where(lane < 64, s2[POOL_HIST:], jnp.where(
            lane < 128, s4[POOL_HIST:], jnp.where(lane < 192, s8[POOL_HIST:], s16[POOL_HIST:])))
        pos1 = jnp.where(is_meta, row - (META_PAD - 1), max(POOL_WINDOWS))
        cnt = jnp.clip(pos1, 1, win).astype(F32)
        m = (wsum / cnt - x1[POOL_HIST:]).astype(BF16)
        y_pool = jnp.dot(m, poolw_ref[...], preferred_element_type=F32) * pscale_ref[...]
        y_s[:, 0:POOL_WIDTH] = y_pool.astype(BF16)
        pbuf[0:POOL_HIST, :] = x1[ts:ts + POOL_HIST]

    def conv_mixer():
        zz = zbuf[...]
        z1 = pltpu.roll(zz, 1, 0)
        z2 = pltpu.roll(zz, 2, 0)
        cw = convw_ref[...]
        yc = (cw[0:1, :] * z2[CONV_HIST:] + cw[1:2, :] * z1[CONV_HIST:]
              + cw[2:3, :] * zz[CONV_HIST:])
        y_s[:, POOL_WIDTH:POOL_WIDTH + CONV_WIDTH] = (proj(OFF_CB, CONV_WIDTH) * yc).astype(BF16)
        zbuf[0:CONV_HIST, :] = zz[ts:ts + CONV_HIST]

    @pl.when(step >= 0)
    def _():
        a_s[...] = _rms(h_ref[...], pre_ref[...]).astype(BF16)
        g_low = proj(OFF_GL, GL_PAD).astype(BF16)
        gx = jnp.dot(g_low, w2_ref[...], preferred_element_type=F32) + b2_ref[...]
        log_sig = jnp.minimum(gx, 0.0) - jnp.log(1.0 + jnp.exp(-jnp.abs(gx)))
        la = log_sig * (LOG2_E / GLA_TAU)
        la_hi = la.astype(BF16)
        la_lo = (la - la_hi.astype(F32)).astype(BF16)
        la_cat = jnp.concatenate([la_hi, la_lo], axis=1)
        pbuf[POOL_HIST:, :] = proj(OFF_POOL, POOL_WIDTH)
        zbuf[CONV_HIST:, :] = proj(OFF_CC, CONV_WIDTH) * proj(OFF_CU, CONV_WIDTH)
        tril = jnp.where(ri >= ci, 1.0, 0.0).astype(BF16)
        for c in range(nc):
            rows = slice(c * CHUNK, (c + 1) * CHUNK)
            bb = jnp.dot(tril, la_cat[rows, :], preferred_element_type=F32)
            b_s[rows, :] = bb[:, :GLA_KW] + bb[:, GLA_KW:]
        q_s[...] = proj(OFF_Q, GLA_KW) * (GLA_DK ** -0.5)
        k_s[...] = proj(OFF_K, GLA_KW)
        pool_mixer()
        conv_mixer()
        v_s[...] = proj(OFF_V, GLA_WIDTH).astype(BF16)
        og = proj(OFF_OG, GLA_WIDTH)
        gate_s[...] = og * jax.nn.sigmoid(og) * gnorm_ref[...]
        bmin_s[0] = jnp.min(b_s[...])

    b, q, k = b_s[...], q_s[...], k_s[...]
    b3 = b.reshape(nc, CHUNK, GLA_KW)
    b_last = b3[:, CHUNK - 1:CHUNK, :]
    qe_s[...] = (q.reshape(b3.shape) * jnp.exp2(b3)).reshape(ts, GLA_KW).astype(BF16)
    kd_s[...] = (k.reshape(b3.shape) * jnp.exp2(b_last - b3)).reshape(ts, GLA_KW).astype(BF16)
    decay = jnp.exp2(b_last)

    nt = (((1,), (1,)), ((), ()))
    tn = (((0,), (0,)), ((), ()))
    ksl = [slice(hd * GLA_DK, (hd + 1) * GLA_DK) for hd in range(GLA_HEADS)]
    vsl = [slice(hd * GLA_DV, (hd + 1) * GLA_DV) for hd in range(GLA_HEADS)]
    pairs = [(c, hd) for c in range(nc) for hd in range(GLA_HEADS)]
    group = 2 * GLA_HEADS
    chunk_rows = lambda c: slice(c * CHUNK, (c + 1) * CHUNK)

    single_ref_ok = bmin_s[0] > -SINGLE_REF_MAX_LOG2

    def state_increments(grp):
        return [lax.dot_general(v_s[chunk_rows(c), vsl[hd]], kd_s[chunk_rows(c), ksl[hd]], tn,
                                preferred_element_type=F32) for c, hd in grp]

    def finish_group(grp, ps, us):
        pvs = [jnp.dot(p, v_s[chunk_rows(c), vsl[hd]], preferred_element_type=F32)
               for p, (c, hd) in zip(ps, grp)]
        for pv, u, (c, hd) in zip(pvs, us, grp):
            o_s[chunk_rows(c), vsl[hd]] = pv
            u_s[c * GLA_HEADS + hd] = u

    @pl.when(single_ref_ok)
    def _():
        lk_s[0] = (k.reshape(b3.shape) * jnp.exp2(-b3)).reshape(ts, GLA_KW).astype(BF16)
        causal = ri >= ci
        for g0 in range(0, len(pairs), group):
            grp = pairs[g0:g0 + group]
            scs = [lax.dot_general(qe_s[chunk_rows(c), ksl[hd]], lk_s[0, chunk_rows(c), ksl[hd]],
                                   nt, preferred_element_type=F32) for c, hd in grp]
            us = state_increments(grp)
            finish_group(grp, [jnp.where(causal, sc, 0.0).astype(BF16) for sc in scs], us)

    lane_hi = lax.broadcasted_iota(jnp.int32, (1, 1, 128), 2) >= GLA_DK
    t64 = lax.broadcasted_iota(jnp.int32, (1, CHUNK, 1), 1)
    blk16, blk4, pos4 = t64 // 16, (t64 // 4) % 4, t64 % 4
    in_hi4 = lax.broadcasted_iota(jnp.int32, (1, 8, 1), 1) >= 4
    neg = -1e30

    def replicate_heads():
        for n, src in enumerate((b_s, q_s, k_s)):
            for c2 in range(2):
                x = src[:, 128 * c2:128 * c2 + 128]
                x_sw = pltpu.roll(x, GLA_DK, 1)
                rep_s[GLA_HEADS * n + 2 * c2] = jnp.where(lane_hi[0], x_sw, x)
                rep_s[GLA_HEADS * n + 2 * c2 + 1] = jnp.where(lane_hi[0], x, x_sw)

    def build_level_operands(hd):
        rb, rq, rk = (rep_s[GLA_HEADS * n + hd] for n in range(3))
        rb64, rq64, rk64 = (x.reshape(nc, CHUNK, 128) for x in (rb, rq, rk))
        rb16 = rb.reshape(ts // 16, 16, 128)
        rb8 = rb.reshape(ts // 8, 8, 128)
        to64 = lambda x: x.reshape(nc, CHUNK, 128)
        end4 = jnp.where(in_hi4, rb8[:, 7:8, :], rb8[:, 3:4, :])
        k1 = rk64 * to64(jnp.exp2(rb16[:, 15:16, :] - rb16))
        k2 = rk64 * to64(jnp.exp2(end4 - rb8))
        nxt = jnp.where(lane_hi, pltpu.roll(rb8, 7, 1), rb8)
        for p in range(2):
            col = slice(128 * p, 128 * p + 128)
            j = jnp.where(lane_hi, 2 * p + 1, 2 * p)
            ref1 = jnp.where(lane_hi, rb64[:, 32 * p + 31:32 * p + 32, :],
                             rb64[:, 32 * p + 15:32 * p + 16, :])
            ref2 = jnp.where(lane_hi, rb16[:, 8 * p + 7:8 * p + 8, :], rb16[:, 8 * p + 3:8 * p + 4, :])
            ref3 = jnp.where(in_hi4, nxt[:, 4 + 2 * p:5 + 2 * p, :], nxt[:, 2 * p:2 * p + 1, :])
            q1 = rq64 * jnp.exp2(jnp.where(blk16 > j, rb64 - ref1, neg))
            q2 = rq64 * jnp.exp2(jnp.where(blk4 > j, to64(rb16 - ref2), neg))
            q3 = rq64 * jnp.exp2(jnp.where(pos4 >= j, to64(rb8 - ref3), neg))
            for n, (ql, kl) in enumerate(((q1, jnp.where(blk16 == j, k1, 0.0)),
                                          (q2, jnp.where(blk4 == j, k2, 0.0)),
                                          (q3, jnp.where(pos4 == j, rk64, 0.0)))):
                lq_s[3 * hd + n, :, col] = ql.reshape(ts, 128).astype(BF16)
                lk_s[3 * hd + n, :, col] = kl.reshape(ts, 128).astype(BF16)

    @pl.when(jnp.logical_not(single_ref_ok))
    def _():
        replicate_heads()
        for hd in range(GLA_HEADS):
            build_level_operands(hd)
        same16 = (ri // 16) == (ci // 16)
        same4 = (ri // 4) == (ci // 4)
        for g0 in range(0, len(pairs), group):
            grp = pairs[g0:g0 + group]
            lvl = [[lax.dot_general(lq_s[3 * hd + n, chunk_rows(c), :],
                                    lk_s[3 * hd + n, chunk_rows(c), :], nt,
                                    preferred_element_type=F32) for n in range(3)]
                   for c, hd in grp]
            us = state_increments(grp)
            finish_group(grp, [(g1 + jnp.where(same16, g2, 0.0)
                                + jnp.where(same4, g3, 0.0)).astype(BF16) for g1, g2, g3 in lvl], us)


    sts = [state_s[hd] for hd in range(GLA_HEADS)]
    for c in range(nc):
        for hd in range(GLA_HEADS):
            sb_s[c * GLA_HEADS + hd] = sts[hd].astype(BF16)
        sts = [sts[hd] * decay[c][:, ksl[hd]] + u_s[c * GLA_HEADS + hd]
               for hd in range(GLA_HEADS)]
    for hd in range(GLA_HEADS):
        state_s[hd] = sts[hd]

    for g0 in range(0, len(pairs), group):
        grp = pairs[g0:g0 + group]
        inters = [lax.dot_general(qe_s[c * CHUNK:(c + 1) * CHUNK, ksl[hd]],
                                  sb_s[c * GLA_HEADS + hd], nt, preferred_element_type=F32)
                  for c, hd in grp]
        for inter, (c, hd) in zip(inters, grp):
            o_s[chunk_rows(c), vsl[hd]] = o_s[chunk_rows(c), vsl[hd]] + inter

    for hd in range(GLA_HEADS):
        vs = slice(hd * GLA_DV, (hd + 1) * GLA_DV)
        o = o_s[:, vs]
        o = o * lax.rsqrt(jnp.mean(o * o, axis=-1, keepdims=True) + EPS)
        y_s[:, POOL_WIDTH + CONV_WIDTH + hd * GLA_DV:POOL_WIDTH + CONV_WIDTH + (hd + 1) * GLA_DV] = (
            o * gate_s[:, vs]).astype(BF16)

    mix = jnp.dot(y_s[...], wout_ref[...], preferred_element_type=F32)
    o_ref[...] = h_ref[...] + _rms(mix, post_ref[...])

    @pl.when(is_meta)
    def _():
        pmeta[...] = pbuf[0:POOL_HIST, :]
        zmeta[...] = zbuf[0:CONV_HIST, :]
        smeta[...] = state_s[...]


def _mixer(h, tiles_per_seq, pre, post, win, wout, poolw, pscale, convw, w2, b2, gnorm, *,
           layer):
    ts = TILE
    return pl.pallas_call(
        functools.partial(_mixer_kernel, tiles_per_seq=tiles_per_seq),
        grid=(h.shape[0] // ts,),
        in_specs=[
            pl.BlockSpec((ts, D_MODEL), lambda i: (i, 0)),
            _resident((1, D_MODEL), layer),
            _resident((1, D_MODEL), layer),
            _resident((D_MODEL, D_PROJ_PACKED), layer),
            _resident((D_MODEL, D_MODEL), layer),
            _resident((POOL_WIDTH, POOL_WIDTH), layer),
            _resident((1, POOL_WIDTH), layer),
            _resident((8, CONV_WIDTH), layer),
            _resident((GL_PAD, GLA_KW), layer),
            _resident((1, GLA_KW), layer),
            _resident((1, GLA_WIDTH), layer),
        ],
        out_specs=pl.BlockSpec((ts, D_MODEL), lambda i: (i, 0)),
        out_shape=jax.ShapeDtypeStruct(h.shape, F32),
        scratch_shapes=[
            pltpu.VMEM((ts, D_MODEL), BF16),
            pltpu.VMEM((POOL_HIST + ts, POOL_WIDTH), F32),
            pltpu.VMEM((CONV_HIST + ts, CONV_WIDTH), F32),
            pltpu.VMEM((ts, GLA_KW), F32),
            pltpu.VMEM((ts, GLA_KW), F32),
            pltpu.VMEM((ts, GLA_KW), F32),
            pltpu.VMEM((3 * GLA_HEADS, ts, 128), F32),
            pltpu.VMEM((ts, GLA_KW), BF16),
            pltpu.VMEM((ts, GLA_KW), BF16),
            pltpu.VMEM((3 * GLA_HEADS, ts, 4 * GLA_DK), BF16),
            pltpu.VMEM((3 * GLA_HEADS, ts, 4 * GLA_DK), BF16),
            pltpu.VMEM((ts, GLA_WIDTH), BF16),
            pltpu.VMEM((ts, GLA_WIDTH), F32),
            pltpu.VMEM((ts, GLA_WIDTH), F32),
            pltpu.VMEM((ts // CHUNK * GLA_HEADS, GLA_DV, GLA_DK), F32),
            pltpu.VMEM((ts // CHUNK * GLA_HEADS, GLA_DV, GLA_DK), BF16),
            pltpu.VMEM((ts, D_MODEL), BF16),
            pltpu.VMEM((GLA_HEADS, GLA_DV, GLA_DK), F32),
            pltpu.VMEM((POOL_HIST, POOL_WIDTH), F32),
            pltpu.VMEM((CONV_HIST, CONV_WIDTH), F32),
            pltpu.VMEM((GLA_HEADS, GLA_DV, GLA_DK), F32),
            pltpu.SMEM((1,), F32),
        ],
        compiler_params=pltpu.CompilerParams(
            dimension_semantics=("arbitrary",), vmem_limit_bytes=VMEM_LIMIT),
        name="mixer",
    )(h, pre, post, win, wout, poolw, pscale, convw, w2, b2, gnorm)


def _pad_to(v, axis, size):
    pads = [(0, 0)] * v.ndim
    pads[axis] = (0, size - v.shape[axis])
    return jnp.pad(v, pads)


def _pack_w_in(w):
    pool, cb, cc, cu, q, k, v, gl, og = jnp.split(
        w, np.cumsum((256, 256, 256, 256, 256, 256, 512, 16, 512))[:-1].tolist(), axis=-1)
    return jnp.concatenate([pool, cb, cc, cu, q, k, v, og, _pad_to(gl, -1, GL_PAD)],
                           axis=-1).astype(BF16)


def _block_diag(w):
    out = jnp.zeros((w.shape[0], POOL_WIDTH, POOL_WIDTH), w.dtype)
    for g in range(len(POOL_WINDOWS)):
        sl = slice(g * POOL_GROUP, (g + 1) * POOL_GROUP)
        out = out.at[:, sl, sl].set(w[:, g])
    return out


def kernel(x, meta, ffn1_pre, ffn1_post, ffn1_wg, ffn1_wu, ffn1_wd, mix_pre, mix_post, w_in,
           pool_w, pool_scale, conv_w, gla_w2, gla_b2, gla_norm, w_out,
           ffn2_pre, ffn2_post, ffn2_wg, ffn2_wu, ffn2_wd):
    n_batch, seq, d = x.shape
    depth = w_in.shape[0]
    assert d == D_MODEL and seq % TILE == 0 and meta.shape[0] == N_META
    meta_tile = jnp.pad(meta.astype(F32), ((META_PAD, 0), (0, 0)))
    h = x.reshape(n_batch * seq, d)
    rows = lambda v: v.reshape(depth, 1, -1).astype(F32)
    mixer_params = (
        rows(mix_pre), rows(mix_post), _pack_w_in(w_in), w_out.astype(BF16),
        _block_diag(pool_w).astype(BF16), rows(pool_scale), _pad_to(conv_w.astype(F32), 1, 8),
        _pad_to(gla_w2, 1, GL_PAD).astype(BF16), rows(gla_b2), rows(gla_norm))
    ffn1 = (rows(ffn1_pre), rows(ffn1_post), ffn1_wg, ffn1_wu, ffn1_wd)
    ffn2 = (rows(ffn2_pre), rows(ffn2_post), ffn2_wg, ffn2_wu, ffn2_wd)
    for l in range(depth):
        h = _ffn(h, meta_tile, *ffn1, mode="first" if l == 0 else "mid", layer=l)
        h = _mixer(h, seq // TILE, *mixer_params, layer=l)
        h = _ffn(h, meta_tile, *ffn2, mode="last" if l == depth - 1 else "mid", layer=l)
    return h.reshape(n_batch, seq, d)
```

```python
import functools

import jax
import jax.numpy as jnp
from jax import lax
from jax.experimental import pallas as pl
from jax.experimental.pallas import tpu as pltpu

F32 = jnp.float32
BF16 = jnp.bfloat16

D_MODEL = 1024
D_FF = 2816
N_META = 16
EPS = 1e-6
FFN_RESID = 0.5

POOL_WIDTH = 256
POOL_WINDOWS = (2, 4, 8, 16)
POOL_GROUP = 64
CONV_WIDTH = 256
CONV_K = 3
GLA_HEADS = 4
GLA_DK = 64
GLA_DV = 128
GLA_KW = GLA_HEADS * GLA_DK
GLA_WIDTH = GLA_HEADS * GLA_DV
GLA_GATE_RANK = 16
GLA_TAU = 16.0
LOG2_E = 1.4426950408889634
SINGLE_REF_MAX_LOG2 = 100.0
CHUNK = 64


OFF_POOL, OFF_CB, OFF_CC, OFF_CU, OFF_Q, OFF_K = 0, 256, 512, 768, 1024, 1280
OFF_V, OFF_OG, OFF_GL = 1536, 2048, 2560
GL_PAD = 128
D_PROJ_PACKED = OFF_GL + GL_PAD
D_PROJ = OFF_OG + GLA_GATE_RANK + GLA_WIDTH

V7X_VMEM_BYTES = 64 * 1024 * 1024
VMEM_LIMIT = 56 * 1024 * 1024

TILE = 512
FF_CHUNK = 256
META_PAD = TILE - N_META
META_ROWS = 64
POOL_HIST = 16
CONV_HIST = 8


def _rms(x, g):
    return x * lax.rsqrt(jnp.mean(x * x, axis=-1, keepdims=True) + EPS) * g


def _load_ffn_weights(layer, wg_hbm, wu_hbm, wd_hbm, wg_ref, wu_ref, wd_ref, stage_cols,
                      stage_rows, sems, on_chunk):
    n_chunks = D_FF // FF_CHUNK

    def copies(j, slot):
        off = pl.multiple_of(j * FF_CHUNK, FF_CHUNK)
        return (
            pltpu.make_async_copy(wg_hbm.at[layer, :, pl.ds(off, FF_CHUNK)],
                                  stage_cols.at[0, slot], sems.at[0, slot]),
            pltpu.make_async_copy(wu_hbm.at[layer, :, pl.ds(off, FF_CHUNK)],
                                  stage_cols.at[1, slot], sems.at[1, slot]),
            pltpu.make_async_copy(wd_hbm.at[layer, pl.ds(off, FF_CHUNK), :],
                                  stage_rows.at[slot], sems.at[2, slot]),
        )

    for cp in copies(0, 0):
        cp.start()

    def body(j, carry):
        slot = lax.rem(j, 2)

        @pl.when(j + 1 < n_chunks)
        def _():
            for cp in copies(j + 1, 1 - slot):
                cp.start()

        for cp in copies(j, slot):
            cp.wait()
        off = pl.multiple_of(j * FF_CHUNK, FF_CHUNK)
        wg_ref[:, pl.ds(off, FF_CHUNK)] = (0.5 * stage_cols[0, slot]).astype(BF16)
        wu_ref[:, pl.ds(off, FF_CHUNK)] = stage_cols[1, slot].astype(BF16)
        wd_ref[pl.ds(off, FF_CHUNK), :] = stage_rows[slot].astype(BF16)
        on_chunk(off)
        return carry

    lax.fori_loop(0, n_chunks, body, 0)


def _ffn_kernel(h_ref, meta_ref, pre_ref, post_ref, wg_hbm, wu_hbm, wd_hbm, o_ref,
                a_ref, z_ref, wg_ref, wu_ref, wd_ref, stage_cols, stage_rows, sems, *,
                step0, layer):
    step = pl.program_id(0)

    def hidden_chunk(rows, cols):
        a = a_ref[rows, :]
        half_g = jnp.dot(a, wg_ref[:, cols], preferred_element_type=F32)
        u = jnp.dot(a, wu_ref[:, cols], preferred_element_type=F32)
        z_ref[rows, cols] = (half_g * (1.0 + jnp.tanh(half_g)) * u).astype(BF16)

    def finish(rows, h):
        f = jnp.dot(z_ref[rows, :], wd_ref[...], preferred_element_type=F32)
        o_ref[rows, :] = h + _rms(f, FFN_RESID * post_ref[...])

    @pl.when(step == 0)
    def _():
        rows = slice(0, TILE) if step0 == "tokens" else slice(TILE - META_ROWS, TILE)
        h = (meta_ref if step0 == "meta_input" else h_ref)[rows, :]
        a_ref[rows, :] = _rms(h, pre_ref[...]).astype(BF16)
        _load_ffn_weights(layer, wg_hbm, wu_hbm, wd_hbm, wg_ref, wu_ref, wd_ref, stage_cols,
                          stage_rows, sems,
                          lambda off: hidden_chunk(rows, pl.ds(off, FF_CHUNK)))
        if step0 != "tokens":
            o_ref[0:TILE - META_ROWS, :] = jnp.zeros((TILE - META_ROWS, D_MODEL), F32)
        finish(rows, h)

    @pl.when(step > 0)
    def _():
        h = h_ref[...]
        a_ref[...] = _rms(h, pre_ref[...]).astype(BF16)
        for j in range(D_FF // FF_CHUNK):
            hidden_chunk(slice(0, TILE), slice(j * FF_CHUNK, (j + 1) * FF_CHUNK))
        finish(slice(0, TILE), h)


def _resident(shape, layer=None):
    if layer is None:
        return pl.BlockSpec(shape, lambda *_: (0,) * len(shape), pipeline_mode=pl.Buffered(1))
    return pl.BlockSpec((None,) + tuple(shape), lambda *_: (layer,) + (0,) * len(shape),
                        pipeline_mode=pl.Buffered(1))


def _ffn(src, meta_tile, pre, post, wg, wu, wd, *, mode, layer):
    tile = TILE
    n_tiles = src.shape[0] // tile
    assert src.shape[0] % tile == 0
    if mode == "first":
        grid, src_map = n_tiles + 1, lambda i: (jnp.maximum(i - 1, 0), 0)
    elif mode == "mid":
        grid, src_map = n_tiles, lambda i: (i, 0)
    else:
        grid, src_map = n_tiles - 1, lambda i: (i + 1, 0)
    out_rows = grid * tile
    return pl.pallas_call(
        functools.partial(
            _ffn_kernel, layer=layer,
            step0={"first": "meta_input", "mid": "meta_block", "last": "tokens"}[mode]),
        grid=(grid,),
        in_specs=[
            pl.BlockSpec((tile, D_MODEL), src_map),
            _resident((TILE, D_MODEL)),
            _resident((1, D_MODEL), layer),
            _resident((1, D_MODEL), layer),
            pl.BlockSpec(memory_space=pl.ANY),
            pl.BlockSpec(memory_space=pl.ANY),
            pl.BlockSpec(memory_space=pl.ANY),
        ],
        out_specs=pl.BlockSpec((tile, D_MODEL), lambda i: (i, 0)),
        out_shape=jax.ShapeDtypeStruct((out_rows, D_MODEL), F32),
        scratch_shapes=[
            pltpu.VMEM((tile, D_MODEL), BF16),
            pltpu.VMEM((tile, D_FF), BF16),
            pltpu.VMEM((D_MODEL, D_FF), BF16),
            pltpu.VMEM((D_MODEL, D_FF), BF16),
            pltpu.VMEM((D_FF, D_MODEL), BF16),
            pltpu.VMEM((2, 2, D_MODEL, FF_CHUNK), F32),
            pltpu.VMEM((2, FF_CHUNK, D_MODEL), F32),
            pltpu.SemaphoreType.DMA((3, 2)),
        ],
        compiler_params=pltpu.CompilerParams(
            dimension_semantics=("arbitrary",), vmem_limit_bytes=VMEM_LIMIT),
        name="ffn_" + mode,
    )(src, meta_tile, pre, post, wg, wu, wd)


def _mixer_kernel(h_ref, pre_ref, post_ref, win_ref, wout_ref, poolw_ref, pscale_ref,
                  convw_ref, w2_ref, b2_ref, gnorm_ref, o_ref,
                  a_s, pbuf, zbuf, b_s, q_s, k_s, rep_s, qe_s, kd_s, lq_s, lk_s, v_s, gate_s, o_s, u_s, sb_s, y_s, state_s,
                  pmeta, zmeta, smeta, bmin_s, wout_s, win_s, *, tiles_per_seq):
    step = pl.program_id(0)
    ts = TILE
    is_meta = step == 0

    @pl.when(is_meta)
    def _():
        wout_s[...] = wout_ref[...].astype(BF16)
        win_s[:, 0:OFF_OG] = win_ref[:, 0:OFF_OG].astype(BF16)
        win_s[:, OFF_OG:OFF_GL] = win_ref[:, OFF_OG + GLA_GATE_RANK:D_PROJ].astype(BF16)
        low_rank = win_ref[:, OFF_OG:OFF_OG + GL_PAD]
        keep = lax.broadcasted_iota(jnp.int32, (1, GL_PAD), 1) < GLA_GATE_RANK
        win_s[:, OFF_GL:D_PROJ_PACKED] = jnp.where(keep, low_rank, 0.0).astype(BF16)
        pbuf[0:POOL_HIST, :] = jnp.zeros((POOL_HIST, POOL_WIDTH), F32)
        zbuf[0:CONV_HIST, :] = jnp.zeros((CONV_HIST, CONV_WIDTH), F32)
        state_s[...] = jnp.zeros_like(state_s)

    @pl.when(jnp.logical_and(step > 0, lax.rem(step - 1, tiles_per_seq) == 0))
    def _():
        pbuf[0:POOL_HIST, :] = pmeta[...]
        zbuf[0:CONV_HIST, :] = zmeta[...]
        state_s[...] = smeta[...]

    nc = ts // CHUNK
    ri = lax.broadcasted_iota(jnp.int32, (CHUNK, CHUNK), 0)
    ci = lax.broadcasted_iota(jnp.int32, (CHUNK, CHUNK), 1)

    def proj(off, width):
        return jnp.dot(a_s[...], win_s[:, off:off + width], preferred_element_type=F32)

    def pool_mixer():
        x1 = pbuf[...]
        s2 = x1 + pltpu.roll(x1, 1, 0)
        s4 = s2 + pltpu.roll(s2, 2, 0)
        s8 = s4 + pltpu.roll(s4, 4, 0)
        s16 = s8 + pltpu.roll(s8, 8, 0)
        lane = lax.broadcasted_iota(jnp.int32, (ts, POOL_WIDTH), 1)
        row = lax.broadcasted_iota(jnp.int32, (ts, POOL_WIDTH), 0)
        win = jnp.where(lane < 64, 2, jnp.where(lane < 128, 4, jnp.where(lane < 192, 8, 16)))
        wsum = jnp.where(lane < 64, s2[POOL_HIST:], jnp.where(
            lane < 128, s4[POOL_HIST:], jnp.where(lane < 192, s8[POOL_HIST:], s16[POOL_HIST:])))
        pos1 = jnp.where(is_meta, row - (META_PAD - 1), max(POOL_WINDOWS))
        cnt = jnp.clip(pos1, 1, win).astype(F32)
        m = (wsum / cnt - x1[POOL_HIST:]).astype(BF16)
        y_pool = jnp.dot(m, poolw_ref[...], preferred_element_type=F32) * pscale_ref[...]
        y_s[:, 0:POOL_WIDTH] = y_pool.astype(BF16)
        pbuf[0:POOL_HIST, :] = x1[ts:ts + POOL_HIST]

    def conv_mixer():
        zz = zbuf[...]
        z1 = pltpu.roll(zz, 1, 0)
        z2 = pltpu.roll(zz, 2, 0)
        cw = convw_ref[...]
        yc = (cw[0:1, :] * z2[CONV_HIST:] + cw[1:2, :] * z1[CONV_HIST:]
              + cw[2:3, :] * zz[CONV_HIST:])
        y_s[:, POOL_WIDTH:POOL_WIDTH + CONV_WIDTH] = (proj(OFF_CB, CONV_WIDTH) * yc).astype(BF16)
        zbuf[0:CONV_HIST, :] = zz[ts:ts + CONV_HIST]

    @pl.when(step >= 0)
    def _():
        a_s[...] = _rms(h_ref[...], pre_ref[...]).astype(BF16)
        g_low = proj(OFF_GL, GL_PAD).astype(BF16)
        gx = jnp.dot(g_low, w2_ref[...], preferred_element_type=F32) + b2_ref[...]
        log_sig = jnp.minimum(gx, 0.0) - jnp.log(1.0 + jnp.exp(-jnp.abs(gx)))
        la = log_sig * (LOG2_E / GLA_TAU)
        la_hi = la.astype(BF16)
        la_lo = (la - la_hi.astype(F32)).astype(BF16)
        la_cat = jnp.concatenate([la_hi, la_lo], axis=1)
        pbuf[POOL_HIST:, :] = proj(OFF_POOL, POOL_WIDTH)
        zbuf[CONV_HIST:, :] = proj(OFF_CC, CONV_WIDTH) * proj(OFF_CU, CONV_WIDTH)
        tril = jnp.where(ri >= ci, 1.0, 0.0).astype(BF16)
        for c in range(nc):
            rows = slice(c * CHUNK, (c + 1) * CHUNK)
            bb = jnp.dot(tril, la_cat[rows, :], preferred_element_type=F32)
            b_s[rows, :] = bb[:, :GLA_KW] + bb[:, GLA_KW:]
        q_s[...] = proj(OFF_Q, GLA_KW) * (GLA_DK ** -0.5)
        k_s[...] = proj(OFF_K, GLA_KW)
        pool_mixer()
        conv_mixer()
        v_s[...] = proj(OFF_V, GLA_WIDTH).astype(BF16)
        og = proj(OFF_OG, GLA_WIDTH)
        gate_s[...] = og * jax.nn.sigmoid(og) * gnorm_ref[...]
        bmin_s[0] = jnp.min(b_s[...])

    b, q, k = b_s[...], q_s[...], k_s[...]
    b3 = b.reshape(nc, CHUNK, GLA_KW)
    b_last = b3[:, CHUNK - 1:CHUNK, :]
    qe_s[...] = (q.reshape(b3.shape) * jnp.exp2(b3)).reshape(ts, GLA_KW).astype(BF16)
    kd_s[...] = (k.reshape(b3.shape) * jnp.exp2(b_last - b3)).reshape(ts, GLA_KW).astype(BF16)
    decay = jnp.exp2(b_last)

    nt = (((1,), (1,)), ((), ()))
    tn = (((0,), (0,)), ((), ()))
    ksl = [slice(hd * GLA_DK, (hd + 1) * GLA_DK) for hd in range(GLA_HEADS)]
    vsl = [slice(hd * GLA_DV, (hd + 1) * GLA_DV) for hd in range(GLA_HEADS)]
    pairs = [(c, hd) for c in range(nc) for hd in range(GLA_HEADS)]
    group = 2 * GLA_HEADS
    chunk_rows = lambda c: slice(c * CHUNK, (c + 1) * CHUNK)

    single_ref_ok = bmin_s[0] > -SINGLE_REF_MAX_LOG2

    def state_increments(grp):
        return [lax.dot_general(v_s[chunk_rows(c), vsl[hd]], kd_s[chunk_rows(c), ksl[hd]], tn,
                                preferred_element_type=F32) for c, hd in grp]

    def finish_group(grp, ps, us):
        pvs = [jnp.dot(p, v_s[chunk_rows(c), vsl[hd]], preferred_element_type=F32)
               for p, (c, hd) in zip(ps, grp)]
        for pv, u, (c, hd) in zip(pvs, us, grp):
            o_s[chunk_rows(c), vsl[hd]] = pv
            u_s[c * GLA_HEADS + hd] = u

    @pl.when(single_ref_ok)
    def _():
        lk_s[0] = (k.reshape(b3.shape) * jnp.exp2(-b3)).reshape(ts, GLA_KW).astype(BF16)
        causal = ri >= ci
        for g0 in range(0, len(pairs), group):
            grp = pairs[g0:g0 + group]
            scs = [lax.dot_general(qe_s[chunk_rows(c), ksl[hd]], lk_s[0, chunk_rows(c), ksl[hd]],
                                   nt, preferred_element_type=F32) for c, hd in grp]
            us = state_increments(grp)
            finish_group(grp, [jnp.where(causal, sc, 0.0).astype(BF16) for sc in scs], us)

    lane_hi = lax.broadcasted_iota(jnp.int32, (1, 1, 128), 2) >= GLA_DK
    t64 = lax.broadcasted_iota(jnp.int32, (1, CHUNK, 1), 1)
    blk16, blk4, pos4 = t64 // 16, (t64 // 4) % 4, t64 % 4
    in_hi4 = lax.broadcasted_iota(jnp.int32, (1, 8, 1), 1) >= 4
    neg = -1e30

    def replicate_heads():
        for n, src in enumerate((b_s, q_s, k_s)):
            for c2 in range(2):
                x = src[:, 128 * c2:128 * c2 + 128]
                x_sw = pltpu.roll(x, GLA_DK, 1)
                rep_s[GLA_HEADS * n + 2 * c2] = jnp.where(lane_hi[0], x_sw, x)
                rep_s[GLA_HEADS * n + 2 * c2 + 1] = jnp.where(lane_hi[0], x, x_sw)

    def build_level_operands(hd):
        rb, rq, rk = (rep_s[GLA_HEADS * n + hd] for n in range(3))
        rb64, rq64, rk64 = (x.reshape(nc, CHUNK, 128) for x in (rb, rq, rk))
        rb16 = rb.reshape(ts // 16, 16, 128)
        rb8 = rb.reshape(ts // 8, 8, 128)
        to64 = lambda x: x.reshape(nc, CHUNK, 128)
        end4 = jnp.where(in_hi4, rb8[:, 7:8, :], rb8[:, 3:4, :])
        k1 = rk64 * to64(jnp.exp2(rb16[:, 15:16, :] - rb16))
        k2 = rk64 * to64(jnp.exp2(end4 - rb8))
        nxt = jnp.where(lane_hi, pltpu.roll(rb8, 7, 1), rb8)
        for p in range(2):
            col = slice(128 * p, 128 * p + 128)
            j = jnp.where(lane_hi, 2 * p + 1, 2 * p)
            ref1 = jnp.where(lane_hi, rb64[:, 32 * p + 31:32 * p + 32, :],
                             rb64[:, 32 * p + 15:32 * p + 16, :])
            ref2 = jnp.where(lane_hi, rb16[:, 8 * p + 7:8 * p + 8, :], rb16[:, 8 * p + 3:8 * p + 4, :])
            ref3 = jnp.where(in_hi4, nxt[:, 4 + 2 * p:5 + 2 * p, :], nxt[:, 2 * p:2 * p + 1, :])
            q1 = rq64 * jnp.exp2(jnp.where(blk16 > j, rb64 - ref1, neg))
            q2 = rq64 * jnp.exp2(jnp.where(blk4 > j, to64(rb16 - ref2), neg))
            q3 = rq64 * jnp.exp2(jnp.where(pos4 >= j, to64(rb8 - ref3), neg))
            for n, (ql, kl) in enumerate(((q1, jnp.where(blk16 == j, k1, 0.0)),
                                          (q2, jnp.where(blk4 == j, k2, 0.0)),
                                          (q3, jnp.where(pos4 == j, rk64, 0.0)))):
                lq_s[3 * hd + n, :, col] = ql.reshape(ts, 128).astype(BF16)
                lk_s[3 * hd + n, :, col] = kl.reshape(ts, 128).astype(BF16)

    @pl.when(jnp.logical_not(single_ref_ok))
    def _():
        replicate_heads()
        for hd in range(GLA_HEADS):
            build_level_operands(hd)
        same16 = (ri // 16) == (ci // 16)
        same4 = (ri // 4) == (ci // 4)
        for g0 in range(0, len(pairs), group):
            grp = pairs[g0:g0 + group]
            lvl = [[lax.dot_general(lq_s[3 * hd + n, chunk_rows(c), :],
                                    lk_s[3 * hd + n, chunk_rows(c), :], nt,
                                    preferred_element_type=F32) for n in range(3)]
                   for c, hd in grp]
            us = state_increments(grp)
            finish_group(grp, [(g1 + jnp.where(same16, g2, 0.0)
                                + jnp.where(same4, g3, 0.0)).astype(BF16) for g1, g2, g3 in lvl], us)


    sts = [state_s[hd] for hd in range(GLA_HEADS)]
    for c in range(nc):
        for hd in range(GLA_HEADS):
            sb_s[c * GLA_HEADS + hd] = sts[hd].astype(BF16)
        sts = [sts[hd] * decay[c][:, ksl[hd]] + u_s[c * GLA_HEADS + hd]
               for hd in range(GLA_HEADS)]
    for hd in range(GLA_HEADS):
        state_s[hd] = sts[hd]

    for g0 in range(0, len(pairs), group):
        grp = pairs[g0:g0 + group]
        inters = [lax.dot_general(qe_s[c * CHUNK:(c + 1) * CHUNK, ksl[hd]],
                                  sb_s[c * GLA_HEADS + hd], nt, preferred_element_type=F32)
                  for c, hd in grp]
        for inter, (c, hd) in zip(inters, grp):
            o_s[chunk_rows(c), vsl[hd]] = o_s[chunk_rows(c), vsl[hd]] + inter

    for hd in range(GLA_HEADS):
        vs = slice(hd * GLA_DV, (hd + 1) * GLA_DV)
        o = o_s[:, vs]
        o = o * lax.rsqrt(jnp.mean(o * o, axis=-1, keepdims=True) + EPS)
        y_s[:, POOL_WIDTH + CONV_WIDTH + hd * GLA_DV:POOL_WIDTH + CONV_WIDTH + (hd + 1) * GLA_DV] = (
            o * gate_s[:, vs]).astype(BF16)

    mix = jnp.dot(y_s[...], wout_s[...], preferred_element_type=F32)
    o_ref[...] = h_ref[...] + _rms(mix, post_ref[...])

    @pl.when(is_meta)
    def _():
        pmeta[...] = pbuf[0:POOL_HIST, :]
        zmeta[...] = zbuf[0:CONV_HIST, :]
        smeta[...] = state_s[...]


def _mixer(h, tiles_per_seq, pre, post, win, wout, poolw, pscale, convw, w2, b2, gnorm, *,
           layer):
    ts = TILE
    return pl.pallas_call(
        functools.partial(_mixer_kernel, tiles_per_seq=tiles_per_seq),
        grid=(h.shape[0] // ts,),
        in_specs=[
            pl.BlockSpec((ts, D_MODEL), lambda i: (i, 0)),
            _resident((1, D_MODEL), layer),
            _resident((1, D_MODEL), layer),
            _resident((D_MODEL, D_PROJ), layer),
            _resident((D_MODEL, D_MODEL), layer),
            _resident((POOL_WIDTH, POOL_WIDTH), layer),
            _resident((1, POOL_WIDTH), layer),
            _resident((8, CONV_WIDTH), layer),
            _resident((GL_PAD, GLA_KW), layer),
            _resident((1, GLA_KW), layer),
            _resident((1, GLA_WIDTH), layer),
        ],
        out_specs=pl.BlockSpec((ts, D_MODEL), lambda i: (i, 0)),
        out_shape=jax.ShapeDtypeStruct(h.shape, F32),
        scratch_shapes=[
            pltpu.VMEM((ts, D_MODEL), BF16),
            pltpu.VMEM((POOL_HIST + ts, POOL_WIDTH), F32),
            pltpu.VMEM((CONV_HIST + ts, CONV_WIDTH), F32),
            pltpu.VMEM((ts, GLA_KW), F32),
            pltpu.VMEM((ts, GLA_KW), F32),
            pltpu.VMEM((ts, GLA_KW), F32),
            pltpu.VMEM((3 * GLA_HEADS, ts, 128), F32),
            pltpu.VMEM((ts, GLA_KW), BF16),
            pltpu.VMEM((ts, GLA_KW), BF16),
            pltpu.VMEM((3 * GLA_HEADS, ts, 4 * GLA_DK), BF16),
            pltpu.VMEM((3 * GLA_HEADS, ts, 4 * GLA_DK), BF16),
            pltpu.VMEM((ts, GLA_WIDTH), BF16),
            pltpu.VMEM((ts, GLA_WIDTH), F32),
            pltpu.VMEM((ts, GLA_WIDTH), F32),
            pltpu.VMEM((ts // CHUNK * GLA_HEADS, GLA_DV, GLA_DK), F32),
            pltpu.VMEM((ts // CHUNK * GLA_HEADS, GLA_DV, GLA_DK), BF16),
            pltpu.VMEM((ts, D_MODEL), BF16),
            pltpu.VMEM((GLA_HEADS, GLA_DV, GLA_DK), F32),
            pltpu.VMEM((POOL_HIST, POOL_WIDTH), F32),
            pltpu.VMEM((CONV_HIST, CONV_WIDTH), F32),
            pltpu.VMEM((GLA_HEADS, GLA_DV, GLA_DK), F32),
            pltpu.SMEM((1,), F32),
            pltpu.VMEM((D_MODEL, D_MODEL), BF16),
            pltpu.VMEM((D_MODEL, D_PROJ_PACKED), BF16),
        ],
        compiler_params=pltpu.CompilerParams(
            dimension_semantics=("arbitrary",), vmem_limit_bytes=VMEM_LIMIT),
        name="mixer",
    )(h, pre, post, win, wout, poolw, pscale, convw, w2, b2, gnorm)


def _pad_to(v, axis, size):
    pads = [(0, 0)] * v.ndim
    pads[axis] = (0, size - v.shape[axis])
    return jnp.pad(v, pads)


def _block_diag(w):
    out = jnp.zeros((w.shape[0], POOL_WIDTH, POOL_WIDTH), w.dtype)
    for g in range(len(POOL_WINDOWS)):
        sl = slice(g * POOL_GROUP, (g + 1) * POOL_GROUP)
        out = out.at[:, sl, sl].set(w[:, g])
    return out


def kernel(x, meta, ffn1_pre, ffn1_post, ffn1_wg, ffn1_wu, ffn1_wd, mix_pre, mix_post, w_in,
           pool_w, pool_scale, conv_w, gla_w2, gla_b2, gla_norm, w_out,
           ffn2_pre, ffn2_post, ffn2_wg, ffn2_wu, ffn2_wd):
    n_batch, seq, d = x.shape
    depth = w_in.shape[0]
    assert d == D_MODEL and seq % TILE == 0 and meta.shape[0] == N_META
    meta_tile = jnp.pad(meta.astype(F32), ((META_PAD, 0), (0, 0)))
    h = x.reshape(n_batch * seq, d)
    rows = lambda v: v.reshape(depth, 1, -1).astype(F32)
    mixer_small = (
        _block_diag(pool_w).astype(BF16), rows(pool_scale), _pad_to(conv_w.astype(F32), 1, 8),
        _pad_to(gla_w2, 1, GL_PAD).astype(BF16), rows(gla_b2), rows(gla_norm))
    mix_pre, mix_post = rows(mix_pre), rows(mix_post)
    ffn1 = (rows(ffn1_pre), rows(ffn1_post), ffn1_wg, ffn1_wu, ffn1_wd)
    ffn2 = (rows(ffn2_pre), rows(ffn2_post), ffn2_wg, ffn2_wu, ffn2_wd)
    for l in range(depth):
        h = _ffn(h, meta_tile, *ffn1, mode="first" if l == 0 else "mid", layer=l)
        h = _mixer(h, seq // TILE, mix_pre, mix_post, w_in, w_out, *mixer_small, layer=l)
        h = _ffn(h, meta_tile, *ffn2, mode="last" if l == depth - 1 else "mid", layer=l)
    return h.reshape(n_batch, seq, d)
```

```python
import functools

import jax
import jax.numpy as jnp
from jax import lax
from jax.experimental import pallas as pl
from jax.experimental.pallas import tpu as pltpu

F32 = jnp.float32
BF16 = jnp.bfloat16

D_MODEL = 1024
D_FF = 2816
N_META = 16
EPS = 1e-6
FFN_RESID = 0.5

POOL_WIDTH = 256
POOL_WINDOWS = (2, 4, 8, 16)
POOL_GROUP = 64
CONV_WIDTH = 256
CONV_K = 3
GLA_HEADS = 4
GLA_DK = 64
GLA_DV = 128
GLA_KW = GLA_HEADS * GLA_DK
GLA_WIDTH = GLA_HEADS * GLA_DV
GLA_GATE_RANK = 16
GLA_TAU = 16.0
LOG2_E = 1.4426950408889634
SINGLE_REF_MAX_LOG2 = 40.0
CHUNK = 64


OFF_POOL, OFF_CB, OFF_CC, OFF_CU, OFF_Q, OFF_K = 0, 256, 512, 768, 1024, 1280
OFF_V, OFF_OG, OFF_GL = 1536, 2048, 2560
GL_PAD = 128
D_PROJ_PACKED = OFF_GL + GL_PAD
D_PROJ = OFF_OG + GLA_GATE_RANK + GLA_WIDTH

V7X_VMEM_BYTES = 64 * 1024 * 1024
COMPILER_SCRATCH_BYTES = 8 * 1024 * 1024
VMEM_LIMIT = V7X_VMEM_BYTES - COMPILER_SCRATCH_BYTES

TILE = 512
FF_CHUNK = 256
META_PAD = TILE - N_META
META_ROWS = 64
ROW_POOL_SCALE, ROW_GATE_BIAS, ROW_CONV = 0, 1, 2
SMALL_ROWS = 16
POOL_HIST = 16
CONV_HIST = 16


def _rms(x, g):
    return x * lax.rsqrt(jnp.mean(x * x, axis=-1, keepdims=True) + EPS) * g


def _load_ffn_weights(layer, wg_hbm, wu_hbm, wd_hbm, wg_ref, wu_ref, wd_ref, stage_cols,
                      stage_rows, sems, on_chunk):
    n_chunks = D_FF // FF_CHUNK

    def copies(j, slot):
        off = pl.multiple_of(j * FF_CHUNK, FF_CHUNK)
        return (
            pltpu.make_async_copy(wg_hbm.at[layer, :, pl.ds(off, FF_CHUNK)],
                                  stage_cols.at[0, slot], sems.at[0, slot]),
            pltpu.make_async_copy(wu_hbm.at[layer, :, pl.ds(off, FF_CHUNK)],
                                  stage_cols.at[1, slot], sems.at[1, slot]),
            pltpu.make_async_copy(wd_hbm.at[layer, pl.ds(off, FF_CHUNK), :],
                                  stage_rows.at[slot], sems.at[2, slot]),
        )

    for cp in copies(0, 0):
        cp.start()

    def body(j, carry):
        slot = lax.rem(j, 2)

        @pl.when(j + 1 < n_chunks)
        def _():
            for cp in copies(j + 1, 1 - slot):
                cp.start()

        for cp in copies(j, slot):
            cp.wait()
        off = pl.multiple_of(j * FF_CHUNK, FF_CHUNK)
        wg_ref[:, pl.ds(off, FF_CHUNK)] = (0.5 * stage_cols[0, slot]).astype(BF16)
        wu_ref[:, pl.ds(off, FF_CHUNK)] = stage_cols[1, slot].astype(BF16)
        wd_ref[pl.ds(off, FF_CHUNK), :] = stage_rows[slot].astype(BF16)
        on_chunk(off)
        return carry

    lax.fori_loop(0, n_chunks, body, 0)


def _ffn_kernel(h_ref, meta_ref, pre_ref, post_ref, wg_hbm, wu_hbm, wd_hbm, o_ref,
                a_ref, z_ref, wg_ref, wu_ref, wd_ref, stage_cols, stage_rows, sems, *,
                step0, layer):
    step = pl.program_id(0)

    def hidden_chunk(rows, cols):
        a = a_ref[rows, :]
        half_g = jnp.dot(a, wg_ref[:, cols], preferred_element_type=F32)
        u = jnp.dot(a, wu_ref[:, cols], preferred_element_type=F32)
        z_ref[rows, cols] = (half_g * (1.0 + jnp.tanh(half_g)) * u).astype(BF16)

    def finish(rows, h):
        f = jnp.dot(z_ref[rows, :], wd_ref[...], preferred_element_type=F32)
        o_ref[rows, :] = h + _rms(f, FFN_RESID * post_ref[...])

    @pl.when(step == 0)
    def _():
        rows = slice(0, TILE) if step0 == "tokens" else slice(TILE - META_ROWS, TILE)
        h = (meta_ref if step0 == "meta_input" else h_ref)[rows, :]
        a_ref[rows, :] = _rms(h, pre_ref[...]).astype(BF16)
        _load_ffn_weights(layer, wg_hbm, wu_hbm, wd_hbm, wg_ref, wu_ref, wd_ref, stage_cols,
                          stage_rows, sems,
                          lambda off: hidden_chunk(rows, pl.ds(off, FF_CHUNK)))
        if step0 != "tokens":
            o_ref[0:TILE - META_ROWS, :] = jnp.zeros((TILE - META_ROWS, D_MODEL), F32)
        finish(rows, h)

    @pl.when(step > 0)
    def _():
        h = h_ref[...]
        a_ref[...] = _rms(h, pre_ref[...]).astype(BF16)
        for j in range(D_FF // FF_CHUNK):
            hidden_chunk(slice(0, TILE), slice(j * FF_CHUNK, (j + 1) * FF_CHUNK))
        finish(slice(0, TILE), h)


def _resident(shape, layer=None):
    if layer is None:
        return pl.BlockSpec(shape, lambda *_: (0,) * len(shape), pipeline_mode=pl.Buffered(1))
    return pl.BlockSpec((None,) + tuple(shape), lambda *_: (layer,) + (0,) * len(shape),
                        pipeline_mode=pl.Buffered(1))


def _ffn(src, meta_tile, pre, post, wg, wu, wd, *, mode, layer):
    tile = TILE
    n_tiles = src.shape[0] // tile
    assert src.shape[0] % tile == 0
    if mode == "first":
        grid, src_map = n_tiles + 1, lambda i: (jnp.maximum(i - 1, 0), 0)
    elif mode == "mid":
        grid, src_map = n_tiles, lambda i: (i, 0)
    else:
        grid, src_map = n_tiles - 1, lambda i: (i + 1, 0)
    out_rows = grid * tile
    return pl.pallas_call(
        functools.partial(
            _ffn_kernel, layer=layer,
            step0={"first": "meta_input", "mid": "meta_block", "last": "tokens"}[mode]),
        grid=(grid,),
        in_specs=[
            pl.BlockSpec((tile, D_MODEL), src_map),
            _resident((TILE, D_MODEL)),
            _resident((1, D_MODEL), layer),
            _resident((1, D_MODEL), layer),
            pl.BlockSpec(memory_space=pl.ANY),
            pl.BlockSpec(memory_space=pl.ANY),
            pl.BlockSpec(memory_space=pl.ANY),
        ],
        out_specs=pl.BlockSpec((tile, D_MODEL), lambda i: (i, 0)),
        out_shape=jax.ShapeDtypeStruct((out_rows, D_MODEL), F32),
        scratch_shapes=[
            pltpu.VMEM((tile, D_MODEL), BF16),
            pltpu.VMEM((tile, D_FF), BF16),
            pltpu.VMEM((D_MODEL, D_FF), BF16),
            pltpu.VMEM((D_MODEL, D_FF), BF16),
            pltpu.VMEM((D_FF, D_MODEL), BF16),
            pltpu.VMEM((2, 2, D_MODEL, FF_CHUNK), F32),
            pltpu.VMEM((2, FF_CHUNK, D_MODEL), F32),
            pltpu.SemaphoreType.DMA((3, 2)),
        ],
        compiler_params=pltpu.CompilerParams(
            dimension_semantics=("arbitrary",), vmem_limit_bytes=VMEM_LIMIT),
        name="ffn_" + mode,
    )(src, meta_tile, pre, post, wg, wu, wd)


def _mixer_kernel(h_ref, pre_ref, post_ref, win_ref, wout_ref, poolw_ref, rows_ref,
                  w2_ref, gnorm_ref, o_ref,
                  a_s, pbuf, zbuf, b_s, q_s, k_s, rep_s, qe_s, kd_s, lq_s, lk_s, v_s, gate_s,
                  o_s, u_s, sb_s, y_s, state_s, pmeta, zmeta, smeta, bmin_s, wout_s, win_s, *,
                  tiles_per_seq):
    step = pl.program_id(0)
    ts = TILE
    is_meta = step == 0

    @pl.when(is_meta)
    def _():
        wout_s[...] = wout_ref[...].astype(BF16)
        win_s[:, 0:OFF_OG] = win_ref[:, 0:OFF_OG].astype(BF16)
        win_s[:, OFF_OG:OFF_GL] = win_ref[:, OFF_OG + GLA_GATE_RANK:D_PROJ].astype(BF16)
        low_rank = win_ref[:, OFF_OG:OFF_OG + GL_PAD]
        keep = lax.broadcasted_iota(jnp.int32, (1, GL_PAD), 1) < GLA_GATE_RANK
        win_s[:, OFF_GL:D_PROJ_PACKED] = jnp.where(keep, low_rank, 0.0).astype(BF16)
        pbuf[0:POOL_HIST, :] = jnp.zeros((POOL_HIST, POOL_WIDTH), F32)
        zbuf[0:CONV_HIST, :] = jnp.zeros((CONV_HIST, CONV_WIDTH), F32)
        state_s[...] = jnp.zeros_like(state_s)

    @pl.when(jnp.logical_and(step > 0, lax.rem(step - 1, tiles_per_seq) == 0))
    def _():
        pbuf[0:POOL_HIST, :] = pmeta[...]
        zbuf[0:CONV_HIST, :] = zmeta[...]
        state_s[...] = smeta[...]

    nc = ts // CHUNK
    ri = lax.broadcasted_iota(jnp.int32, (CHUNK, CHUNK), 0)
    ci = lax.broadcasted_iota(jnp.int32, (CHUNK, CHUNK), 1)

    def proj(off, width):
        return jnp.dot(a_s[...], win_s[:, off:off + width], preferred_element_type=F32)

    def pool_mixer():
        x1 = pbuf[...]
        s2 = x1 + pltpu.roll(x1, 1, 0)
        s4 = s2 + pltpu.roll(s2, 2, 0)
        s8 = s4 + pltpu.roll(s4, 4, 0)
        s16 = s8 + pltpu.roll(s8, 8, 0)
        lane = lax.broadcasted_iota(jnp.int32, (ts, POOL_WIDTH), 1)
        row = lax.broadcasted_iota(jnp.int32, (ts, POOL_WIDTH), 0)
        win = jnp.where(lane < 64, 2, jnp.where(lane < 128, 4, jnp.where(lane < 192, 8, 16)))
        wsum = jnp.where(lane < 64, s2[POOL_HIST:], jnp.where(
            lane < 128, s4[POOL_HIST:], jnp.where(lane < 192, s8[POOL_HIST:], s16[POOL_HIST:])))
        pos1 = jnp.where(is_meta, row - (META_PAD - 1), max(POOL_WINDOWS))
        cnt = jnp.clip(pos1, 1, win).astype(F32)
        m = (wsum / cnt - x1[POOL_HIST:]).astype(BF16)
        y_pool = (jnp.dot(m, poolw_ref[...], preferred_element_type=F32)
                  * rows_ref[ROW_POOL_SCALE:ROW_POOL_SCALE + 1, :])
        y_s[:, 0:POOL_WIDTH] = y_pool.astype(BF16)
        pbuf[0:POOL_HIST, :] = x1[ts:ts + POOL_HIST]

    def conv_mixer():
        zz = zbuf[...]
        z1 = pltpu.roll(zz, 1, 0)
        z2 = pltpu.roll(zz, 2, 0)
        cw = rows_ref[ROW_CONV:ROW_CONV + CONV_K, :]
        yc = (cw[0:1, :] * z2[CONV_HIST:] + cw[1:2, :] * z1[CONV_HIST:]
              + cw[2:3, :] * zz[CONV_HIST:])
        y_s[:, POOL_WIDTH:POOL_WIDTH + CONV_WIDTH] = (proj(OFF_CB, CONV_WIDTH) * yc).astype(BF16)
        zbuf[0:CONV_HIST, :] = zz[ts:ts + CONV_HIST]

    @pl.when(step >= 0)
    def _():
        a_s[...] = _rms(h_ref[...], pre_ref[...]).astype(BF16)
        g_low = proj(OFF_GL, GL_PAD).astype(BF16)
        gx = (jnp.dot(g_low, w2_ref[...], preferred_element_type=F32)
              + rows_ref[ROW_GATE_BIAS:ROW_GATE_BIAS + 1, :])
        log_sig = jnp.minimum(gx, 0.0) - jnp.log(1.0 + jnp.exp(-jnp.abs(gx)))
        la = log_sig * (LOG2_E / GLA_TAU)
        la_hi = la.astype(BF16)
        la_lo = (la - la_hi.astype(F32)).astype(BF16)
        la_cat = jnp.concatenate([la_hi, la_lo], axis=1)
        pbuf[POOL_HIST:, :] = proj(OFF_POOL, POOL_WIDTH)
        zbuf[CONV_HIST:, :] = proj(OFF_CC, CONV_WIDTH) * proj(OFF_CU, CONV_WIDTH)
        tril = jnp.where(ri >= ci, 1.0, 0.0).astype(BF16)
        for c in range(nc):
            rows = slice(c * CHUNK, (c + 1) * CHUNK)
            bb = jnp.dot(tril, la_cat[rows, :], preferred_element_type=F32)
            b_s[rows, :] = bb[:, :GLA_KW] + bb[:, GLA_KW:]
        q_s[...] = proj(OFF_Q, GLA_KW) * (GLA_DK ** -0.5)
        k_s[...] = proj(OFF_K, GLA_KW)
        pool_mixer()
        conv_mixer()
        v_s[...] = proj(OFF_V, GLA_WIDTH).astype(BF16)
        og = proj(OFF_OG, GLA_WIDTH)
        gate_s[...] = og * jax.nn.sigmoid(og) * gnorm_ref[...]
        bmin_s[0] = jnp.min(b_s[...])

    b, q, k = b_s[...], q_s[...], k_s[...]
    b3 = b.reshape(nc, CHUNK, GLA_KW)
    b_last = b3[:, CHUNK - 1:CHUNK, :]
    qe_s[...] = (q.reshape(b3.shape) * jnp.exp2(b3)).reshape(ts, GLA_KW).astype(BF16)
    kd_s[...] = (k.reshape(b3.shape) * jnp.exp2(b_last - b3)).reshape(ts, GLA_KW).astype(BF16)
    decay = jnp.exp2(b_last)

    nt = (((1,), (1,)), ((), ()))
    tn = (((0,), (0,)), ((), ()))
    ksl = [slice(hd * GLA_DK, (hd + 1) * GLA_DK) for hd in range(GLA_HEADS)]
    vsl = [slice(hd * GLA_DV, (hd + 1) * GLA_DV) for hd in range(GLA_HEADS)]
    pairs = [(c, hd) for c in range(nc) for hd in range(GLA_HEADS)]
    group = 2 * GLA_HEADS
    chunk_rows = lambda c: slice(c * CHUNK, (c + 1) * CHUNK)

    single_ref_ok = bmin_s[0] > -SINGLE_REF_MAX_LOG2

    def state_increments(grp):
        return [lax.dot_general(v_s[chunk_rows(c), vsl[hd]], kd_s[chunk_rows(c), ksl[hd]], tn,
                                preferred_element_type=F32) for c, hd in grp]

    def finish_group(grp, ps, us):
        pvs = [jnp.dot(p, v_s[chunk_rows(c), vsl[hd]], preferred_element_type=F32)
               for p, (c, hd) in zip(ps, grp)]
        for pv, u, (c, hd) in zip(pvs, us, grp):
            o_s[chunk_rows(c), vsl[hd]] = pv
            u_s[c * GLA_HEADS + hd] = u

    @pl.when(single_ref_ok)
    def _():
        lk_s[0] = (k.reshape(b3.shape) * jnp.exp2(-b3)).reshape(ts, GLA_KW).astype(BF16)
        causal = ri >= ci
        for g0 in range(0, len(pairs), group):
            grp = pairs[g0:g0 + group]
            scs = [lax.dot_general(qe_s[chunk_rows(c), ksl[hd]], lk_s[0, chunk_rows(c), ksl[hd]],
                                   nt, preferred_element_type=F32) for c, hd in grp]
            us = state_increments(grp)
            finish_group(grp, [jnp.where(causal, sc, 0.0).astype(BF16) for sc in scs], us)

    lane_hi = lax.broadcasted_iota(jnp.int32, (1, 1, 128), 2) >= GLA_DK
    t64 = lax.broadcasted_iota(jnp.int32, (1, CHUNK, 1), 1)
    blk16, blk4, pos4 = t64 // 16, (t64 // 4) % 4, t64 % 4
    in_hi4 = lax.broadcasted_iota(jnp.int32, (1, 8, 1), 1) >= 4
    neg = -1e30

    def replicate_heads():
        for n, src in enumerate((b_s, q_s, k_s)):
            for c2 in range(2):
                x = src[:, 128 * c2:128 * c2 + 128]
                x_sw = pltpu.roll(x, GLA_DK, 1)
                rep_s[GLA_HEADS * n + 2 * c2] = jnp.where(lane_hi[0], x_sw, x)
                rep_s[GLA_HEADS * n + 2 * c2 + 1] = jnp.where(lane_hi[0], x, x_sw)

    def build_level_operands(hd):
        rb, rq, rk = (rep_s[GLA_HEADS * n + hd] for n in range(3))
        rb64, rq64, rk64 = (x.reshape(nc, CHUNK, 128) for x in (rb, rq, rk))
        rb16 = rb.reshape(ts // 16, 16, 128)
        rb8 = rb.reshape(ts // 8, 8, 128)
        to64 = lambda x: x.reshape(nc, CHUNK, 128)
        end4 = jnp.where(in_hi4, rb8[:, 7:8, :], rb8[:, 3:4, :])
        k1 = rk64 * to64(jnp.exp2(rb16[:, 15:16, :] - rb16))
        k2 = rk64 * to64(jnp.exp2(end4 - rb8))
        nxt = jnp.where(lane_hi, pltpu.roll(rb8, 7, 1), rb8)
        for p in range(2):
            col = slice(128 * p, 128 * p + 128)
            j = jnp.where(lane_hi, 2 * p + 1, 2 * p)
            ref1 = jnp.where(lane_hi, rb64[:, 32 * p + 31:32 * p + 32, :],
                             rb64[:, 32 * p + 15:32 * p + 16, :])
            ref2 = jnp.where(lane_hi, rb16[:, 8 * p + 7:8 * p + 8, :], rb16[:, 8 * p + 3:8 * p + 4, :])
            ref3 = jnp.where(in_hi4, nxt[:, 4 + 2 * p:5 + 2 * p, :], nxt[:, 2 * p:2 * p + 1, :])
            q1 = rq64 * jnp.exp2(jnp.where(blk16 > j, rb64 - ref1, neg))
            q2 = rq64 * jnp.exp2(jnp.where(blk4 > j, to64(rb16 - ref2), neg))
            q3 = rq64 * jnp.exp2(jnp.where(pos4 >= j, to64(rb8 - ref3), neg))
            for n, (ql, kl) in enumerate(((q1, jnp.where(blk16 == j, k1, 0.0)),
                                          (q2, jnp.where(blk4 == j, k2, 0.0)),
                                          (q3, jnp.where(pos4 == j, rk64, 0.0)))):
                lq_s[3 * hd + n, :, col] = ql.reshape(ts, 128).astype(BF16)
                lk_s[3 * hd + n, :, col] = kl.reshape(ts, 128).astype(BF16)

    @pl.when(jnp.logical_not(single_ref_ok))
    def _():
        replicate_heads()
        for hd in range(GLA_HEADS):
            build_level_operands(hd)
        same16 = (ri // 16) == (ci // 16)
        same4 = (ri // 4) == (ci // 4)
        for g0 in range(0, len(pairs), group):
            grp = pairs[g0:g0 + group]
            lvl = [[lax.dot_general(lq_s[3 * hd + n, chunk_rows(c), :],
                                    lk_s[3 * hd + n, chunk_rows(c), :], nt,
                                    preferred_element_type=F32) for n in range(3)]
                   for c, hd in grp]
            us = state_increments(grp)
            finish_group(grp, [(g1 + jnp.where(same16, g2, 0.0)
                                + jnp.where(same4, g3, 0.0)).astype(BF16) for g1, g2, g3 in lvl], us)


    sts = [state_s[hd] for hd in range(GLA_HEADS)]
    for c in range(nc):
        for hd in range(GLA_HEADS):
            sb_s[c * GLA_HEADS + hd] = sts[hd].astype(BF16)
        sts = [sts[hd] * decay[c][:, ksl[hd]] + u_s[c * GLA_HEADS + hd]
               for hd in range(GLA_HEADS)]
    for hd in range(GLA_HEADS):
        state_s[hd] = sts[hd]

    for g0 in range(0, len(pairs), group):
        grp = pairs[g0:g0 + group]
        inters = [lax.dot_general(qe_s[c * CHUNK:(c + 1) * CHUNK, ksl[hd]],
                                  sb_s[c * GLA_HEADS + hd], nt, preferred_element_type=F32)
                  for c, hd in grp]
        for inter, (c, hd) in zip(inters, grp):
            o_s[chunk_rows(c), vsl[hd]] = o_s[chunk_rows(c), vsl[hd]] + inter

    for hd in range(GLA_HEADS):
        vs = slice(hd * GLA_DV, (hd + 1) * GLA_DV)
        o = o_s[:, vs]
        o = o * lax.rsqrt(jnp.mean(o * o, axis=-1, keepdims=True) + EPS)
        y_s[:, POOL_WIDTH + CONV_WIDTH + hd * GLA_DV:POOL_WIDTH + CONV_WIDTH + (hd + 1) * GLA_DV] = (
            o * gate_s[:, vs]).astype(BF16)

    mix = jnp.dot(y_s[...], wout_s[...], preferred_element_type=F32)
    o_ref[...] = h_ref[...] + _rms(mix, post_ref[...])

    @pl.when(is_meta)
    def _():
        pmeta[...] = pbuf[0:POOL_HIST, :]
        zmeta[...] = zbuf[0:CONV_HIST, :]
        smeta[...] = state_s[...]


def _mixer(h, tiles_per_seq, pre, post, win, wout, poolw, small_rows, w2, gnorm, *, layer):
    ts = TILE
    return pl.pallas_call(
        functools.partial(_mixer_kernel, tiles_per_seq=tiles_per_seq),
        grid=(h.shape[0] // ts,),
        in_specs=[
            pl.BlockSpec((ts, D_MODEL), lambda i: (i, 0)),
            _resident((1, D_MODEL), layer),
            _resident((1, D_MODEL), layer),
            _resident((D_MODEL, D_PROJ), layer),
            _resident((D_MODEL, D_MODEL), layer),
            _resident((POOL_WIDTH, POOL_WIDTH), layer),
            _resident((SMALL_ROWS, GLA_KW), layer),
            _resident((GL_PAD, GLA_KW), layer),
            _resident((1, GLA_WIDTH), layer),
        ],
        out_specs=pl.BlockSpec((ts, D_MODEL), lambda i: (i, 0)),
        out_shape=jax.ShapeDtypeStruct(h.shape, F32),
        scratch_shapes=[
            pltpu.VMEM((ts, D_MODEL), BF16),
            pltpu.VMEM((POOL_HIST + ts, POOL_WIDTH), F32),
            pltpu.VMEM((CONV_HIST + ts, CONV_WIDTH), F32),
            pltpu.VMEM((ts, GLA_KW), F32),
            pltpu.VMEM((ts, GLA_KW), F32),
            pltpu.VMEM((ts, GLA_KW), F32),
            pltpu.VMEM((3 * GLA_HEADS, ts, 128), F32),
            pltpu.VMEM((ts, GLA_KW), BF16),
            pltpu.VMEM((ts, GLA_KW), BF16),
            pltpu.VMEM((3 * GLA_HEADS, ts, 4 * GLA_DK), BF16),
            pltpu.VMEM((3 * GLA_HEADS, ts, 4 * GLA_DK), BF16),
            pltpu.VMEM((ts, GLA_WIDTH), BF16),
            pltpu.VMEM((ts, GLA_WIDTH), F32),
            pltpu.VMEM((ts, GLA_WIDTH), F32),
            pltpu.VMEM((ts // CHUNK * GLA_HEADS, GLA_DV, GLA_DK), F32),
            pltpu.VMEM((ts // CHUNK * GLA_HEADS, GLA_DV, GLA_DK), BF16),
            pltpu.VMEM((ts, D_MODEL), BF16),
            pltpu.VMEM((GLA_HEADS, GLA_DV, GLA_DK), F32),
            pltpu.VMEM((POOL_HIST, POOL_WIDTH), F32),
            pltpu.VMEM((CONV_HIST, CONV_WIDTH), F32),
            pltpu.VMEM((GLA_HEADS, GLA_DV, GLA_DK), F32),
            pltpu.SMEM((1,), F32),
            pltpu.VMEM((D_MODEL, D_MODEL), BF16),
            pltpu.VMEM((D_MODEL, D_PROJ_PACKED), BF16),
        ],
        compiler_params=pltpu.CompilerParams(
            dimension_semantics=("arbitrary",), vmem_limit_bytes=VMEM_LIMIT),
        name="mixer",
    )(h, pre, post, win, wout, poolw, small_rows, w2, gnorm)


def _pad_to(v, axis, size):
    pads = [(0, 0)] * v.ndim
    pads[axis] = (0, size - v.shape[axis])
    return jnp.pad(v, pads)


def _block_diag(w):
    out = jnp.zeros((w.shape[0], POOL_WIDTH, POOL_WIDTH), w.dtype)
    for g in range(len(POOL_WINDOWS)):
        sl = slice(g * POOL_GROUP, (g + 1) * POOL_GROUP)
        out = out.at[:, sl, sl].set(w[:, g])
    return out


def kernel(x, meta, ffn1_pre, ffn1_post, ffn1_wg, ffn1_wu, ffn1_wd, mix_pre, mix_post, w_in,
           pool_w, pool_scale, conv_w, gla_w2, gla_b2, gla_norm, w_out,
           ffn2_pre, ffn2_post, ffn2_wg, ffn2_wu, ffn2_wd):
    n_batch, seq, d = x.shape
    depth = w_in.shape[0]
    assert d == D_MODEL and seq % TILE == 0 and meta.shape[0] == N_META
    meta_tile = jnp.pad(meta.astype(F32), ((META_PAD, 0), (0, 0)))
    h = x.reshape(n_batch * seq, d)
    rows = lambda v: v.reshape(depth, 1, -1).astype(F32)
    assert (ROW_POOL_SCALE, ROW_GATE_BIAS, ROW_CONV) == (0, 1, 2)
    small_rows = _pad_to(
        jnp.concatenate([rows(pool_scale), rows(gla_b2), conv_w.astype(F32)], axis=1), 1, SMALL_ROWS)
    mixer_small = (_block_diag(pool_w).astype(BF16), small_rows,
                   _pad_to(gla_w2, 1, GL_PAD).astype(BF16), rows(gla_norm))
    mix_pre, mix_post = rows(mix_pre), rows(mix_post)
    ffn1 = (rows(ffn1_pre), rows(ffn1_post), ffn1_wg, ffn1_wu, ffn1_wd)
    ffn2 = (rows(ffn2_pre), rows(ffn2_post), ffn2_wg, ffn2_wu, ffn2_wd)
    for l in range(depth):
        h = _ffn(h, meta_tile, *ffn1, mode="first" if l == 0 else "mid", layer=l)
        h = _mixer(h, seq // TILE, mix_pre, mix_post, w_in, w_out, *mixer_small, layer=l)
        h = _ffn(h, meta_tile, *ffn2, mode="last" if l == depth - 1 else "mid", layer=l)
    return h.reshape(n_batch, seq, d)
```

```python
import functools

import jax
import jax.numpy as jnp
from jax import lax
from jax.experimental import pallas as pl
from jax.experimental.pallas import tpu as pltpu

F32 = jnp.float32
BF16 = jnp.bfloat16

D_MODEL = 1024
D_FF = 2816
N_META = 16
EPS = 1e-6
FFN_RESID = 0.5

POOL_WIDTH = 256
POOL_WINDOWS = (2, 4, 8, 16)
POOL_GROUP = 64
CONV_WIDTH = 256
CONV_K = 3
GLA_HEADS = 4
GLA_DK = 64
GLA_DV = 128
GLA_KW = GLA_HEADS * GLA_DK
GLA_WIDTH = GLA_HEADS * GLA_DV
GLA_GATE_RANK = 16
GLA_TAU = 16.0
LOG2_E = 1.4426950408889634
SINGLE_REF_MAX_LOG2 = 40.0
CHUNK = 64


OFF_POOL, OFF_CB, OFF_CC, OFF_CU, OFF_Q, OFF_K = 0, 256, 512, 768, 1024, 1280
OFF_V, OFF_OG, OFF_GL = 1536, 2048, 2560
GL_PAD = 128
D_PROJ_PACKED = OFF_GL + GL_PAD
D_PROJ = OFF_OG + GLA_GATE_RANK + GLA_WIDTH

V7X_VMEM_BYTES = 64 * 1024 * 1024
COMPILER_SCRATCH_BYTES = 8 * 1024 * 1024
VMEM_LIMIT = V7X_VMEM_BYTES - COMPILER_SCRATCH_BYTES

TILE = 512
FF_CHUNK = 256
META_PAD = TILE - N_META
META_ROWS = 64
ROW_POOL_SCALE, ROW_GATE_BIAS, ROW_CONV = 0, 1, 2
SMALL_ROWS = 16
POOL_HIST = 16
CONV_HIST = 16


def _rms(x, g):
    return x * lax.rsqrt(jnp.mean(x * x, axis=-1, keepdims=True) + EPS) * g


def _load_ffn_weights(layer, wg_hbm, wu_hbm, wd_hbm, wg_ref, wu_ref, wd_ref, stage_cols,
                      stage_rows, sems, on_chunk):
    n_chunks = D_FF // FF_CHUNK

    def copies(j, slot):
        off = pl.multiple_of(j * FF_CHUNK, FF_CHUNK)
        return (
            pltpu.make_async_copy(wg_hbm.at[layer, :, pl.ds(off, FF_CHUNK)],
                                  stage_cols.at[0, slot], sems.at[0, slot]),
            pltpu.make_async_copy(wu_hbm.at[layer, :, pl.ds(off, FF_CHUNK)],
                                  stage_cols.at[1, slot], sems.at[1, slot]),
            pltpu.make_async_copy(wd_hbm.at[layer, pl.ds(off, FF_CHUNK), :],
                                  stage_rows.at[slot], sems.at[2, slot]),
        )

    for cp in copies(0, 0):
        cp.start()

    def body(j, carry):
        slot = lax.rem(j, 2)

        @pl.when(j + 1 < n_chunks)
        def _():
            for cp in copies(j + 1, 1 - slot):
                cp.start()

        for cp in copies(j, slot):
            cp.wait()
        off = pl.multiple_of(j * FF_CHUNK, FF_CHUNK)
        wg_ref[:, pl.ds(off, FF_CHUNK)] = (0.5 * stage_cols[0, slot]).astype(BF16)
        wu_ref[:, pl.ds(off, FF_CHUNK)] = stage_cols[1, slot].astype(BF16)
        wd_ref[pl.ds(off, FF_CHUNK), :] = stage_rows[slot].astype(BF16)
        on_chunk(off)
        return carry

    lax.fori_loop(0, n_chunks, body, 0)


def _ffn_kernel(h_ref, meta_ref, pre_ref, post_ref, wg_hbm, wu_hbm, wd_hbm, o_ref,
                a_ref, z_ref, wg_ref, wu_ref, wd_ref, stage_cols, stage_rows, sems, *,
                step0, layer):
    step = pl.program_id(0)

    def hidden_chunk(rows, cols):
        a = a_ref[rows, :]
        half_g = jnp.dot(a, wg_ref[:, cols], preferred_element_type=F32)
        u = jnp.dot(a, wu_ref[:, cols], preferred_element_type=F32)
        z_ref[rows, cols] = (half_g * (1.0 + jnp.tanh(half_g)) * u).astype(BF16)

    def finish(rows, src_ref):
        f = jnp.dot(z_ref[rows, :], wd_ref[...], preferred_element_type=F32)
        o_ref[rows, :] = src_ref[rows, :] + _rms(f, FFN_RESID * post_ref[...])

    @pl.when(step == 0)
    def _():
        rows = slice(0, TILE) if step0 == "tokens" else slice(TILE - META_ROWS, TILE)
        src_ref = meta_ref if step0 == "meta_input" else h_ref
        a_ref[rows, :] = _rms(src_ref[rows, :], pre_ref[...]).astype(BF16)
        _load_ffn_weights(layer, wg_hbm, wu_hbm, wd_hbm, wg_ref, wu_ref, wd_ref, stage_cols,
                          stage_rows, sems,
                          lambda off: hidden_chunk(rows, pl.ds(off, FF_CHUNK)))
        if step0 != "tokens":
            o_ref[0:TILE - META_ROWS, :] = jnp.zeros((TILE - META_ROWS, D_MODEL), F32)
        finish(rows, src_ref)

    @pl.when(step > 0)
    def _():
        a_ref[...] = _rms(h_ref[...], pre_ref[...]).astype(BF16)
        for j in range(D_FF // FF_CHUNK):
            hidden_chunk(slice(0, TILE), slice(j * FF_CHUNK, (j + 1) * FF_CHUNK))
        finish(slice(0, TILE), h_ref)


def _resident(shape, layer=None):
    if layer is None:
        return pl.BlockSpec(shape, lambda *_: (0,) * len(shape), pipeline_mode=pl.Buffered(1))
    return pl.BlockSpec((None,) + tuple(shape), lambda *_: (layer,) + (0,) * len(shape),
                        pipeline_mode=pl.Buffered(1))


def _ffn(src, meta_tile, gains, wg, wu, wd, *, gain_rows, mode, layer):
    tile = TILE
    n_tiles = src.shape[0] // tile
    assert src.shape[0] % tile == 0
    if mode == "first":
        grid, src_map = n_tiles + 1, lambda i: (jnp.maximum(i - 1, 0), 0)
    elif mode == "mid":
        grid, src_map = n_tiles, lambda i: (i, 0)
    else:
        grid, src_map = n_tiles - 1, lambda i: (i + 1, 0)
    out_rows = grid * tile
    return pl.pallas_call(
        functools.partial(
            _ffn_kernel, layer=layer,
            step0={"first": "meta_input", "mid": "meta_block", "last": "tokens"}[mode]),
        grid=(grid,),
        in_specs=[
            pl.BlockSpec((tile, D_MODEL), src_map),
            _resident((TILE, D_MODEL)),
            _resident((1, D_MODEL), gain_rows[0]),
            _resident((1, D_MODEL), gain_rows[1]),
            pl.BlockSpec(memory_space=pl.ANY),
            pl.BlockSpec(memory_space=pl.ANY),
            pl.BlockSpec(memory_space=pl.ANY),
        ],
        out_specs=pl.BlockSpec((tile, D_MODEL), lambda i: (i, 0)),
        out_shape=jax.ShapeDtypeStruct((out_rows, D_MODEL), F32),
        scratch_shapes=[
            pltpu.VMEM((tile, D_MODEL), BF16),
            pltpu.VMEM((tile, D_FF), BF16),
            pltpu.VMEM((D_MODEL, D_FF), BF16),
            pltpu.VMEM((D_MODEL, D_FF), BF16),
            pltpu.VMEM((D_FF, D_MODEL), BF16),
            pltpu.VMEM((2, 2, D_MODEL, FF_CHUNK), F32),
            pltpu.VMEM((2, FF_CHUNK, D_MODEL), F32),
            pltpu.SemaphoreType.DMA((3, 2)),
        ],
        compiler_params=pltpu.CompilerParams(
            dimension_semantics=("arbitrary",), vmem_limit_bytes=VMEM_LIMIT),
        name="ffn_" + mode,
    )(src, meta_tile, gains, gains, wg, wu, wd)


def _mixer_kernel(h_ref, pre_ref, post_ref, win_ref, wout_ref, poolw_ref, rows_ref,
                  w2_ref, gnorm_ref, o_ref,
                  a_s, pbuf, zbuf, b_s, q_s, k_s, rep_s, qe_s, kd_s, lq_s, lk_s, v_s, gate_s,
                  o_s, u_s, sb_s, y_s, state_s, pmeta, zmeta, smeta, bmin_s, wout_s, win_s, *,
                  tiles_per_seq):
    step = pl.program_id(0)
    ts = TILE
    is_meta = step == 0

    @pl.when(is_meta)
    def _():
        wout_s[...] = wout_ref[...].astype(BF16)
        win_s[:, 0:OFF_OG] = win_ref[:, 0:OFF_OG].astype(BF16)
        win_s[:, OFF_OG:OFF_GL] = win_ref[:, OFF_OG + GLA_GATE_RANK:D_PROJ].astype(BF16)
        low_rank = win_ref[:, OFF_OG:OFF_OG + GL_PAD]
        keep = lax.broadcasted_iota(jnp.int32, (1, GL_PAD), 1) < GLA_GATE_RANK
        win_s[:, OFF_GL:D_PROJ_PACKED] = jnp.where(keep, low_rank, 0.0).astype(BF16)
        pbuf[0:POOL_HIST, :] = jnp.zeros((POOL_HIST, POOL_WIDTH), F32)
        zbuf[0:CONV_HIST, :] = jnp.zeros((CONV_HIST, CONV_WIDTH), F32)
        state_s[...] = jnp.zeros_like(state_s)

    @pl.when(jnp.logical_and(step > 0, lax.rem(step - 1, tiles_per_seq) == 0))
    def _():
        pbuf[0:POOL_HIST, :] = pmeta[...]
        zbuf[0:CONV_HIST, :] = zmeta[...]
        state_s[...] = smeta[...]

    nc = ts // CHUNK
    ri = lax.broadcasted_iota(jnp.int32, (CHUNK, CHUNK), 0)
    ci = lax.broadcasted_iota(jnp.int32, (CHUNK, CHUNK), 1)

    def proj(off, width):
        return jnp.dot(a_s[...], win_s[:, off:off + width], preferred_element_type=F32)

    def pool_mixer():
        x1 = pbuf[...]
        s2 = x1 + pltpu.roll(x1, 1, 0)
        s4 = s2 + pltpu.roll(s2, 2, 0)
        s8 = s4 + pltpu.roll(s4, 4, 0)
        s16 = s8 + pltpu.roll(s8, 8, 0)
        lane = lax.broadcasted_iota(jnp.int32, (ts, POOL_WIDTH), 1)
        row = lax.broadcasted_iota(jnp.int32, (ts, POOL_WIDTH), 0)
        win = jnp.where(lane < 64, 2, jnp.where(lane < 128, 4, jnp.where(lane < 192, 8, 16)))
        wsum = jnp.where(lane < 64, s2[POOL_HIST:], jnp.where(
            lane < 128, s4[POOL_HIST:], jnp.where(lane < 192, s8[POOL_HIST:], s16[POOL_HIST:])))
        pos1 = jnp.where(is_meta, row - (META_PAD - 1), max(POOL_WINDOWS))
        cnt = jnp.clip(pos1, 1, win).astype(F32)
        m = (wsum / cnt - x1[POOL_HIST:]).astype(BF16)
        y_pool = (jnp.dot(m, poolw_ref[...], preferred_element_type=F32)
                  * rows_ref[ROW_POOL_SCALE:ROW_POOL_SCALE + 1, :])
        y_s[:, 0:POOL_WIDTH] = y_pool.astype(BF16)
        pbuf[0:POOL_HIST, :] = x1[ts:ts + POOL_HIST]

    def conv_mixer():
        zz = zbuf[...]
        z1 = pltpu.roll(zz, 1, 0)
        z2 = pltpu.roll(zz, 2, 0)
        cw = rows_ref[ROW_CONV:ROW_CONV + CONV_K, :]
        yc = (cw[0:1, :] * z2[CONV_HIST:] + cw[1:2, :] * z1[CONV_HIST:]
              + cw[2:3, :] * zz[CONV_HIST:])
        y_s[:, POOL_WIDTH:POOL_WIDTH + CONV_WIDTH] = (proj(OFF_CB, CONV_WIDTH) * yc).astype(BF16)
        zbuf[0:CONV_HIST, :] = zz[ts:ts + CONV_HIST]

    def project_and_mix_local():
        a_s[...] = _rms(h_ref[...], pre_ref[...]).astype(BF16)
        g_low = proj(OFF_GL, GL_PAD).astype(BF16)
        gx = (jnp.dot(g_low, w2_ref[...], preferred_element_type=F32)
              + rows_ref[ROW_GATE_BIAS:ROW_GATE_BIAS + 1, :])
        log_sig = jnp.minimum(gx, 0.0) - jnp.log(1.0 + jnp.exp(-jnp.abs(gx)))
        la = log_sig * (LOG2_E / GLA_TAU)
        la_hi = la.astype(BF16)
        la_lo = (la - la_hi.astype(F32)).astype(BF16)
        la_cat = jnp.concatenate([la_hi, la_lo], axis=1)
        pbuf[POOL_HIST:, :] = proj(OFF_POOL, POOL_WIDTH)
        zbuf[CONV_HIST:, :] = proj(OFF_CC, CONV_WIDTH) * proj(OFF_CU, CONV_WIDTH)
        tril = jnp.where(ri >= ci, 1.0, 0.0).astype(BF16)
        for c in range(nc):
            rows = slice(c * CHUNK, (c + 1) * CHUNK)
            bb = jnp.dot(tril, la_cat[rows, :], preferred_element_type=F32)
            b_s[rows, :] = bb[:, :GLA_KW] + bb[:, GLA_KW:]
        q_s[...] = proj(OFF_Q, GLA_KW) * (GLA_DK ** -0.5)
        k_s[...] = proj(OFF_K, GLA_KW)
        pool_mixer()
        conv_mixer()
        v_s[...] = proj(OFF_V, GLA_WIDTH).astype(BF16)
        og = proj(OFF_OG, GLA_WIDTH)
        gate_s[...] = og * jax.nn.sigmoid(og) * gnorm_ref[...]
        bmin_s[0] = jnp.min(b_s[...])

    project_and_mix_local()
    b, q, k = b_s[...], q_s[...], k_s[...]
    b3 = b.reshape(nc, CHUNK, GLA_KW)
    b_last = b3[:, CHUNK - 1:CHUNK, :]
    qe_s[...] = (q.reshape(b3.shape) * jnp.exp2(b3)).reshape(ts, GLA_KW).astype(BF16)
    kd_s[...] = (k.reshape(b3.shape) * jnp.exp2(b_last - b3)).reshape(ts, GLA_KW).astype(BF16)
    decay = jnp.exp2(b_last)

    nt = (((1,), (1,)), ((), ()))
    tn = (((0,), (0,)), ((), ()))
    ksl = [slice(hd * GLA_DK, (hd + 1) * GLA_DK) for hd in range(GLA_HEADS)]
    vsl = [slice(hd * GLA_DV, (hd + 1) * GLA_DV) for hd in range(GLA_HEADS)]
    pairs = [(c, hd) for c in range(nc) for hd in range(GLA_HEADS)]
    group = 2 * GLA_HEADS
    chunk_rows = lambda c: slice(c * CHUNK, (c + 1) * CHUNK)

    single_ref_ok = bmin_s[0] > -SINGLE_REF_MAX_LOG2

    def state_increments(grp):
        return [lax.dot_general(v_s[chunk_rows(c), vsl[hd]], kd_s[chunk_rows(c), ksl[hd]], tn,
                                preferred_element_type=F32) for c, hd in grp]

    def finish_group(grp, ps, us):
        pvs = [jnp.dot(p, v_s[chunk_rows(c), vsl[hd]], preferred_element_type=F32)
               for p, (c, hd) in zip(ps, grp)]
        for pv, u, (c, hd) in zip(pvs, us, grp):
            o_s[chunk_rows(c), vsl[hd]] = pv
            u_s[c * GLA_HEADS + hd] = u

    @pl.when(single_ref_ok)
    def _():
        lk_s[0] = (k.reshape(b3.shape) * jnp.exp2(-b3)).reshape(ts, GLA_KW).astype(BF16)
        causal = ri >= ci
        for g0 in range(0, len(pairs), group):
            grp = pairs[g0:g0 + group]
            scs = [lax.dot_general(qe_s[chunk_rows(c), ksl[hd]], lk_s[0, chunk_rows(c), ksl[hd]],
                                   nt, preferred_element_type=F32) for c, hd in grp]
            us = state_increments(grp)
            finish_group(grp, [jnp.where(causal, sc, 0.0).astype(BF16) for sc in scs], us)

    lane_hi = lax.broadcasted_iota(jnp.int32, (1, 1, 128), 2) >= GLA_DK
    t64 = lax.broadcasted_iota(jnp.int32, (1, CHUNK, 1), 1)
    blk16, blk4, pos4 = t64 // 16, (t64 // 4) % 4, t64 % 4
    in_hi4 = lax.broadcasted_iota(jnp.int32, (1, 8, 1), 1) >= 4
    neg = -1e30

    def replicate_heads():
        for n, src in enumerate((b_s, q_s, k_s)):
            for c2 in range(2):
                x = src[:, 128 * c2:128 * c2 + 128]
                x_sw = pltpu.roll(x, GLA_DK, 1)
                rep_s[GLA_HEADS * n + 2 * c2] = jnp.where(lane_hi[0], x_sw, x)
                rep_s[GLA_HEADS * n + 2 * c2 + 1] = jnp.where(lane_hi[0], x, x_sw)

    def build_level_operands(hd):
        rb, rq, rk = (rep_s[GLA_HEADS * n + hd] for n in range(3))
        rb64, rq64, rk64 = (x.reshape(nc, CHUNK, 128) for x in (rb, rq, rk))
        rb16 = rb.reshape(ts // 16, 16, 128)
        rb8 = rb.reshape(ts // 8, 8, 128)
        to64 = lambda x: x.reshape(nc, CHUNK, 128)
        end4 = jnp.where(in_hi4, rb8[:, 7:8, :], rb8[:, 3:4, :])
        k1 = rk64 * to64(jnp.exp2(rb16[:, 15:16, :] - rb16))
        k2 = rk64 * to64(jnp.exp2(end4 - rb8))
        nxt = jnp.where(lane_hi, pltpu.roll(rb8, 7, 1), rb8)
        for p in range(2):
            col = slice(128 * p, 128 * p + 128)
            j = jnp.where(lane_hi, 2 * p + 1, 2 * p)
            ref1 = jnp.where(lane_hi, rb64[:, 32 * p + 31:32 * p + 32, :],
                             rb64[:, 32 * p + 15:32 * p + 16, :])
            ref2 = jnp.where(lane_hi, rb16[:, 8 * p + 7:8 * p + 8, :], rb16[:, 8 * p + 3:8 * p + 4, :])
            ref3 = jnp.where(in_hi4, nxt[:, 4 + 2 * p:5 + 2 * p, :], nxt[:, 2 * p:2 * p + 1, :])
            q1 = rq64 * jnp.exp2(jnp.where(blk16 > j, rb64 - ref1, neg))
            q2 = rq64 * jnp.exp2(jnp.where(blk4 > j, to64(rb16 - ref2), neg))
            q3 = rq64 * jnp.exp2(jnp.where(pos4 >= j, to64(rb8 - ref3), neg))
            for n, (ql, kl) in enumerate(((q1, jnp.where(blk16 == j, k1, 0.0)),
                                          (q2, jnp.where(blk4 == j, k2, 0.0)),
                                          (q3, jnp.where(pos4 == j, rk64, 0.0)))):
                lq_s[3 * hd + n, :, col] = ql.reshape(ts, 128).astype(BF16)
                lk_s[3 * hd + n, :, col] = kl.reshape(ts, 128).astype(BF16)

    @pl.when(jnp.logical_not(single_ref_ok))
    def _():
        replicate_heads()
        for hd in range(GLA_HEADS):
            build_level_operands(hd)
        same16 = (ri // 16) == (ci // 16)
        same4 = (ri // 4) == (ci // 4)
        for g0 in range(0, len(pairs), group):
            grp = pairs[g0:g0 + group]
            lvl = [[lax.dot_general(lq_s[3 * hd + n, chunk_rows(c), :],
                                    lk_s[3 * hd + n, chunk_rows(c), :], nt,
                                    preferred_element_type=F32) for n in range(3)]
                   for c, hd in grp]
            us = state_increments(grp)
            finish_group(grp, [(g1 + jnp.where(same16, g2, 0.0)
                                + jnp.where(same4, g3, 0.0)).astype(BF16) for g1, g2, g3 in lvl], us)


    sts = [state_s[hd] for hd in range(GLA_HEADS)]
    for c in range(nc):
        for hd in range(GLA_HEADS):
            sb_s[c * GLA_HEADS + hd] = sts[hd].astype(BF16)
        sts = [sts[hd] * decay[c][:, ksl[hd]] + u_s[c * GLA_HEADS + hd]
               for hd in range(GLA_HEADS)]
    for hd in range(GLA_HEADS):
        state_s[hd] = sts[hd]

    for g0 in range(0, len(pairs), group):
        grp = pairs[g0:g0 + group]
        inters = [lax.dot_general(qe_s[c * CHUNK:(c + 1) * CHUNK, ksl[hd]],
                                  sb_s[c * GLA_HEADS + hd], nt, preferred_element_type=F32)
                  for c, hd in grp]
        for inter, (c, hd) in zip(inters, grp):
            o_s[chunk_rows(c), vsl[hd]] = o_s[chunk_rows(c), vsl[hd]] + inter

    for hd in range(GLA_HEADS):
        vs = slice(hd * GLA_DV, (hd + 1) * GLA_DV)
        o = o_s[:, vs]
        o = o * lax.rsqrt(jnp.mean(o * o, axis=-1, keepdims=True) + EPS)
        y_s[:, POOL_WIDTH + CONV_WIDTH + hd * GLA_DV:POOL_WIDTH + CONV_WIDTH + (hd + 1) * GLA_DV] = (
            o * gate_s[:, vs]).astype(BF16)

    mix = jnp.dot(y_s[...], wout_s[...], preferred_element_type=F32)
    o_ref[...] = h_ref[...] + _rms(mix, post_ref[...])

    @pl.when(is_meta)
    def _():
        pmeta[...] = pbuf[0:POOL_HIST, :]
        zmeta[...] = zbuf[0:CONV_HIST, :]
        smeta[...] = state_s[...]


def _mixer(h, tiles_per_seq, gains, win, wout, poolw, small_rows, w2, gnorm, *, gain_rows,
           layer):
    ts = TILE
    return pl.pallas_call(
        functools.partial(_mixer_kernel, tiles_per_seq=tiles_per_seq),
        grid=(h.shape[0] // ts,),
        in_specs=[
            pl.BlockSpec((ts, D_MODEL), lambda i: (i, 0)),
            _resident((1, D_MODEL), gain_rows[0]),
            _resident((1, D_MODEL), gain_rows[1]),
            _resident((D_MODEL, D_PROJ), layer),
            _resident((D_MODEL, D_MODEL), layer),
            _resident((POOL_WIDTH, POOL_WIDTH), layer),
            _resident((SMALL_ROWS, GLA_KW), layer),
            _resident((GL_PAD, GLA_KW), layer),
            _resident((1, GLA_WIDTH), layer),
        ],
        out_specs=pl.BlockSpec((ts, D_MODEL), lambda i: (i, 0)),
        out_shape=jax.ShapeDtypeStruct(h.shape, F32),
        scratch_shapes=[
            pltpu.VMEM((ts, D_MODEL), BF16),
            pltpu.VMEM((POOL_HIST + ts, POOL_WIDTH), F32),
            pltpu.VMEM((CONV_HIST + ts, CONV_WIDTH), F32),
            pltpu.VMEM((ts, GLA_KW), F32),
            pltpu.VMEM((ts, GLA_KW), F32),
            pltpu.VMEM((ts, GLA_KW), F32),
            pltpu.VMEM((3 * GLA_HEADS, ts, 128), F32),
            pltpu.VMEM((ts, GLA_KW), BF16),
            pltpu.VMEM((ts, GLA_KW), BF16),
            pltpu.VMEM((3 * GLA_HEADS, ts, 4 * GLA_DK), BF16),
            pltpu.VMEM((3 * GLA_HEADS, ts, 4 * GLA_DK), BF16),
            pltpu.VMEM((ts, GLA_WIDTH), BF16),
            pltpu.VMEM((ts, GLA_WIDTH), F32),
            pltpu.VMEM((ts, GLA_WIDTH), F32),
            pltpu.VMEM((ts // CHUNK * GLA_HEADS, GLA_DV, GLA_DK), F32),
            pltpu.VMEM((ts // CHUNK * GLA_HEADS, GLA_DV, GLA_DK), BF16),
            pltpu.VMEM((ts, D_MODEL), BF16),
            pltpu.VMEM((GLA_HEADS, GLA_DV, GLA_DK), F32),
            pltpu.VMEM((POOL_HIST, POOL_WIDTH), F32),
            pltpu.VMEM((CONV_HIST, CONV_WIDTH), F32),
            pltpu.VMEM((GLA_HEADS, GLA_DV, GLA_DK), F32),
            pltpu.SMEM((1,), F32),
            pltpu.VMEM((D_MODEL, D_MODEL), BF16),
            pltpu.VMEM((D_MODEL, D_PROJ_PACKED), BF16),
        ],
        compiler_params=pltpu.CompilerParams(
            dimension_semantics=("arbitrary",), vmem_limit_bytes=VMEM_LIMIT),
        name="mixer",
    )(h, gains, gains, win, wout, poolw, small_rows, w2, gnorm)


def _pad_to(v, axis, size):
    pads = [(0, 0)] * v.ndim
    pads[axis] = (0, size - v.shape[axis])
    return jnp.pad(v, pads)


def _block_diag(w):
    out = jnp.zeros((w.shape[0], POOL_WIDTH, POOL_WIDTH), w.dtype)
    for g in range(len(POOL_WINDOWS)):
        sl = slice(g * POOL_GROUP, (g + 1) * POOL_GROUP)
        out = out.at[:, sl, sl].set(w[:, g])
    return out


def kernel(x, meta, ffn1_pre, ffn1_post, ffn1_wg, ffn1_wu, ffn1_wd, mix_pre, mix_post, w_in,
           pool_w, pool_scale, conv_w, gla_w2, gla_b2, gla_norm, w_out,
           ffn2_pre, ffn2_post, ffn2_wg, ffn2_wu, ffn2_wd):
    n_batch, seq, d = x.shape
    depth = w_in.shape[0]
    assert d == D_MODEL and seq % TILE == 0 and meta.shape[0] == N_META
    meta_tile = jnp.pad(meta.astype(F32), ((META_PAD, 0), (0, 0)))
    h = x.reshape(n_batch * seq, d)
    rows = lambda v: v.reshape(depth, 1, -1).astype(F32)
    assert (ROW_POOL_SCALE, ROW_GATE_BIAS, ROW_CONV) == (0, 1, 2)
    small_rows = _pad_to(
        jnp.concatenate([rows(pool_scale), rows(gla_b2), conv_w.astype(F32)], axis=1), 1, SMALL_ROWS)
    mixer_small = (_block_diag(pool_w).astype(BF16), small_rows,
                   _pad_to(gla_w2, 1, GL_PAD).astype(BF16), rows(gla_norm))
    gains = jnp.stack([ffn1_pre, ffn1_post, mix_pre, mix_post, ffn2_pre, ffn2_post]).astype(F32)
    gains = gains.reshape(6 * depth, 1, d)
    for l in range(depth):
        h = _ffn(h, meta_tile, gains, ffn1_wg, ffn1_wu, ffn1_wd, gain_rows=(l, depth + l),
                 mode="first" if l == 0 else "mid", layer=l)
        h = _mixer(h, seq // TILE, gains, w_in, w_out, *mixer_small,
                   gain_rows=(2 * depth + l, 3 * depth + l), layer=l)
        h = _ffn(h, meta_tile, gains, ffn2_wg, ffn2_wu, ffn2_wd,
                 gain_rows=(4 * depth + l, 5 * depth + l),
                 mode="last" if l == depth - 1 else "mid", layer=l)
    return h.reshape(n_batch, seq, d)
```

```python
import functools

import jax
import jax.numpy as jnp
from jax import lax
from jax.experimental import pallas as pl
from jax.experimental.pallas import tpu as pltpu

F32 = jnp.float32
BF16 = jnp.bfloat16

D_MODEL = 1024
D_FF = 2816
N_META = 16
EPS = 1e-6
FFN_RESID = 0.5

POOL_WIDTH = 256
POOL_WINDOWS = (2, 4, 8, 16)
POOL_GROUP = 64
CONV_WIDTH = 256
CONV_K = 3
GLA_HEADS = 4
GLA_DK = 64
GLA_DV = 128
GLA_KW = GLA_HEADS * GLA_DK
GLA_WIDTH = GLA_HEADS * GLA_DV
GLA_GATE_RANK = 16
GLA_TAU = 16.0
LOG2_E = 1.4426950408889634
SINGLE_REF_MAX_LOG2 = 40.0
CHUNK = 64


OFF_POOL, OFF_CB, OFF_CC, OFF_CU, OFF_Q, OFF_K = 0, 256, 512, 768, 1024, 1280
OFF_V, OFF_OG, OFF_GL = 1536, 2048, 2560
GL_PAD = 128
D_PROJ_PACKED = OFF_GL + GL_PAD
D_PROJ = OFF_OG + GLA_GATE_RANK + GLA_WIDTH

V7X_VMEM_BYTES = 64 * 1024 * 1024
COMPILER_SCRATCH_BYTES = 8 * 1024 * 1024
VMEM_LIMIT = V7X_VMEM_BYTES - COMPILER_SCRATCH_BYTES

TILE = 512
FF_CHUNK = 256
META_PAD = TILE - N_META
META_ROWS = 64
ROW_POOL_SCALE, ROW_GATE_BIAS, ROW_CONV = 0, 1, 2
SMALL_ROWS = 16
POOL_HIST = 16
CONV_HIST = 16


def _rms(x, g):
    return x * lax.rsqrt(jnp.mean(x * x, axis=-1, keepdims=True) + EPS) * g


def _load_ffn_weights(layer, wg_hbm, wu_hbm, wd_hbm, wg_ref, wu_ref, wd_ref, stage_cols,
                      stage_rows, sems, on_chunk):
    n_chunks = D_FF // FF_CHUNK

    def copies(j, slot):
        off = pl.multiple_of(j * FF_CHUNK, FF_CHUNK)
        return (
            pltpu.make_async_copy(wg_hbm.at[layer, :, pl.ds(off, FF_CHUNK)],
                                  stage_cols.at[0, slot], sems.at[0, slot]),
            pltpu.make_async_copy(wu_hbm.at[layer, :, pl.ds(off, FF_CHUNK)],
                                  stage_cols.at[1, slot], sems.at[1, slot]),
            pltpu.make_async_copy(wd_hbm.at[layer, pl.ds(off, FF_CHUNK), :],
                                  stage_rows.at[slot], sems.at[2, slot]),
        )

    for cp in copies(0, 0):
        cp.start()

    def body(j, carry):
        slot = lax.rem(j, 2)

        @pl.when(j + 1 < n_chunks)
        def _():
            for cp in copies(j + 1, 1 - slot):
                cp.start()

        for cp in copies(j, slot):
            cp.wait()
        off = pl.multiple_of(j * FF_CHUNK, FF_CHUNK)
        wg_ref[:, pl.ds(off, FF_CHUNK)] = (0.5 * stage_cols[0, slot]).astype(BF16)
        wu_ref[:, pl.ds(off, FF_CHUNK)] = stage_cols[1, slot].astype(BF16)
        wd_ref[pl.ds(off, FF_CHUNK), :] = stage_rows[slot].astype(BF16)
        on_chunk(off)
        return carry

    lax.fori_loop(0, n_chunks, body, 0)


def _ffn_kernel(h_ref, meta_ref, pre_ref, post_ref, wg_hbm, wu_hbm, wd_hbm, o_ref,
                a_ref, z_ref, wg_ref, wu_ref, wd_ref, stage_cols, stage_rows, sems, *,
                step0, layer):
    step = pl.program_id(0)

    def hidden_chunk(rows, cols):
        a = a_ref[rows, :]
        half_g = jnp.dot(a, wg_ref[:, cols], preferred_element_type=F32)
        u = jnp.dot(a, wu_ref[:, cols], preferred_element_type=F32)
        z_ref[rows, cols] = (half_g * (1.0 + jnp.tanh(half_g)) * u).astype(BF16)

    def finish(rows, src_ref):
        f = jnp.dot(z_ref[rows, :], wd_ref[...], preferred_element_type=F32)
        o_ref[rows, :] = src_ref[rows, :] + _rms(f, FFN_RESID * post_ref[...])

    @pl.when(step == 0)
    def _():
        rows = slice(0, TILE) if step0 == "tokens" else slice(TILE - META_ROWS, TILE)
        src_ref = meta_ref if step0 == "meta_input" else h_ref
        a_ref[rows, :] = _rms(src_ref[rows, :], pre_ref[...]).astype(BF16)
        _load_ffn_weights(layer, wg_hbm, wu_hbm, wd_hbm, wg_ref, wu_ref, wd_ref, stage_cols,
                          stage_rows, sems,
                          lambda off: hidden_chunk(rows, pl.ds(off, FF_CHUNK)))
        if step0 != "tokens":
            o_ref[0:TILE - META_ROWS, :] = jnp.zeros((TILE - META_ROWS, D_MODEL), F32)
        finish(rows, src_ref)

    @pl.when(step > 0)
    def _():
        a_ref[...] = _rms(h_ref[...], pre_ref[...]).astype(BF16)
        for j in range(D_FF // FF_CHUNK):
            hidden_chunk(slice(0, TILE), slice(j * FF_CHUNK, (j + 1) * FF_CHUNK))
        finish(slice(0, TILE), h_ref)


def _resident(shape, layer=None):
    if layer is None:
        return pl.BlockSpec(shape, lambda *_: (0,) * len(shape), pipeline_mode=pl.Buffered(1))
    return pl.BlockSpec((None,) + tuple(shape), lambda *_: (layer,) + (0,) * len(shape),
                        pipeline_mode=pl.Buffered(1))


def _ffn(src, meta_tile, gains, wg, wu, wd, *, gain_rows, mode, layer):
    tile = TILE
    n_tiles = src.shape[0] // tile
    assert src.shape[0] % tile == 0
    if mode == "first":
        grid, src_map = n_tiles + 1, lambda i: (jnp.maximum(i - 1, 0), 0)
    elif mode == "mid":
        grid, src_map = n_tiles, lambda i: (i, 0)
    else:
        grid, src_map = n_tiles - 1, lambda i: (i + 1, 0)
    out_rows = grid * tile
    return pl.pallas_call(
        functools.partial(
            _ffn_kernel, layer=layer,
            step0={"first": "meta_input", "mid": "meta_block", "last": "tokens"}[mode]),
        grid=(grid,),
        in_specs=[
            pl.BlockSpec((tile, D_MODEL), src_map),
            _resident((TILE, D_MODEL)),
            _resident((1, D_MODEL), gain_rows[0]),
            _resident((1, D_MODEL), gain_rows[1]),
            pl.BlockSpec(memory_space=pl.ANY),
            pl.BlockSpec(memory_space=pl.ANY),
            pl.BlockSpec(memory_space=pl.ANY),
        ],
        out_specs=pl.BlockSpec((tile, D_MODEL), lambda i: (i, 0)),
        out_shape=jax.ShapeDtypeStruct((out_rows, D_MODEL), F32),
        scratch_shapes=[
            pltpu.VMEM((tile, D_MODEL), BF16),
            pltpu.VMEM((tile, D_FF), BF16),
            pltpu.VMEM((D_MODEL, D_FF), BF16),
            pltpu.VMEM((D_MODEL, D_FF), BF16),
            pltpu.VMEM((D_FF, D_MODEL), BF16),
            pltpu.VMEM((2, 2, D_MODEL, FF_CHUNK), F32),
            pltpu.VMEM((2, FF_CHUNK, D_MODEL), F32),
            pltpu.SemaphoreType.DMA((3, 2)),
        ],
        compiler_params=pltpu.CompilerParams(
            dimension_semantics=("arbitrary",), vmem_limit_bytes=VMEM_LIMIT),
        name="ffn_" + mode,
    )(src, meta_tile, gains, gains, wg, wu, wd)


def _mixer_kernel(h_ref, pre_ref, post_ref, win_ref, wout_ref, poolw_ref, rows_ref,
                  w2_ref, gnorm_ref, o_ref,
                  a_s, pbuf, zbuf, b_s, q_s, k_s, rep_s, qe_s, kd_s, lq_s, lk_s, v_s, gate_s,
                  o_s, u_s, sb_s, y_s, state_s, pmeta, zmeta, smeta, bmin_s, wout_s, win_s, *,
                  tiles_per_seq):
    step = pl.program_id(0)
    ts = TILE
    is_meta = step == 0

    @pl.when(is_meta)
    def _():
        wout_s[...] = wout_ref[...].astype(BF16)
        win_s[:, 0:OFF_OG] = win_ref[:, 0:OFF_OG].astype(BF16)
        win_s[:, OFF_OG:OFF_GL] = win_ref[:, OFF_OG + GLA_GATE_RANK:D_PROJ].astype(BF16)
        low_rank = win_ref[:, OFF_OG:OFF_OG + GL_PAD]
        keep = lax.broadcasted_iota(jnp.int32, (1, GL_PAD), 1) < GLA_GATE_RANK
        win_s[:, OFF_GL:D_PROJ_PACKED] = jnp.where(keep, low_rank, 0.0).astype(BF16)
        pbuf[0:POOL_HIST, :] = jnp.zeros((POOL_HIST, POOL_WIDTH), F32)
        zbuf[0:CONV_HIST, :] = jnp.zeros((CONV_HIST, CONV_WIDTH), F32)
        state_s[...] = jnp.zeros_like(state_s)

    @pl.when(jnp.logical_and(step > 0, lax.rem(step - 1, tiles_per_seq) == 0))
    def _():
        pbuf[0:POOL_HIST, :] = pmeta[...]
        zbuf[0:CONV_HIST, :] = zmeta[...]
        state_s[...] = smeta[...]

    nc = ts // CHUNK
    ri = lax.broadcasted_iota(jnp.int32, (CHUNK, CHUNK), 0)
    ci = lax.broadcasted_iota(jnp.int32, (CHUNK, CHUNK), 1)

    def proj(off, width):
        return jnp.dot(a_s[...], win_s[:, off:off + width], preferred_element_type=F32)

    def pool_mixer():
        x1 = pbuf[...]
        s2 = x1 + pltpu.roll(x1, 1, 0)
        s4 = s2 + pltpu.roll(s2, 2, 0)
        s8 = s4 + pltpu.roll(s4, 4, 0)
        s16 = s8 + pltpu.roll(s8, 8, 0)
        lane = lax.broadcasted_iota(jnp.int32, (ts, POOL_WIDTH), 1)
        row = lax.broadcasted_iota(jnp.int32, (ts, POOL_WIDTH), 0)
        win = jnp.where(lane < 64, 2, jnp.where(lane < 128, 4, jnp.where(lane < 192, 8, 16)))
        wsum = jnp.where(lane < 64, s2[POOL_HIST:], jnp.where(
            lane < 128, s4[POOL_HIST:], jnp.where(lane < 192, s8[POOL_HIST:], s16[POOL_HIST:])))
        pos1 = jnp.where(is_meta, row - (META_PAD - 1), max(POOL_WINDOWS))
        cnt = jnp.clip(pos1, 1, win).astype(F32)
        m = (wsum / cnt - x1[POOL_HIST:]).astype(BF16)
        y_pool = (jnp.dot(m, poolw_ref[...], preferred_element_type=F32)
                  * rows_ref[ROW_POOL_SCALE:ROW_POOL_SCALE + 1, :])
        y_s[:, 0:POOL_WIDTH] = y_pool.astype(BF16)
        pbuf[0:POOL_HIST, :] = x1[ts:ts + POOL_HIST]

    def conv_mixer():
        zz = zbuf[...]
        z1 = pltpu.roll(zz, 1, 0)
        z2 = pltpu.roll(zz, 2, 0)
        cw = rows_ref[ROW_CONV:ROW_CONV + CONV_K, :]
        yc = (cw[0:1, :] * z2[CONV_HIST:] + cw[1:2, :] * z1[CONV_HIST:]
              + cw[2:3, :] * zz[CONV_HIST:])
        y_s[:, POOL_WIDTH:POOL_WIDTH + CONV_WIDTH] = (proj(OFF_CB, CONV_WIDTH) * yc).astype(BF16)
        zbuf[0:CONV_HIST, :] = zz[ts:ts + CONV_HIST]

    def project_and_mix_local():
        a_s[...] = _rms(h_ref[...], pre_ref[...]).astype(BF16)
        g_low = proj(OFF_GL, GL_PAD).astype(BF16)
        gx = (jnp.dot(g_low, w2_ref[...], preferred_element_type=F32)
              + rows_ref[ROW_GATE_BIAS:ROW_GATE_BIAS + 1, :])
        log_sig = jnp.minimum(gx, 0.0) - jnp.log(1.0 + jnp.exp(-jnp.abs(gx)))
        la = log_sig * (LOG2_E / GLA_TAU)
        la_hi = la.astype(BF16)
        la_lo = (la - la_hi.astype(F32)).astype(BF16)
        la_cat = jnp.concatenate([la_hi, la_lo], axis=1)
        pbuf[POOL_HIST:, :] = proj(OFF_POOL, POOL_WIDTH)
        zbuf[CONV_HIST:, :] = proj(OFF_CC, CONV_WIDTH) * proj(OFF_CU, CONV_WIDTH)
        tril = jnp.where(ri >= ci, 1.0, 0.0).astype(BF16)
        for c in range(nc):
            rows = slice(c * CHUNK, (c + 1) * CHUNK)
            bb = jnp.dot(tril, la_cat[rows, :], preferred_element_type=F32)
            b_s[rows, :] = bb[:, :GLA_KW] + bb[:, GLA_KW:]
        q_s[...] = proj(OFF_Q, GLA_KW) * (GLA_DK ** -0.5)
        k_s[...] = proj(OFF_K, GLA_KW)
        pool_mixer()
        conv_mixer()
        v_s[...] = proj(OFF_V, GLA_WIDTH).astype(BF16)
        og = proj(OFF_OG, GLA_WIDTH)
        gate_s[...] = og * jax.nn.sigmoid(og) * gnorm_ref[...]
        bmin_s[0] = jnp.min(b_s[...])

    project_and_mix_local()
    b, q, k = b_s[...], q_s[...], k_s[...]
    b3 = b.reshape(nc, CHUNK, GLA_KW)
    b_last = b3[:, CHUNK - 1:CHUNK, :]
    qe_s[...] = (q.reshape(b3.shape) * jnp.exp2(b3)).reshape(ts, GLA_KW).astype(BF16)
    kd_s[...] = (k.reshape(b3.shape) * jnp.exp2(b_last - b3)).reshape(ts, GLA_KW).astype(BF16)
    decay = jnp.exp2(b_last)

    nt = (((1,), (1,)), ((), ()))
    tn = (((0,), (0,)), ((), ()))
    ksl = [slice(hd * GLA_DK, (hd + 1) * GLA_DK) for hd in range(GLA_HEADS)]
    vsl = [slice(hd * GLA_DV, (hd + 1) * GLA_DV) for hd in range(GLA_HEADS)]
    pairs = [(c, hd) for c in range(nc) for hd in range(GLA_HEADS)]
    group = 8 * GLA_HEADS
    chunk_rows = lambda c: slice(c * CHUNK, (c + 1) * CHUNK)

    single_ref_ok = bmin_s[0] > -SINGLE_REF_MAX_LOG2

    def state_increments(grp):
        return [lax.dot_general(v_s[chunk_rows(c), vsl[hd]], kd_s[chunk_rows(c), ksl[hd]], tn,
                                preferred_element_type=F32) for c, hd in grp]

    def finish_group(grp, ps, us):
        pvs = [jnp.dot(p, v_s[chunk_rows(c), vsl[hd]], preferred_element_type=F32)
               for p, (c, hd) in zip(ps, grp)]
        for pv, u, (c, hd) in zip(pvs, us, grp):
            o_s[chunk_rows(c), vsl[hd]] = pv
            u_s[c * GLA_HEADS + hd] = u

    @pl.when(single_ref_ok)
    def _():
        lk_s[0] = (k.reshape(b3.shape) * jnp.exp2(-b3)).reshape(ts, GLA_KW).astype(BF16)
        causal = ri >= ci
        for g0 in range(0, len(pairs), group):
            grp = pairs[g0:g0 + group]
            scs = [lax.dot_general(qe_s[chunk_rows(c), ksl[hd]], lk_s[0, chunk_rows(c), ksl[hd]],
                                   nt, preferred_element_type=F32) for c, hd in grp]
            us = state_increments(grp)
            finish_group(grp, [jnp.where(causal, sc, 0.0).astype(BF16) for sc in scs], us)

    lane_hi = lax.broadcasted_iota(jnp.int32, (1, 1, 128), 2) >= GLA_DK
    t64 = lax.broadcasted_iota(jnp.int32, (1, CHUNK, 1), 1)
    blk16, blk4, pos4 = t64 // 16, (t64 // 4) % 4, t64 % 4
    in_hi4 = lax.broadcasted_iota(jnp.int32, (1, 8, 1), 1) >= 4
    neg = -1e30

    def replicate_heads():
        for n, src in enumerate((b_s, q_s, k_s)):
            for c2 in range(2):
                x = src[:, 128 * c2:128 * c2 + 128]
                x_sw = pltpu.roll(x, GLA_DK, 1)
                rep_s[GLA_HEADS * n + 2 * c2] = jnp.where(lane_hi[0], x_sw, x)
                rep_s[GLA_HEADS * n + 2 * c2 + 1] = jnp.where(lane_hi[0], x, x_sw)

    def build_level_operands(hd):
        rb, rq, rk = (rep_s[GLA_HEADS * n + hd] for n in range(3))
        rb64, rq64, rk64 = (x.reshape(nc, CHUNK, 128) for x in (rb, rq, rk))
        rb16 = rb.reshape(ts // 16, 16, 128)
        rb8 = rb.reshape(ts // 8, 8, 128)
        to64 = lambda x: x.reshape(nc, CHUNK, 128)
        end4 = jnp.where(in_hi4, rb8[:, 7:8, :], rb8[:, 3:4, :])
        k1 = rk64 * to64(jnp.exp2(rb16[:, 15:16, :] - rb16))
        k2 = rk64 * to64(jnp.exp2(end4 - rb8))
        nxt = jnp.where(lane_hi, pltpu.roll(rb8, 7, 1), rb8)
        for p in range(2):
            col = slice(128 * p, 128 * p + 128)
            j = jnp.where(lane_hi, 2 * p + 1, 2 * p)
            ref1 = jnp.where(lane_hi, rb64[:, 32 * p + 31:32 * p + 32, :],
                             rb64[:, 32 * p + 15:32 * p + 16, :])
            ref2 = jnp.where(lane_hi, rb16[:, 8 * p + 7:8 * p + 8, :], rb16[:, 8 * p + 3:8 * p + 4, :])
            ref3 = jnp.where(in_hi4, nxt[:, 4 + 2 * p:5 + 2 * p, :], nxt[:, 2 * p:2 * p + 1, :])
            q1 = rq64 * jnp.exp2(jnp.where(blk16 > j, rb64 - ref1, neg))
            q2 = rq64 * jnp.exp2(jnp.where(blk4 > j, to64(rb16 - ref2), neg))
            q3 = rq64 * jnp.exp2(jnp.where(pos4 >= j, to64(rb8 - ref3), neg))
            for n, (ql, kl) in enumerate(((q1, jnp.where(blk16 == j, k1, 0.0)),
                                          (q2, jnp.where(blk4 == j, k2, 0.0)),
                                          (q3, jnp.where(pos4 == j, rk64, 0.0)))):
                lq_s[3 * hd + n, :, col] = ql.reshape(ts, 128).astype(BF16)
                lk_s[3 * hd + n, :, col] = kl.reshape(ts, 128).astype(BF16)

    @pl.when(jnp.logical_not(single_ref_ok))
    def _():
        replicate_heads()
        for hd in range(GLA_HEADS):
            build_level_operands(hd)
        same16 = (ri // 16) == (ci // 16)
        same4 = (ri // 4) == (ci // 4)
        for g0 in range(0, len(pairs), group):
            grp = pairs[g0:g0 + group]
            lvl = [[lax.dot_general(lq_s[3 * hd + n, chunk_rows(c), :],
                                    lk_s[3 * hd + n, chunk_rows(c), :], nt,
                                    preferred_element_type=F32) for n in range(3)]
                   for c, hd in grp]
            us = state_increments(grp)
            finish_group(grp, [(g1 + jnp.where(same16, g2, 0.0)
                                + jnp.where(same4, g3, 0.0)).astype(BF16) for g1, g2, g3 in lvl], us)


    sts = [state_s[hd] for hd in range(GLA_HEADS)]
    for c in range(nc):
        for hd in range(GLA_HEADS):
            sb_s[c * GLA_HEADS + hd] = sts[hd].astype(BF16)
        sts = [sts[hd] * decay[c][:, ksl[hd]] + u_s[c * GLA_HEADS + hd]
               for hd in range(GLA_HEADS)]
    for hd in range(GLA_HEADS):
        state_s[hd] = sts[hd]

    for g0 in range(0, len(pairs), group):
        grp = pairs[g0:g0 + group]
        inters = [lax.dot_general(qe_s[c * CHUNK:(c + 1) * CHUNK, ksl[hd]],
                                  sb_s[c * GLA_HEADS + hd], nt, preferred_element_type=F32)
                  for c, hd in grp]
        for inter, (c, hd) in zip(inters, grp):
            o_s[chunk_rows(c), vsl[hd]] = o_s[chunk_rows(c), vsl[hd]] + inter

    for hd in range(GLA_HEADS):
        vs = slice(hd * GLA_DV, (hd + 1) * GLA_DV)
        o = o_s[:, vs]
        o = o * lax.rsqrt(jnp.mean(o * o, axis=-1, keepdims=True) + EPS)
        y_s[:, POOL_WIDTH + CONV_WIDTH + hd * GLA_DV:POOL_WIDTH + CONV_WIDTH + (hd + 1) * GLA_DV] = (
            o * gate_s[:, vs]).astype(BF16)

    mix = jnp.dot(y_s[...], wout_s[...], preferred_element_type=F32)
    o_ref[...] = h_ref[...] + _rms(mix, post_ref[...])

    @pl.when(is_meta)
    def _():
        pmeta[...] = pbuf[0:POOL_HIST, :]
        zmeta[...] = zbuf[0:CONV_HIST, :]
        smeta[...] = state_s[...]


def _mixer(h, tiles_per_seq, gains, win, wout, poolw, small_rows, w2, gnorm, *, gain_rows,
           layer):
    ts = TILE
    return pl.pallas_call(
        functools.partial(_mixer_kernel, tiles_per_seq=tiles_per_seq),
        grid=(h.shape[0] // ts,),
        in_specs=[
            pl.BlockSpec((ts, D_MODEL), lambda i: (i, 0)),
            _resident((1, D_MODEL), gain_rows[0]),
            _resident((1, D_MODEL), gain_rows[1]),
            _resident((D_MODEL, D_PROJ), layer),
            _resident((D_MODEL, D_MODEL), layer),
            _resident((POOL_WIDTH, POOL_WIDTH), layer),
            _resident((SMALL_ROWS, GLA_KW), layer),
            _resident((GL_PAD, GLA_KW), layer),
            _resident((1, GLA_WIDTH), layer),
        ],
        out_specs=pl.BlockSpec((ts, D_MODEL), lambda i: (i, 0)),
        out_shape=jax.ShapeDtypeStruct(h.shape, F32),
        scratch_shapes=[
            pltpu.VMEM((ts, D_MODEL), BF16),
            pltpu.VMEM((POOL_HIST + ts, POOL_WIDTH), F32),
            pltpu.VMEM((CONV_HIST + ts, CONV_WIDTH), F32),
            pltpu.VMEM((ts, GLA_KW), F32),
            pltpu.VMEM((ts, GLA_KW), F32),
            pltpu.VMEM((ts, GLA_KW), F32),
            pltpu.VMEM((3 * GLA_HEADS, ts, 128), F32),
            pltpu.VMEM((ts, GLA_KW), BF16),
            pltpu.VMEM((ts, GLA_KW), BF16),
            pltpu.VMEM((3 * GLA_HEADS, ts, 4 * GLA_DK), BF16),
            pltpu.VMEM((3 * GLA_HEADS, ts, 4 * GLA_DK), BF16),
            pltpu.VMEM((ts, GLA_WIDTH), BF16),
            pltpu.VMEM((ts, GLA_WIDTH), F32),
            pltpu.VMEM((ts, GLA_WIDTH), F32),
            pltpu.VMEM((ts // CHUNK * GLA_HEADS, GLA_DV, GLA_DK), F32),
            pltpu.VMEM((ts // CHUNK * GLA_HEADS, GLA_DV, GLA_DK), BF16),
            pltpu.VMEM((ts, D_MODEL), BF16),
            pltpu.VMEM((GLA_HEADS, GLA_DV, GLA_DK), F32),
            pltpu.VMEM((POOL_HIST, POOL_WIDTH), F32),
            pltpu.VMEM((CONV_HIST, CONV_WIDTH), F32),
            pltpu.VMEM((GLA_HEADS, GLA_DV, GLA_DK), F32),
            pltpu.SMEM((1,), F32),
            pltpu.VMEM((D_MODEL, D_MODEL), BF16),
            pltpu.VMEM((D_MODEL, D_PROJ_PACKED), BF16),
        ],
        compiler_params=pltpu.CompilerParams(
            dimension_semantics=("arbitrary",), vmem_limit_bytes=VMEM_LIMIT),
        name="mixer",
    )(h, gains, gains, win, wout, poolw, small_rows, w2, gnorm)


def _pad_to(v, axis, size):
    pads = [(0, 0)] * v.ndim
    pads[axis] = (0, size - v.shape[axis])
    return jnp.pad(v, pads)


def _block_diag(w):
    out = jnp.zeros((w.shape[0], POOL_WIDTH, POOL_WIDTH), w.dtype)
    for g in range(len(POOL_WINDOWS)):
        sl = slice(g * POOL_GROUP, (g + 1) * POOL_GROUP)
        out = out.at[:, sl, sl].set(w[:, g])
    return out


def kernel(x, meta, ffn1_pre, ffn1_post, ffn1_wg, ffn1_wu, ffn1_wd, mix_pre, mix_post, w_in,
           pool_w, pool_scale, conv_w, gla_w2, gla_b2, gla_norm, w_out,
           ffn2_pre, ffn2_post, ffn2_wg, ffn2_wu, ffn2_wd):
    n_batch, seq, d = x.shape
    depth = w_in.shape[0]
    assert d == D_MODEL and seq % TILE == 0 and meta.shape[0] == N_META
    meta_tile = jnp.pad(meta.astype(F32), ((META_PAD, 0), (0, 0)))
    h = x.reshape(n_batch * seq, d)
    rows = lambda v: v.reshape(depth, 1, -1).astype(F32)
    assert (ROW_POOL_SCALE, ROW_GATE_BIAS, ROW_CONV) == (0, 1, 2)
    small_rows = _pad_to(
        jnp.concatenate([rows(pool_scale), rows(gla_b2), conv_w.astype(F32)], axis=1), 1, SMALL_ROWS)
    mixer_small = (_block_diag(pool_w).astype(BF16), small_rows,
                   _pad_to(gla_w2, 1, GL_PAD).astype(BF16), rows(gla_norm))
    gains = jnp.stack([ffn1_pre, ffn1_post, mix_pre, mix_post, ffn2_pre, ffn2_post]).astype(F32)
    gains = gains.reshape(6 * depth, 1, d)
    for l in range(depth):
        h = _ffn(h, meta_tile, gains, ffn1_wg, ffn1_wu, ffn1_wd, gain_rows=(l, depth + l),
                 mode="first" if l == 0 else "mid", layer=l)
        h = _mixer(h, seq // TILE, gains, w_in, w_out, *mixer_small,
                   gain_rows=(2 * depth + l, 3 * depth + l), layer=l)
        h = _ffn(h, meta_tile, gains, ffn2_wg, ffn2_wu, ffn2_wd,
                 gain_rows=(4 * depth + l, 5 * depth + l),
                 mode="last" if l == depth - 1 else "mid", layer=l)
    return h.reshape(n_batch, seq, d)
```
